```python
import jax, jax.numpy as jnp
from jax import lax
import numpy as np

D_MODEL = 1024
BATCH = 2
SEQ = 8192
DEPTH = 2

GRID_W = 64
CTX_LEN = 256
HEAD_DIM = 64
FNET_GROUPS = 8
FNET_WIDTH = FNET_GROUPS * HEAD_DIM
NA_HEADS = 8
NA_WIDTH = NA_HEADS * HEAD_DIM
NA_KR_MAX = 8
NA_KC = 16
NA_QB = 16
NA_KB = 32
GQA_Q_HEADS = 16
GQA_KV_HEADS = 4
GQA_WINDOW = 128
GQA_BLOCK = 128
ROPE_THETA = 10000.0
PEER_HEADS = 8
PEER_NKEYS = 128
PEER_EXPERTS = PEER_NKEYS * PEER_NKEYS
PEER_QDIM = 256
PEER_TOPK = 16
PEER_TOKEN_CHUNK = 128
N_EVEN = (DEPTH + 1) // 2
N_ODD = DEPTH // 2
DN_ALPHA = (2 * DEPTH) ** 0.25
DN_BETA = (8 * DEPTH) ** -0.25
LN_EPS = 1e-6
NEG = -1e30

kernel_name = "hybrid_fnet_natten_swa_peer_dit"


def _layer_norm(x, gain=None, bias=None):
    xf = x.astype(jnp.float32)
    mu = jnp.mean(xf, -1, keepdims=True)
    var = jnp.mean(jnp.square(xf - mu), -1, keepdims=True)
    y = (xf - mu) * lax.rsqrt(var + LN_EPS)
    if gain is not None:
        y = y * gain.astype(jnp.float32) + bias.astype(jnp.float32)
    return y.astype(x.dtype)


def _modulate(x, shift, scale):
    return _layer_norm(x) * (1 + scale) + shift


def _heads(t, n_heads):
    b, l, _ = t.shape
    return t.reshape(b, l, n_heads, HEAD_DIM)


def _axial_rope_tables(n_tokens):
    t = jnp.arange(n_tokens)
    row = (t // GRID_W).astype(jnp.float32)
    col = (t % GRID_W).astype(jnp.float32)
    n_freq = HEAD_DIM // 4
    inv_freq = ROPE_THETA ** (-jnp.arange(n_freq, dtype=jnp.float32) / n_freq)
    ang = jnp.concatenate([row[:, None] * inv_freq, col[:, None] * inv_freq], -1)
    return jnp.cos(ang), jnp.sin(ang)


def _apply_rope(x, cos, sin):
    x1, x2 = jnp.split(x.astype(jnp.float32), 2, axis=-1)
    cs = cos[None, :, None, :]
    sn = sin[None, :, None, :]
    return jnp.concatenate([x1 * cs - x2 * sn, x1 * sn + x2 * cs], -1).astype(x.dtype)


def _fourier_mix(u):
    b, l, _ = u.shape
    ug = u.astype(jnp.float32).reshape(b, l, FNET_GROUPS, HEAD_DIM)
    y = jnp.fft.fft2(ug, axes=(1, 3), norm="ortho").real
    return y.reshape(b, l, FNET_WIDTH).astype(u.dtype)


def _ctx_attention(q, k, v, sink=None):
    b, l, hq, dh = q.shape
    hkv = k.shape[2]
    g = hq // hkv
    qg = q.reshape(b, l, hkv, g, dh)
    s = jnp.einsum('bqhgd,bkhd->bhgqk', qg, k, preferred_element_type=jnp.float32) * (dh ** -0.5)
    if sink is not None:
        sk = jnp.broadcast_to(sink.reshape(hkv, g)[None, :, :, None, None].astype(jnp.float32), s.shape[:-1] + (1,))
        p = jax.nn.softmax(jnp.concatenate([s, sk], -1), -1)[..., :l]
    else:
        p = jax.nn.softmax(s, -1)
    o = jnp.einsum('bhgqk,bkhd->bqhgd', p.astype(v.dtype), v)
    return o.reshape(b, l, hq * dh)


def _neighbourhood_attention(q, k, v, k_ctx, v_ctx, rpb):
    b, s, h, dh = q.shape
    rows = s // GRID_W
    kr = min(NA_KR_MAX, rows)
    n_cb = GRID_W // NA_QB
    r = jnp.arange(rows)
    r_start = jnp.clip(r - kr // 2, 0, rows - kr)
    row_idx = r_start[:, None] + jnp.arange(kr)[None]
    c_start = jnp.clip(jnp.arange(GRID_W) - NA_KC // 2, 0, GRID_W - NA_KC)
    blk_start = jnp.clip(jnp.arange(n_cb) * NA_QB - NA_KC // 2, 0, GRID_W - NA_KB)
    blk_col = blk_start[:, None] + jnp.arange(NA_KB)[None]
    q_col = jnp.arange(n_cb)[:, None] * NA_QB + jnp.arange(NA_QB)[None]
    qc_start = c_start[q_col]
    kc = blk_col[:, None, :]
    valid = (kc >= qc_start[:, :, None]) & (kc < qc_start[:, :, None] + NA_KC)
    dr_i = row_idx - r[:, None] + (NA_KR_MAX - 1)
    dc_i = jnp.clip(kc - q_col[:, :, None], -(NA_KC - 1), NA_KC - 1) + (NA_KC - 1)
    bias = rpb[:, dr_i[:, None, None, :, None], dc_i[None, :, :, None, :]].astype(jnp.float32)

    scale = dh ** -0.5
    qg = q.reshape(b, rows, n_cb, NA_QB, h, dh)
    kg = k.reshape(b, rows, GRID_W, h, dh)
    vg = v.reshape(b, rows, GRID_W, h, dh)
    gi_r = row_idx[:, :, None, None]
    gi_c = blk_col[None, None, :, :]
    k_blk = kg[:, gi_r, gi_c]
    v_blk = vg[:, gi_r, gi_c]
    s_loc = jnp.einsum('brjqhd,brkjchd->bhrjqkc', qg, k_blk, preferred_element_type=jnp.float32) * scale
    s_loc = jnp.where(valid[:, :, None, :], s_loc + bias[None], NEG)
    s_ctx = jnp.einsum('brjqhd,blhd->bhrjql', qg, k_ctx, preferred_element_type=jnp.float32) * scale
    n_loc = kr * NA_KB
    logits = jnp.concatenate([s_loc.reshape(s_loc.shape[:5] + (n_loc,)), s_ctx], -1)
    p = jax.nn.softmax(logits, -1).astype(v.dtype)
    p_loc = p[..., :n_loc].reshape(s_loc.shape)
    p_ctx = p[..., n_loc:]
    o = (jnp.einsum('bhrjqkc,brkjchd->brjqhd', p_loc, v_blk)
         + jnp.einsum('bhrjql,blhd->brjqhd', p_ctx, v_ctx))
    return o.reshape(b, s, h * dh)


def _window_gqa(q, k, v, k_ctx, v_ctx, sink):
    b, s, hq, dh = q.shape
    hkv = k.shape[2]
    g = hq // hkv
    nb = s // GQA_BLOCK
    qb = q.reshape(b, nb, GQA_BLOCK, hkv, g, dh)

    def band(t):
        tp = jnp.pad(t, ((0, 0), (GQA_BLOCK, GQA_BLOCK), (0, 0), (0, 0)))
        tb = tp.reshape(b, nb + 2, GQA_BLOCK, hkv, dh)
        return jnp.concatenate([tb[:, :-2], tb[:, 1:-1], tb[:, 2:]], axis=2)

    kb, vb = band(k), band(v)
    q_pos = jnp.arange(s).reshape(nb, GQA_BLOCK)
    k_pos = (jnp.arange(nb)[:, None] - 1) * GQA_BLOCK + jnp.arange(3 * GQA_BLOCK)[None]
    kp = k_pos[:, None, :]
    valid = (jnp.abs(q_pos[:, :, None] - kp) <= GQA_WINDOW) & (kp >= 0) & (kp < s)
    scale = dh ** -0.5
    s_loc = jnp.einsum('bnqhgd,bnkhd->bnhgqk', qb, kb, preferred_element_type=jnp.float32) * scale
    s_loc = jnp.where(valid[None, :, None, None], s_loc, NEG)
    s_ctx = jnp.einsum('bnqhgd,blhd->bnhgql', qb, k_ctx, preferred_element_type=jnp.float32) * scale
    sk = jnp.broadcast_to(sink.reshape(hkv, g)[None, None, :, :, None, None].astype(jnp.float32),
                          s_loc.shape[:-1] + (1,))
    n_loc = 3 * GQA_BLOCK
    n_ctx = k_ctx.shape[1]
    p = jax.nn.softmax(jnp.concatenate([s_loc, s_ctx, sk], -1), -1).astype(v.dtype)
    o = (jnp.einsum('bnhgqk,bnkhd->bnqhgd', p[..., :n_loc], vb)
         + jnp.einsum('bnhgql,blhd->bnqhgd', p[..., n_loc:n_loc + n_ctx], v_ctx))
    return o.reshape(b, s, hq * dh)


def _even_mixer(h_lat, h_ctx, w_in, w_out, rpb, ctx_out):
    split_at = [FNET_WIDTH, FNET_WIDTH + NA_WIDTH, FNET_WIDTH + 2 * NA_WIDTH]
    f_l, q_l, k_l, v_l = jnp.split(h_lat @ w_in, split_at, -1)
    if ctx_out:
        f_c, q_c, k_c, v_c = jnp.split(h_ctx @ w_in, split_at, -1)
    else:
        k_c, v_c = jnp.split(h_ctx @ w_in[:, FNET_WIDTH + NA_WIDTH:], [NA_WIDTH], -1)
    k_c, v_c = _heads(k_c, NA_HEADS), _heads(v_c, NA_HEADS)
    na_l = _neighbourhood_attention(_heads(q_l, NA_HEADS), _heads(k_l, NA_HEADS), _heads(v_l, NA_HEADS), k_c, v_c, rpb)
    y_lat = jnp.concatenate([_fourier_mix(f_l), na_l], -1) @ w_out
    y_ctx = None
    if ctx_out:
        na_c = _ctx_attention(_heads(q_c, NA_HEADS), k_c, v_c)
        y_ctx = jnp.concatenate([_fourier_mix(f_c), na_c], -1) @ w_out
    return y_lat, y_ctx


def _odd_mixer(h_lat, h_ctx, w_in, w_out, sink, cos, sin, ctx_out):
    qw = GQA_Q_HEADS * HEAD_DIM
    kvw = GQA_KV_HEADS * HEAD_DIM
    q_l, k_l, v_l = jnp.split(h_lat @ w_in, [qw, qw + kvw], -1)
    q_l = _apply_rope(_heads(q_l, GQA_Q_HEADS), cos, sin)
    k_l = _apply_rope(_heads(k_l, GQA_KV_HEADS), cos, sin)
    if ctx_out:
        q_c, k_c, v_c = jnp.split(h_ctx @ w_in, [qw, qw + kvw], -1)
    else:
        k_c, v_c = jnp.split(h_ctx @ w_in[:, qw:], [kvw], -1)
    k_c, v_c = _heads(k_c, GQA_KV_HEADS), _heads(v_c, GQA_KV_HEADS)
    y_lat = _window_gqa(q_l, k_l, _heads(v_l, GQA_KV_HEADS), k_c, v_c, sink) @ w_out
    y_ctx = None
    if ctx_out:
        y_ctx = _ctx_attention(_heads(q_c, GQA_Q_HEADS), k_c, v_c, sink) @ w_out
    return y_lat, y_ctx


def _peer(h, w_q, sub_keys, u_tab, v_tab):
    b, l, d = h.shape
    q = (h @ w_q).reshape(b, l, PEER_HEADS, 2, PEER_QDIM // 2)
    s_half = jnp.einsum('blhpd,hpnd->blhpn', q, sub_keys, preferred_element_type=jnp.float32)
    top_s, top_i = lax.top_k(s_half, PEER_TOPK)
    cand_s = (top_s[..., 0, :, None] + top_s[..., 1, None, :]).reshape(b, l, PEER_HEADS, PEER_TOPK * PEER_TOPK)
    cand_i = (top_i[..., 0, :, None] * PEER_NKEYS + top_i[..., 1, None, :]).reshape(b, l, PEER_HEADS, PEER_TOPK * PEER_TOPK)
    best_s, best_pos = lax.top_k(cand_s, PEER_TOPK)
    expert = jnp.take_along_axis(cand_i, best_pos, -1)
    gate = jax.nn.softmax(best_s, -1).astype(h.dtype)
    n_chunks = (b * l) // PEER_TOKEN_CHUNK
    hk = PEER_HEADS * PEER_TOPK
    hc = h.reshape(n_chunks, PEER_TOKEN_CHUNK, d)
    ec = expert.reshape(n_chunks, PEER_TOKEN_CHUNK, hk)
    gc = gate.reshape(n_chunks, PEER_TOKEN_CHUNK, hk)

    def chunk(args):
        hx, ex, gx = args
        a = jnp.einsum('tkd,td->tk', u_tab[ex], hx)
        w = gx * jax.nn.gelu(a, approximate=False)
        return jnp.einsum('tk,tkd->td', w, v_tab[ex])

    return lax.map(chunk, (hc, ec, gc)).reshape(b, l, d)


def setup_inputs(seed: int = 0) -> dict:
    key = jax.random.key(seed)
    ks = jax.random.split(key, 22)
    d = D_MODEL

    def nrm(k, shape, scale):
        return jax.random.normal(k, shape, jnp.float32) * scale

    even_in = FNET_WIDTH + 3 * NA_WIDTH
    even_mix = FNET_WIDTH + NA_WIDTH
    odd_in = (GQA_Q_HEADS + 2 * GQA_KV_HEADS) * HEAD_DIM
    odd_mix = GQA_Q_HEADS * HEAD_DIM
    return {
        "x": nrm(ks[0], (BATCH, SEQ, d), 1.0),
        "c": nrm(ks[1], (BATCH, d), 1.0),
        "ctx": nrm(ks[2], (BATCH, CTX_LEN, d), 1.0),
        "c_ctx": nrm(ks[3], (d,), 1.0),
        "ada_w": nrm(ks[4], (DEPTH, d, 6 * d), d ** -0.5),
        "ada_b": nrm(ks[5], (DEPTH, 6 * d), 0.01),
        "post_ln_g": 1.0 + nrm(ks[6], (DEPTH, 2, d), 0.02),
        "post_ln_b": nrm(ks[7], (DEPTH, 2, d), 0.02),
        "even_w_in": nrm(ks[8], (N_EVEN, d, even_in), d ** -0.5),
        "even_w_out": nrm(ks[9], (N_EVEN, even_mix, d), DN_BETA * even_mix ** -0.5),
        "na_rpb": nrm(ks[10], (N_EVEN, NA_HEADS, 2 * NA_KR_MAX - 1, 2 * NA_KC - 1), 0.1),
        "odd_w_in": nrm(ks[11], (N_ODD, d, odd_in), d ** -0.5),
        "odd_w_out": nrm(ks[12], (N_ODD, odd_mix, d), DN_BETA * odd_mix ** -0.5),
        "gqa_sink": nrm(ks[13], (N_ODD, GQA_Q_HEADS), 1.0),
        "peer_w_q": nrm(ks[14], (DEPTH, d, PEER_HEADS * PEER_QDIM), d ** -0.5),
        "peer_sub_keys": nrm(ks[15], (DEPTH, PEER_HEADS, 2, PEER_NKEYS, PEER_QDIM // 2), (PEER_QDIM // 2) ** -0.5),
        "peer_u": nrm(ks[16], (DEPTH, PEER_EXPERTS, d), d ** -0.5),
        "peer_v": nrm(ks[17], (DEPTH, PEER_EXPERTS, d), DN_BETA),
    }


def reference(x, c, ctx, c_ctx, ada_w, ada_b, post_ln_g, post_ln_b, even_w_in, even_w_out, na_rpb,
              odd_w_in, odd_w_out, gqa_sink, peer_w_q, peer_sub_keys, peer_u, peer_v):
    cos, sin = _axial_rope_tables(x.shape[1])
    silu_c = jax.nn.silu(c)
    silu_cc = jax.nn.silu(c_ctx)
    h_ctx = ctx
    for layer in range(DEPTH):
        ctx_out = layer < DEPTH - 1
        i = layer // 2
        m_l = jnp.split((silu_c @ ada_w[layer] + ada_b[layer])[:, None, :], 6, -1)
        m_c = jnp.split((silu_cc @ ada_w[layer] + ada_b[layer])[None, None, :], 6, -1)
        a_l = _modulate(x, m_l[0], m_l[1])
        a_c = _modulate(h_ctx, m_c[0], m_c[1])
        if layer % 2 == 0:
            y_l, y_c = _even_mixer(a_l, a_c, even_w_in[i], even_w_out[i], na_rpb[i], ctx_out)
        else:
            y_l, y_c = _odd_mixer(a_l, a_c, odd_w_in[i], odd_w_out[i], gqa_sink[i], cos, sin, ctx_out)
        x = _layer_norm(DN_ALPHA * x + m_l[2] * y_l, post_ln_g[layer, 0], post_ln_b[layer, 0])
        f_l = _peer(_modulate(x, m_l[3], m_l[4]), peer_w_q[layer], peer_sub_keys[layer], peer_u[layer], peer_v[layer])
        x = _layer_norm(DN_ALPHA * x + m_l[5] * f_l, post_ln_g[layer, 1], post_ln_b[layer, 1])
        if ctx_out:
            h_ctx = _layer_norm(DN_ALPHA * h_ctx + m_c[2] * y_c, post_ln_g[layer, 0], post_ln_b[layer, 0])
            f_c = _peer(_modulate(h_ctx, m_c[3], m_c[4]), peer_w_q[layer], peer_sub_keys[layer], peer_u[layer], peer_v[layer])
            h_ctx = _layer_norm(DN_ALPHA * h_ctx + m_c[5] * f_c, post_ln_g[layer, 1], post_ln_b[layer, 1])
    return x
```

```python
import functools
import math

import numpy as np
import jax
import jax.numpy as jnp
from jax import lax
from jax.experimental import pallas as pl
from jax.experimental.pallas import tpu as pltpu

F32 = jnp.float32
BF16 = jnp.bfloat16

HEAD_DIM = 64
GRID_W = 64
FNET_GROUPS = 8
NA_HEADS = 8
NA_KR = 8
NA_KC = 16
NA_ROWS_PER_BLOCK = 8
NA_WIN_ROWS = 16
GQA_Q_HEADS = 16
GQA_KV_HEADS = 4
GQA_WINDOW = 128
GQA_BLOCK = 128
ROPE_THETA = 10000.0
PEER_HEADS = 8
PEER_NKEYS = 128
PEER_TOPK = 16
LN_EPS = 1e-6
NEG = -1e30

LANES = 128
VMEM_LIMIT = 56 * 1024 * 1024


def _cparams(*sem):
    return pltpu.CompilerParams(dimension_semantics=sem, vmem_limit_bytes=VMEM_LIMIT)


def _ln(x):
    mu = jnp.mean(x, axis=-1, keepdims=True)
    xc = x - mu
    var = jnp.mean(xc * xc, axis=-1, keepdims=True)
    return xc * lax.rsqrt(var + LN_EPS)


def _dot(a, b):
    return jnp.dot(a, b, preferred_element_type=F32)


def _dot_nt(a, b):
    return lax.dot_general(a, b, (((1,), (1,)), ((), ())), preferred_element_type=F32)


def _ada_body(c_ref, w_ref, b_ref, o_ref):
    c = c_ref[...]
    o_ref[0] = _dot(c * jax.nn.sigmoid(c), w_ref[0]) + b_ref[0]


def _ada(cond, ada_w, ada_b):
    depth, d, n = ada_w.shape
    tn = 1536
    return pl.pallas_call(
        _ada_body,
        grid=(depth, n // tn),
        in_specs=[
            pl.BlockSpec((8, d), lambda l, j: (0, 0)),
            pl.BlockSpec((1, d, tn), lambda l, j: (l, 0, j)),
            pl.BlockSpec((1, 1, tn), lambda l, j: (l, 0, j)),
        ],
        out_specs=pl.BlockSpec((1, 8, tn), lambda l, j: (l, 0, j)),
        out_shape=jax.ShapeDtypeStruct((depth, 8, n), F32),
        compiler_params=_cparams("arbitrary", "arbitrary"),
        name="ada_modulation",
    )(cond, ada_w, ada_b.reshape(depth, 1, n))


def _matmul_f32_body(a_ref, b_ref, o_ref):
    o_ref[...] = _dot(a_ref[...], b_ref[...])


def _matmul_f32(a, b):
    m, k = a.shape
    n = b.shape[1]
    tm = 256
    return pl.pallas_call(
        _matmul_f32_body,
        grid=(m // tm,),
        in_specs=[pl.BlockSpec((tm, k), lambda i: (i, 0)), pl.BlockSpec((k, n), lambda i: (0, 0))],
        out_specs=pl.BlockSpec((tm, n), lambda i: (i, 0)),
        out_shape=jax.ShapeDtypeStruct((m, n), F32),
        compiler_params=_cparams("arbitrary"),
        name="fold_channel_dft",
    )(a, b)


def _proj_body(*refs, plan, use_rope):
    x_ref, sh_ref, sc_ref, w_ref = refs[:4]
    rest = refs[4:]
    if use_rope:
        cos_ref, sin_ref = rest[:2]
        rest = rest[2:]
    h = _ln(x_ref[...]) * (1.0 + sc_ref[0]) + sh_ref[0]
    acc = _dot(h.astype(BF16), w_ref[...])
    for o_ref, (kind, start, width, scale, rot_start) in zip(rest, plan):
        y = acc[:, start:start + width]
        if rot_start is not None:
            reps = width // LANES
            cos = jnp.tile(cos_ref[...], (1, reps))
            sin = jnp.tile(sin_ref[...], (1, reps))
            y = y * cos + acc[:, rot_start:rot_start + width] * sin
        if scale != 1.0:
            y = y * scale
        if kind == "nat":
            o_ref[...] = y.astype(o_ref.dtype)
        else:
            for hh in range(width // HEAD_DIM):
                o_ref[hh] = y[:, hh * HEAD_DIM:(hh + 1) * HEAD_DIM].astype(o_ref.dtype)


def _proj(x, shift, scale, w, plan, tokens_per_batch, rope=None, tm=512):
    n, d = x.shape
    tm = min(tm, tokens_per_batch)
    per = tokens_per_batch // tm
    in_specs = [
        pl.BlockSpec((tm, d), lambda i: (i, 0)),
        pl.BlockSpec((1, 1, d), lambda i: (i // per, 0, 0)),
        pl.BlockSpec((1, 1, d), lambda i: (i // per, 0, 0)),
        pl.BlockSpec(w.shape, lambda i: (0, 0)),
    ]
    args = [x, shift, scale, w]
    if rope is not None:
        in_specs += [pl.BlockSpec((tm, LANES), lambda i: (i % per, 0))] * 2
        args += list(rope)
    out_specs, out_shape = [], []
    for kind, start, width, sc, rot in plan:
        if kind == "nat":
            out_specs.append(pl.BlockSpec((tm, width), lambda i: (i, 0)))
            out_shape.append(jax.ShapeDtypeStruct((n, width), F32))
        else:
            nh = width // HEAD_DIM
            out_specs.append(pl.BlockSpec((nh, tm, HEAD_DIM), lambda i: (0, i, 0)))
            out_shape.append(jax.ShapeDtypeStruct((nh, n, HEAD_DIM), BF16))
    return pl.pallas_call(
        functools.partial(_proj_body, plan=plan, use_rope=rope is not None),
        grid=(n // tm,),
        in_specs=in_specs,
        out_specs=out_specs,
        out_shape=out_shape,
        compiler_params=_cparams("arbitrary"),
        name="modln_proj",
    )(*args)


def _outproj_body(*refs, n_in, alpha):
    ys, ws = refs[:n_in], refs[n_in:2 * n_in]
    x_ref, gate_ref, g_ref, b_ref, o_ref = refs[2 * n_in:]
    acc = None
    for y_ref, w_ref in zip(ys, ws):
        t = _dot(y_ref[...].astype(BF16), w_ref[...])
        acc = t if acc is None else acc + t
    z = alpha * x_ref[...] + gate_ref[0] * acc
    o_ref[...] = _ln(z) * g_ref[...] + b_ref[...]


def _outproj_ln(ys, ws, x, gate, g, b, tokens_per_batch, alpha, tm=512):
    n, d = x.shape
    tm = min(tm, tokens_per_batch)
    per = tokens_per_batch // tm
    in_specs = [pl.BlockSpec((tm, y.shape[1]), lambda i: (i, 0)) for y in ys]
    in_specs += [pl.BlockSpec(w.shape, lambda i: (0, 0)) for w in ws]
    in_specs += [
        pl.BlockSpec((tm, d), lambda i: (i, 0)),
        pl.BlockSpec((1, 1, d), lambda i: (i // per, 0, 0)),
        pl.BlockSpec((1, d), lambda i: (0, 0)),
        pl.BlockSpec((1, d), lambda i: (0, 0)),
    ]
    return pl.pallas_call(
        functools.partial(_outproj_body, n_in=len(ys), alpha=alpha),
        grid=(n // tm,),
        in_specs=in_specs,
        out_specs=pl.BlockSpec((tm, d), lambda i: (i, 0)),
        out_shape=jax.ShapeDtypeStruct((n, d), F32),
        compiler_params=_cparams("arbitrary"),
        name="outproj_residual_ln",
    )(*ys, *ws, x, gate, g.reshape(1, d), b.reshape(1, d))


def _dft_tables(n):
    idx = np.arange(n)
    ang = 2.0 * np.pi * ((idx[:, None] * idx[None, :]) % n) / n
    return np.cos(ang), np.sin(ang)


def _fourier_rows_body(x_ref, cs_ref, tc_ref, ts_ref, o_ref, *, cb, width):
    nr = cs_ref.shape[1]
    pq = _dot(cs_ref[...], x_ref[0])
    for j in range(cb):
        a0 = j * 2 * width
        pa, pb = pq[:nr, a0:a0 + width], pq[:nr, a0 + width:a0 + 2 * width]
        qa, qb = pq[nr:, a0:a0 + width], pq[nr:, a0 + width:a0 + 2 * width]
        yr = pa + qb
        yi = pb - qa
        tc, ts = tc_ref[j], ts_ref[j]
        o_ref[0, j, :, :width] = yr * tc + yi * ts
        o_ref[0, j, :, width:] = yi * tc - yr * ts


def _fourier_cols_body(y_ref, cs_ref, o_ref, *, kb, width, norm):
    nc = cs_ref.shape[1]
    pq = _dot(cs_ref[...], y_ref[0])
    for j in range(kb):
        a0 = j * 2 * width
        z = pq[:nc, a0:a0 + width] + pq[nc:, a0 + width:a0 + 2 * width]
        o_ref[0, :, j * width:(j + 1) * width] = z * norm


def _fourier_latent(f, batch, rows, width):
    cols = GRID_W
    seq = rows * cols
    c_r, s_r = _dft_tables(rows)
    c_c, s_c = _dft_tables(cols)
    k1 = np.arange(rows)[None, :]
    cc = np.arange(cols)[:, None]
    tw = 2.0 * np.pi * ((cc * k1) % seq) / seq
    cs_r = jnp.asarray(np.concatenate([c_r, s_r], 0), F32)
    cs_c = jnp.asarray(np.concatenate([c_c, s_c], 0), F32)
    tc = jnp.asarray(np.cos(tw)[:, :, None], F32)
    ts = jnp.asarray(np.sin(tw)[:, :, None], F32)
    cb = 4
    kb = 8
    lane_w = 2 * width
    y = pl.pallas_call(
        functools.partial(_fourier_rows_body, cb=cb, width=width),
        grid=(batch, cols // cb),
        in_specs=[
            pl.BlockSpec((1, rows, cb * lane_w), lambda b, j: (b, 0, j)),
            pl.BlockSpec((2 * rows, rows), lambda b, j: (0, 0)),
            pl.BlockSpec((cb, rows, 1), lambda b, j: (j, 0, 0)),
            pl.BlockSpec((cb, rows, 1), lambda b, j: (j, 0, 0)),
        ],
        out_specs=pl.BlockSpec((1, cb, rows, lane_w), lambda b, j: (b, j, 0, 0)),
        out_shape=jax.ShapeDtypeStruct((batch, cols, rows, lane_w), F32),
        compiler_params=_cparams("arbitrary", "arbitrary"),
        name="fourier_rows",
    )(f.reshape(batch, rows, cols * lane_w), cs_r, tc, ts)
    z = pl.pallas_call(
        functools.partial(_fourier_cols_body, kb=kb, width=width, norm=float((seq * HEAD_DIM) ** -0.5)),
        grid=(batch, rows // kb),
        in_specs=[
            pl.BlockSpec((1, cols, kb * lane_w), lambda b, j: (b, 0, j)),
            pl.BlockSpec((2 * cols, cols), lambda b, j: (0, 0)),
        ],
        out_specs=pl.BlockSpec((1, cols, kb * width), lambda b, j: (b, 0, j)),
        out_shape=jax.ShapeDtypeStruct((batch, cols, rows * width), F32),
        compiler_params=_cparams("arbitrary", "arbitrary"),
        name="fourier_cols",
    )(y.reshape(batch, cols, rows * lane_w), cs_c)
    return z.reshape(batch * seq, width)


def _fourier_dense_body(x_ref, c_ref, s_ref, o_ref, *, width, norm):
    x = x_ref[...]
    o_ref[...] = (_dot(c_ref[...], x[:, :width]) + _dot(s_ref[...], x[:, width:])) * norm


def _fourier_dense(f, batch, length, width):
    c, s = _dft_tables(length)
    return pl.pallas_call(
        functools.partial(_fourier_dense_body, width=width, norm=float((length * HEAD_DIM) ** -0.5)),
        grid=(batch,),
        in_specs=[
            pl.BlockSpec((length, 2 * width), lambda b: (b, 0)),
            pl.BlockSpec((length, length), lambda b: (0, 0)),
            pl.BlockSpec((length, length), lambda b: (0, 0)),
        ],
        out_specs=pl.BlockSpec((length, width), lambda b: (b, 0)),
        out_shape=jax.ShapeDtypeStruct((batch * length, width), F32),
        compiler_params=_cparams("arbitrary"),
        name="fourier_dense",
    )(f, jnp.asarray(c, F32), jnp.asarray(s, F32))


def _natten_tables(rows, rpb):
    w = GRID_W
    qb, kw = NA_ROWS_PER_BLOCK, NA_WIN_ROWS
    qi, qc = np.divmod(np.arange(qb * w), w)
    km, kc = np.divmod(np.arange(kw * w), w)
    dr = km[None, :] - qi[:, None] - (kw - qb) // 2
    c_start = np.clip(qc - NA_KC // 2, 0, w - NA_KC)
    col_ok = (kc[None, :] >= c_start[:, None]) & (kc[None, :] < c_start[:, None] + NA_KC)
    dr_i = np.clip(dr + (NA_KR - 1), 0, 2 * NA_KR - 2)
    dc_i = np.clip(kc[None, :] - qc[:, None], -(NA_KC - 1), NA_KC - 1) + (NA_KC - 1)
    bias = rpb[:, dr_i, dc_i].astype(F32)
    col_bias = jnp.where(jnp.asarray(col_ok)[None], bias, NEG)
    nblk = rows // qb
    masks = []
    for j in range(nblk):
        qr = j * qb + qi
        kr = j * qb - (kw - qb) // 2 + km
        r_start = np.clip(qr - NA_KR // 2, 0, rows - NA_KR)
        ok = (kr[None, :] >= r_start[:, None]) & (kr[None, :] < r_start[:, None] + NA_KR)
        masks.append(np.where(ok, 0.0, NEG).astype(np.float32))
    kinds = [(j > 0) + (j == nblk - 1) for j in range(nblk)]
    table = np.zeros((3,) + masks[0].shape, np.float32)
    for j, kd in enumerate(kinds):
        table[kd] = masks[j]
    for j, kd in enumerate(kinds):
        assert np.array_equal(table[kd], masks[j])
    return col_bias, jnp.asarray(table)


def _natten_body(q_ref, k0_ref, k1_ref, k2_ref, k3_ref, v0_ref, v1_ref, v2_ref, v3_ref,
                 kc_ref, vc_ref, cb_ref, rm_ref, o_ref):
    k_refs = (k0_ref, k1_ref, k2_ref, k3_ref)
    v_refs = (v0_ref, v1_ref, v2_ref, v3_ref)
    outs = []
    for hh in range(q_ref.shape[0]):
        q = q_ref[hh]
        s = jnp.concatenate([_dot_nt(q, kr[hh]) for kr in k_refs], axis=1)
        s = s + cb_ref[hh] + rm_ref[0]
        sc = _dot_nt(q, kc_ref[hh])
        m = jnp.maximum(jnp.max(s, axis=-1, keepdims=True), jnp.max(sc, axis=-1, keepdims=True))
        p = jnp.exp(s - m)
        pc = jnp.exp(sc - m)
        l = jnp.sum(p, axis=-1, keepdims=True) + jnp.sum(pc, axis=-1, keepdims=True)
        kb = k0_ref.shape[1]
        o = _dot(pc.astype(BF16), vc_ref[hh])
        for c, vr in enumerate(v_refs):
            o = o + _dot(p[:, c * kb:(c + 1) * kb].astype(BF16), vr[hh])
        outs.append(o / l)
    o_ref[...] = jnp.concatenate(outs, axis=1).astype(o_ref.dtype)


def _natten(q, k, v, kc, vc, col_bias, row_mask, batch, rows, ctx_len):
    nh, n, dh = q.shape
    hp = 2
    qt = NA_ROWS_PER_BLOCK * GRID_W
    kt = NA_WIN_ROWS * GRID_W // 4
    nblk = rows // NA_ROWS_PER_BLOCK
    kblocks = rows * GRID_W // kt

    def kv_spec(c):
        return pl.BlockSpec(
            (hp, kt, dh),
            lambda b, j, h: (h, b * kblocks + jnp.clip(2 * j - 1 + c, 0, kblocks - 1), 0))

    in_specs = [pl.BlockSpec((hp, qt, dh), lambda b, j, h: (h, b * nblk + j, 0))]
    in_specs += [kv_spec(c) for c in range(4)] * 2
    in_specs += [
        pl.BlockSpec((hp, ctx_len, dh), lambda b, j, h: (h, b, 0)),
        pl.BlockSpec((hp, ctx_len, dh), lambda b, j, h: (h, b, 0)),
        pl.BlockSpec((hp, qt, 4 * kt), lambda b, j, h: (h, 0, 0)),
        pl.BlockSpec((1, qt, 4 * kt), lambda b, j, h: ((j > 0).astype(jnp.int32) + (j == nblk - 1).astype(jnp.int32), 0, 0)),
    ]
    return pl.pallas_call(
        _natten_body,
        grid=(batch, nblk, nh // hp),
        in_specs=in_specs,
        out_specs=pl.BlockSpec((qt, hp * dh), lambda b, j, h: (b * nblk + j, h)),
        out_shape=jax.ShapeDtypeStruct((n, nh * dh), BF16),
        compiler_params=_cparams("arbitrary", "arbitrary", "arbitrary"),
        name="neighbourhood_attention",
    )(q, k, k, k, k, v, v, v, v, kc, vc, col_bias, row_mask)


def _ctx_attn_body(q_ref, k_ref, v_ref, o_ref):
    outs = []
    for hh in range(q_ref.shape[0]):
        s = _dot_nt(q_ref[hh], k_ref[hh])
        m = jnp.max(s, axis=-1, keepdims=True)
        p = jnp.exp(s - m)
        l = jnp.sum(p, axis=-1, keepdims=True)
        outs.append(_dot(p.astype(BF16), v_ref[hh]) / l)
    o_ref[...] = jnp.concatenate(outs, axis=1).astype(o_ref.dtype)


def _ctx_attn(q, k, v, batch, ctx_len):
    nh, n, dh = q.shape
    hp = 2
    spec = pl.BlockSpec((hp, ctx_len, dh), lambda b, h: (h, b, 0))
    return pl.pallas_call(
        _ctx_attn_body,
        grid=(batch, nh // hp),
        in_specs=[spec, spec, spec],
        out_specs=pl.BlockSpec((ctx_len, hp * dh), lambda b, h: (b, h)),
        out_shape=jax.ShapeDtypeStruct((n, nh * dh), BF16),
        compiler_params=_cparams("arbitrary", "arbitrary"),
        name="context_attention",
    )(q, k, v)


def _gqa_body(sink_ref, q_ref, kp_ref, kc_ref, kn_ref, vp_ref, vc_ref, vn_ref, kx_ref, vx_ref, o_ref, *, nb):
    g, blk, dh = q_ref.shape
    n = pl.program_id(1)
    kvh = pl.program_id(2)
    q = q_ref[...].reshape(g * blk, dh)
    s = jnp.concatenate([_dot_nt(q, r[0]) for r in (kp_ref, kc_ref, kn_ref)], axis=1)
    qi = lax.broadcasted_iota(jnp.int32, s.shape, 0) % blk
    kp = lax.broadcasted_iota(jnp.int32, s.shape, 1) - blk
    ok = (jnp.abs(qi - kp) <= GQA_WINDOW) & ((kp >= 0) | (n > 0)) & ((kp < blk) | (n < nb - 1))
    s = jnp.where(ok, s, NEG)
    sx = _dot_nt(q, kx_ref[0])
    grp = lax.broadcasted_iota(jnp.int32, (g * blk, 1), 0) // blk
    sk = jnp.zeros((g * blk, 1), F32)
    for gi in range(g):
        sk = jnp.where(grp == gi, sink_ref[kvh * g + gi], sk)
    m = jnp.maximum(jnp.maximum(jnp.max(s, axis=-1, keepdims=True), jnp.max(sx, axis=-1, keepdims=True)), sk)
    p = jnp.exp(s - m)
    px = jnp.exp(sx - m)
    l = jnp.sum(p, axis=-1, keepdims=True) + jnp.sum(px, axis=-1, keepdims=True) + jnp.exp(sk - m)
    o = _dot(px.astype(BF16), vx_ref[0])
    for c, vr in enumerate((vp_ref, vc_ref, vn_ref)):
        o = o + _dot(p[:, c * blk:(c + 1) * blk].astype(BF16), vr[0])
    o = o / l
    o_ref[...] = jnp.concatenate([o[gi * blk:(gi + 1) * blk] for gi in range(g)], axis=1).astype(o_ref.dtype)


def _gqa(q, k, v, kx, vx, sink, batch, seq, ctx_len):
    hq, n, dh = q.shape
    hkv = k.shape[0]
    g = hq // hkv
    blk = GQA_BLOCK
    nb = seq // blk

    def kv_spec(c):
        return pl.BlockSpec((1, blk, dh), lambda b, i, h, s: (h, b * nb + jnp.clip(i - 1 + c, 0, nb - 1), 0))

    x_spec = pl.BlockSpec((1, ctx_len, dh), lambda b, i, h, s: (h, b, 0))
    grid_spec = pltpu.PrefetchScalarGridSpec(
        num_scalar_prefetch=1,
        grid=(batch, nb, hkv),
        in_specs=[pl.BlockSpec((g, blk, dh), lambda b, i, h, s: (h, b * nb + i, 0))]
        + [kv_spec(c) for c in range(3)] * 2 + [x_spec, x_spec],
        out_specs=pl.BlockSpec((blk, g * dh), lambda b, i, h, s: (b * nb + i, h)),
    )
    return pl.pallas_call(
        functools.partial(_gqa_body, nb=nb),
        grid_spec=grid_spec,
        out_shape=jax.ShapeDtypeStruct((n, hq * dh), BF16),
        compiler_params=_cparams("arbitrary", "arbitrary", "arbitrary"),
        name="window_gqa",
    )(sink, q, k, k, k, v, v, v, kx, vx)


def _top_values(s, k):
    tops = []
    for _ in range(k):
        m = jnp.max(s, axis=0, keepdims=True)
        tops.append(m)
        s = jnp.where(s == m, -jnp.inf, s)
    return tops


def _peer_route_body(x_ref, sh_ref, sc_ref, wq_ref, keys_ref, xm_ref, s1_ref, s2_ref, e1_ref, e2_ref, tau_ref):
    nheads = s1_ref.shape[0]
    nk = keys_ref.shape[1]
    h = _ln(x_ref[...]) * (1.0 + sc_ref[0]) + sh_ref[0]
    ht = h.T.astype(BF16)
    xm_ref[...] = ht
    qt = _dot(wq_ref[...], ht).astype(BF16)
    qd = keys_ref.shape[2]
    row = lax.broadcasted_iota(jnp.int32, (PEER_TOPK, 1), 0)
    for hd in range(nheads):
        halves = []
        for half in range(2):
            hp = hd * 2 + half
            s = _dot(keys_ref[hp], qt[hp * qd:(hp + 1) * qd, :])
            tops = _top_values(s, PEER_TOPK)
            halves.append((s, tops))
        (s1, t1), (s2, t2) = halves
        t2a = jnp.concatenate(t2, axis=0)
        slabs = []
        for a in range(PEER_TOPK):
            nb_ok = PEER_TOPK // (a + 1)
            rows = PEER_TOPK if a == 0 else 8
            slab = t1[a] + t2a[:rows]
            slabs.append(jnp.where(row[:rows] < nb_ok, slab, -jnp.inf))
        cand = jnp.concatenate(slabs, axis=0)
        tau = _top_values(cand, PEER_TOPK)[-1]
        cmax = t1[0] + t2[0]
        z = jnp.sum(jnp.where(cand >= tau, jnp.exp(cand - cmax), 0.0), axis=0, keepdims=True)
        s1_ref[hd] = s1
        s2_ref[hd] = s2
        e1_ref[hd] = jnp.exp(s1 - t1[0])
        e2_ref[hd] = jnp.exp(s2 - t2[0]) / z
        tau_ref[pl.ds(hd, 1), :] = tau


def _peer_route(x, shift, scale, wq_t, keys, tokens_per_batch, tt=256):
    n, d = x.shape
    tt = min(tt, tokens_per_batch)
    per = tokens_per_batch // tt
    nh = keys.shape[0] // 2
    nk = keys.shape[1]
    big = pl.BlockSpec((nh, nk, tt), lambda i: (0, 0, i))
    big_shape = jax.ShapeDtypeStruct((nh, nk, n), F32)
    return pl.pallas_call(
        _peer_route_body,
        grid=(n // tt,),
        in_specs=[
            pl.BlockSpec((tt, d), lambda i: (i, 0)),
            pl.BlockSpec((1, 1, d), lambda i: (i // per, 0, 0)),
            pl.BlockSpec((1, 1, d), lambda i: (i // per, 0, 0)),
            pl.BlockSpec(wq_t.shape, lambda i: (0, 0)),
            pl.BlockSpec(keys.shape, lambda i: (0, 0, 0)),
        ],
        out_specs=[pl.BlockSpec((d, tt), lambda i: (0, i)), big, big, big, big,
                   pl.BlockSpec((nh, tt), lambda i: (0, i))],
        out_shape=[jax.ShapeDtypeStruct((d, n), BF16), big_shape, big_shape, big_shape, big_shape,
                   jax.ShapeDtypeStruct((nh, n), F32)],
        compiler_params=_cparams("arbitrary"),
        name="peer_route",
    )(x, shift, scale, wq_t, keys)


def _peer_expert_body(xm_ref, u_ref, vt_ref, s1_ref, s2_ref, e1_ref, e2_ref, tau_ref,
                      x_ref, gate_ref, g_ref, b_ref, o_ref, acc_ref, w_ref, *, alpha):
    e = pl.program_id(1)
    nheads, n1, _ = s1_ref.shape
    nk = s2_ref.shape[1]

    @pl.when(e == 0)
    def _():
        acc_ref[...] = jnp.zeros_like(acc_ref)

    a_t = _dot(u_ref[...], xm_ref[...])
    for r in range(n1):
        gsum = None
        for hd in range(nheads):
            s = s1_ref[hd, pl.ds(r, 1), :] + s2_ref[hd]
            val = e1_ref[hd, pl.ds(r, 1), :] * e2_ref[hd]
            term = jnp.where(s >= tau_ref[pl.ds(hd, 1), :], val, 0.0)
            gsum = term if gsum is None else gsum + term
        a = a_t[r * nk:(r + 1) * nk, :]
        act = 0.5 * a * (1.0 + lax.erf(a * (2.0 ** -0.5)))
        w_ref[pl.ds(r * nk, nk), :] = (gsum * act).astype(BF16)
    acc_ref[...] += _dot(vt_ref[...], w_ref[...])

    @pl.when(e == pl.num_programs(1) - 1)
    def _():
        f = acc_ref[...].T
        z = alpha * x_ref[...] + gate_ref[0] * f
        o_ref[...] = _ln(z) * g_ref[...] + b_ref[...]


def _peer_experts(xm_t, u, v_t, s1, s2, e1, e2, tau, x, gate, g, b, tokens_per_batch, alpha, tt=512, et=1024):
    n, d = x.shape
    tt = min(tt, tokens_per_batch)
    per = tokens_per_batch // tt
    nh, nk, _ = s1.shape
    n1 = et // nk
    ne = u.shape[0] // et
    sel = pl.BlockSpec((nh, n1, tt), lambda i, e: (0, e, i))
    full = pl.BlockSpec((nh, nk, tt), lambda i, e: (0, 0, i))
    return pl.pallas_call(
        functools.partial(_peer_expert_body, alpha=alpha),
        grid=(n // tt, ne),
        in_specs=[
            pl.BlockSpec((d, tt), lambda i, e: (0, i)),
            pl.BlockSpec((et, d), lambda i, e: (e, 0)),
            pl.BlockSpec((d, et), lambda i, e: (0, e)),
            sel, full, sel, full,
            pl.BlockSpec((nh, tt), lambda i, e: (0, i)),
            pl.BlockSpec((tt, d), lambda i, e: (i, 0)),
            pl.BlockSpec((1, 1, d), lambda i, e: (i // per, 0, 0)),
            pl.BlockSpec((1, d), lambda i, e: (0, 0)),
            pl.BlockSpec((1, d), lambda i, e: (0, 0)),
        ],
        out_specs=pl.BlockSpec((tt, d), lambda i, e: (i, 0)),
        out_shape=jax.ShapeDtypeStruct((n, d), F32),
        scratch_shapes=[pltpu.VMEM((d, tt), F32), pltpu.VMEM((et, tt), BF16)],
        compiler_params=_cparams("arbitrary", "arbitrary"),
        name="peer_experts",
    )(xm_t, u, v_t, s1, s2, e1, e2, tau, x, gate, g.reshape(1, d), b.reshape(1, d))


def _peer_layer(x, shift, scale, gate, g, b, tables, tokens_per_batch, alpha):
    wq_t, keys, u, v_t = tables
    xm_t, s1, s2, e1, e2, tau = _peer_route(x, shift, scale, wq_t, keys, tokens_per_batch)
    return _peer_experts(xm_t, u, v_t, s1, s2, e1, e2, tau, x, gate, g, b, tokens_per_batch, alpha)


def _rope_tables(seq):
    t = jnp.arange(seq)
    row = (t // GRID_W).astype(F32)
    col = (t % GRID_W).astype(F32)
    n_freq = HEAD_DIM // 4
    inv_freq = ROPE_THETA ** (-jnp.arange(n_freq, dtype=F32) / n_freq)
    ang = jnp.concatenate([row[:, None] * inv_freq, col[:, None] * inv_freq], -1)
    cos, sin = jnp.cos(ang), jnp.sin(ang)
    reps = LANES // (HEAD_DIM // 2)
    return jnp.tile(cos, (1, reps)), jnp.tile(sin, (1, reps))


def _rotate_half_columns(w, n_heads):
    d = w.shape[0]
    wh = w.reshape(d, n_heads, 2, HEAD_DIM // 2)
    return jnp.concatenate([-wh[:, :, 1], wh[:, :, 0]], axis=-1).reshape(d, n_heads * HEAD_DIM)


def _peer_tables(w_q, sub_keys, u, v):
    nh, _, nk, qd = sub_keys.shape
    return (w_q.T.astype(BF16), sub_keys.reshape(nh * 2, nk, qd).astype(BF16), u.astype(BF16), v.T.astype(BF16))


def kernel(x, c, ctx, c_ctx, ada_w, ada_b, post_ln_g, post_ln_b, even_w_in, even_w_out, na_rpb,
           odd_w_in, odd_w_out, gqa_sink, peer_w_q, peer_sub_keys, peer_u, peer_v):
    batch, seq, d = x.shape
    ctx_len = ctx.shape[1]
    depth = ada_w.shape[0]
    rows = seq // GRID_W
    alpha = float((2 * depth) ** 0.25)
    fw = FNET_GROUPS * HEAD_DIM
    nw = NA_HEADS * HEAD_DIM
    qw = GQA_Q_HEADS * HEAD_DIM
    kvw = GQA_KV_HEADS * HEAD_DIM
    qscale = HEAD_DIM ** -0.5

    cond = jnp.zeros((8, d), F32).at[:batch].set(c).at[batch].set(c_ctx)
    mods = _ada(cond, ada_w, ada_b)

    xl = x.reshape(batch * seq, d)
    hc = ctx.reshape(batch * ctx_len, d)
    cos_t, sin_t = _rope_tables(seq)

    cg, sg = _dft_tables(HEAD_DIM)
    eye = np.eye(FNET_GROUPS)
    chan = jnp.asarray(np.concatenate([np.kron(eye, cg), -np.kron(eye, sg)], axis=1), F32)

    for layer in range(depth):
        ctx_out = layer < depth - 1
        i = layer // 2
        m_l = [m.reshape(batch, 1, d) for m in jnp.split(mods[layer, :batch], 6, axis=-1)]
        m_c = [jnp.broadcast_to(m.reshape(1, 1, d), (batch, 1, d)) for m in jnp.split(mods[layer, batch], 6, axis=-1)]
        g0, b0 = post_ln_g[layer, 0], post_ln_b[layer, 0]
        g1, b1 = post_ln_g[layer, 1], post_ln_b[layer, 1]

        if layer % 2 == 0:
            w_in, w_out = even_w_in[i], even_w_out[i]
            w_f = _matmul_f32(w_in[:, :fw], chan)
            w_aug = jnp.concatenate([w_f, w_in[:, fw:]], axis=1).astype(BF16)
            plan = (("nat", 0, 2 * fw, 1.0, None),
                    ("heads", 2 * fw, nw, qscale, None),
                    ("heads", 2 * fw + nw, nw, 1.0, None),
                    ("heads", 2 * fw + 2 * nw, nw, 1.0, None))
            f_l, q_l, k_l, v_l = _proj(xl, m_l[0], m_l[1], w_aug, plan, seq)
            f_c, q_c, k_c, v_c = _proj(hc, m_c[0], m_c[1], w_aug, plan, ctx_len)
            col_bias, row_mask = _natten_tables(rows, na_rpb[i])
            na_l = _natten(q_l, k_l, v_l, k_c, v_c, col_bias, row_mask, batch, rows, ctx_len)
            fm_l = _fourier_latent(f_l, batch, rows, fw)
            w_out_b = w_out.astype(BF16)
            ws = [w_out_b[:fw], w_out_b[fw:]]
            xl_new = _outproj_ln([fm_l, na_l], ws, xl, m_l[2], g0, b0, seq, alpha)
            if ctx_out:
                na_c = _ctx_attn(q_c, k_c, v_c, batch, ctx_len)
                fm_c = _fourier_dense(f_c, batch, ctx_len, fw)
                hc_new = _outproj_ln([fm_c, na_c], ws, hc, m_c[2], g0, b0, ctx_len, alpha)
        else:
            w_in, w_out = odd_w_in[i], odd_w_out[i]
            wq, wk, wv = w_in[:, :qw], w_in[:, qw:qw + kvw], w_in[:, qw + kvw:]
            w_aug = jnp.concatenate([wq, wk, wv, _rotate_half_columns(wq, GQA_Q_HEADS),
                                     _rotate_half_columns(wk, GQA_KV_HEADS)], axis=1).astype(BF16)
            plan = (("heads", 0, qw, qscale, qw + 2 * kvw),
                    ("heads", qw, kvw, 1.0, 2 * qw + 2 * kvw),
                    ("heads", qw + kvw, kvw, 1.0, None))
            q_l, k_l, v_l = _proj(xl, m_l[0], m_l[1], w_aug, plan, seq, rope=(cos_t, sin_t))
            plan_c = (("heads", 0, kvw, 1.0, None), ("heads", kvw, kvw, 1.0, None))
            if ctx_out:
                raise NotImplementedError("an odd layer must be the last layer (no context output path)")
            k_c, v_c = _proj(hc, m_c[0], m_c[1], w_in[:, qw:].astype(BF16), plan_c, ctx_len)
            y_l = _gqa(q_l, k_l, v_l, k_c, v_c, gqa_sink[i], batch, seq, ctx_len)
            xl_new = _outproj_ln([y_l], [w_out.astype(BF16)], xl, m_l[2], g0, b0, seq, alpha)

        tables = _peer_tables(peer_w_q[layer], peer_sub_keys[layer], peer_u[layer], peer_v[layer])
        xl = _peer_layer(xl_new, m_l[3], m_l[4], m_l[5], g1, b1, tables, seq, alpha)
        if ctx_out:
            hc = _peer_layer(hc_new, m_c[3], m_c[4], m_c[5], g1, b1, tables, ctx_len, alpha)

    return xl.reshape(batch, seq, d)
```

```python
import functools
import math

import numpy as np
import jax
import jax.numpy as jnp
from jax import lax
from jax.experimental import pallas as pl
from jax.experimental.pallas import tpu as pltpu

F32 = jnp.float32
BF16 = jnp.bfloat16

HEAD_DIM = 64
GRID_W = 64
FNET_GROUPS = 8
NA_HEADS = 8
NA_KR = 8
NA_KC = 16
NA_ROWS_PER_BLOCK = 8
NA_WIN_ROWS = 16
GQA_Q_HEADS = 16
GQA_KV_HEADS = 4
GQA_WINDOW = 128
GQA_BLOCK = 128
ROPE_THETA = 10000.0
PEER_HEADS = 8
PEER_NKEYS = 128
PEER_TOPK = 16
LN_EPS = 1e-6
NEG = -1e30

LANES = 128
VMEM_LIMIT = 56 * 1024 * 1024


def _cparams(*sem):
    return pltpu.CompilerParams(dimension_semantics=sem, vmem_limit_bytes=VMEM_LIMIT)


def _ln(x):
    mu = jnp.mean(x, axis=-1, keepdims=True)
    xc = x - mu
    var = jnp.mean(xc * xc, axis=-1, keepdims=True)
    return xc * lax.rsqrt(var + LN_EPS)


def _dot(a, b):
    return jnp.dot(a, b, preferred_element_type=F32)


def _dot_nt(a, b):
    return lax.dot_general(a, b, (((1,), (1,)), ((), ())), preferred_element_type=F32)


def _ada_body(c_ref, w_ref, b_ref, o_ref):
    c = c_ref[...]
    o_ref[0] = _dot(c * jax.nn.sigmoid(c), w_ref[0]) + b_ref[0]


def _ada(cond, ada_w, ada_b):
    depth, d, n = ada_w.shape
    tn = 1536
    return pl.pallas_call(
        _ada_body,
        grid=(depth, n // tn),
        in_specs=[
            pl.BlockSpec((8, d), lambda l, j: (0, 0)),
            pl.BlockSpec((1, d, tn), lambda l, j: (l, 0, j)),
            pl.BlockSpec((1, 1, tn), lambda l, j: (l, 0, j)),
        ],
        out_specs=pl.BlockSpec((1, 8, tn), lambda l, j: (l, 0, j)),
        out_shape=jax.ShapeDtypeStruct((depth, 8, n), F32),
        compiler_params=_cparams("arbitrary", "arbitrary"),
        name="ada_modulation",
    )(cond, ada_w, ada_b.reshape(depth, 1, n))


def _matmul_f32_body(a_ref, b_ref, o_ref):
    o_ref[...] = jnp.dot(a_ref[...], b_ref[...], preferred_element_type=F32, precision=lax.Precision.HIGHEST)


def _matmul_f32(a, b):
    m, k = a.shape
    n = b.shape[1]
    tm = min(256, m)
    return pl.pallas_call(
        _matmul_f32_body,
        grid=(m // tm,),
        in_specs=[pl.BlockSpec((tm, k), lambda i: (i, 0)), pl.BlockSpec((k, n), lambda i: (0, 0))],
        out_specs=pl.BlockSpec((tm, n), lambda i: (i, 0)),
        out_shape=jax.ShapeDtypeStruct((m, n), F32),
        compiler_params=_cparams("arbitrary"),
        name="small_matmul_f32",
    )(a, b)


def _proj_body(*refs, plan, use_rope):
    x_ref, sh_ref, sc_ref, w_ref = refs[:4]
    rest = refs[4:]
    if use_rope:
        cos_ref, sin_ref = rest[:2]
        rest = rest[2:]
    h = _ln(x_ref[...]) * (1.0 + sc_ref[0]) + sh_ref[0]
    acc = _dot(h.astype(BF16), w_ref[...])
    for o_ref, (kind, start, width, scale, rot_start) in zip(rest, plan):
        y = acc[:, start:start + width]
        if rot_start is not None:
            reps = width // LANES
            cos = jnp.tile(cos_ref[...], (1, reps))
            sin = jnp.tile(sin_ref[...], (1, reps))
            y = y * cos + acc[:, rot_start:rot_start + width] * sin
        if scale != 1.0:
            y = y * scale
        if kind == "nat":
            o_ref[...] = y.astype(o_ref.dtype)
        else:
            for hh in range(width // HEAD_DIM):
                o_ref[hh] = y[:, hh * HEAD_DIM:(hh + 1) * HEAD_DIM].astype(o_ref.dtype)


def _proj(x, shift, scale, w, plan, tokens_per_batch, rope=None, tm=512):
    n, d = x.shape
    tm = min(tm, tokens_per_batch)
    per = tokens_per_batch // tm
    in_specs = [
        pl.BlockSpec((tm, d), lambda i: (i, 0)),
        pl.BlockSpec((1, 1, d), lambda i: (i // per, 0, 0)),
        pl.BlockSpec((1, 1, d), lambda i: (i // per, 0, 0)),
        pl.BlockSpec(w.shape, lambda i: (0, 0)),
    ]
    args = [x, shift, scale, w]
    if rope is not None:
        in_specs += [pl.BlockSpec((tm, LANES), lambda i: (i % per, 0))] * 2
        args += list(rope)
    out_specs, out_shape = [], []
    for kind, start, width, sc, rot in plan:
        if kind == "nat":
            out_specs.append(pl.BlockSpec((tm, width), lambda i: (i, 0)))
            out_shape.append(jax.ShapeDtypeStruct((n, width), F32))
        else:
            nh = width // HEAD_DIM
            out_specs.append(pl.BlockSpec((nh, tm, HEAD_DIM), lambda i: (0, i, 0)))
            out_shape.append(jax.ShapeDtypeStruct((nh, n, HEAD_DIM), BF16))
    return pl.pallas_call(
        functools.partial(_proj_body, plan=plan, use_rope=rope is not None),
        grid=(n // tm,),
        in_specs=in_specs,
        out_specs=out_specs,
        out_shape=out_shape,
        compiler_params=_cparams("arbitrary"),
        name="modln_proj",
    )(*args)


def _outproj_body(*refs, n_in, alpha):
    ys, ws = refs[:n_in], refs[n_in:2 * n_in]
    x_ref, gate_ref, g_ref, b_ref, o_ref = refs[2 * n_in:]
    acc = None
    for y_ref, w_ref in zip(ys, ws):
        t = _dot(y_ref[...].astype(BF16), w_ref[...])
        acc = t if acc is None else acc + t
    z = alpha * x_ref[...] + gate_ref[0] * acc
    o_ref[...] = _ln(z) * g_ref[...] + b_ref[...]


def _outproj_ln(ys, ws, x, gate, g, b, tokens_per_batch, alpha, tm=512):
    n, d = x.shape
    tm = min(tm, tokens_per_batch)
    per = tokens_per_batch // tm
    in_specs = [pl.BlockSpec((tm, y.shape[1]), lambda i: (i, 0)) for y in ys]
    in_specs += [pl.BlockSpec(w.shape, lambda i: (0, 0)) for w in ws]
    in_specs += [
        pl.BlockSpec((tm, d), lambda i: (i, 0)),
        pl.BlockSpec((1, 1, d), lambda i: (i // per, 0, 0)),
        pl.BlockSpec((1, d), lambda i: (0, 0)),
        pl.BlockSpec((1, d), lambda i: (0, 0)),
    ]
    return pl.pallas_call(
        functools.partial(_outproj_body, n_in=len(ys), alpha=alpha),
        grid=(n // tm,),
        in_specs=in_specs,
        out_specs=pl.BlockSpec((tm, d), lambda i: (i, 0)),
        out_shape=jax.ShapeDtypeStruct((n, d), F32),
        compiler_params=_cparams("arbitrary"),
        name="outproj_residual_ln",
    )(*ys, *ws, x, gate, g.reshape(1, d), b.reshape(1, d))


def _dft_tables(n):
    idx = np.arange(n)
    ang = 2.0 * np.pi * ((idx[:, None] * idx[None, :]) % n) / n
    return np.cos(ang), np.sin(ang)


def _fourier_rows_body(x_ref, cs_ref, tc_ref, ts_ref, o_ref, *, cb, width):
    nr = cs_ref.shape[1]
    pq = _dot(cs_ref[...], x_ref[0])
    for j in range(cb):
        a0 = j * 2 * width
        pa, pb = pq[:nr, a0:a0 + width], pq[:nr, a0 + width:a0 + 2 * width]
        qa, qb = pq[nr:, a0:a0 + width], pq[nr:, a0 + width:a0 + 2 * width]
        yr = pa + qb
        yi = pb - qa
        tc, ts = tc_ref[j], ts_ref[j]
        o_ref[0, j, :, :width] = yr * tc + yi * ts
        o_ref[0, j, :, width:] = yi * tc - yr * ts


def _fourier_cols_body(y_ref, cs_ref, o_ref, *, kb, width, norm):
    nc = cs_ref.shape[1]
    pq = _dot(cs_ref[...], y_ref[0])
    for j in range(kb):
        a0 = j * 2 * width
        z = pq[:nc, a0:a0 + width] + pq[nc:, a0 + width:a0 + 2 * width]
        o_ref[0, :, j * width:(j + 1) * width] = z * norm


def _fourier_latent(f, batch, rows, width):
    cols = GRID_W
    seq = rows * cols
    c_r, s_r = _dft_tables(rows)
    c_c, s_c = _dft_tables(cols)
    k1 = np.arange(rows)[None, :]
    cc = np.arange(cols)[:, None]
    tw = 2.0 * np.pi * ((cc * k1) % seq) / seq
    cs_r = jnp.asarray(np.concatenate([c_r, s_r], 0), F32)
    cs_c = jnp.asarray(np.concatenate([c_c, s_c], 0), F32)
    tc = jnp.asarray(np.cos(tw)[:, :, None], F32)
    ts = jnp.asarray(np.sin(tw)[:, :, None], F32)
    cb = 4
    kb = 8
    lane_w = 2 * width
    y = pl.pallas_call(
        functools.partial(_fourier_rows_body, cb=cb, width=width),
        grid=(batch, cols // cb),
        in_specs=[
            pl.BlockSpec((1, rows, cb * lane_w), lambda b, j: (b, 0, j)),
            pl.BlockSpec((2 * rows, rows), lambda b, j: (0, 0)),
            pl.BlockSpec((cb, rows, 1), lambda b, j: (j, 0, 0)),
            pl.BlockSpec((cb, rows, 1), lambda b, j: (j, 0, 0)),
        ],
        out_specs=pl.BlockSpec((1, cb, rows, lane_w), lambda b, j: (b, j, 0, 0)),
        out_shape=jax.ShapeDtypeStruct((batch, cols, rows, lane_w), F32),
        compiler_params=_cparams("arbitrary", "arbitrary"),
        name="fourier_rows",
    )(f.reshape(batch, rows, cols * lane_w), cs_r, tc, ts)
    z = pl.pallas_call(
        functools.partial(_fourier_cols_body, kb=kb, width=width, norm=float((seq * HEAD_DIM) ** -0.5)),
        grid=(batch, rows // kb),
        in_specs=[
            pl.BlockSpec((1, cols, kb * lane_w), lambda b, j: (b, 0, j)),
            pl.BlockSpec((2 * cols, cols), lambda b, j: (0, 0)),
        ],
        out_specs=pl.BlockSpec((1, cols, kb * width), lambda b, j: (b, 0, j)),
        out_shape=jax.ShapeDtypeStruct((batch, cols, rows * width), F32),
        compiler_params=_cparams("arbitrary", "arbitrary"),
        name="fourier_cols",
    )(y.reshape(batch, cols, rows * lane_w), cs_c)
    return z.reshape(batch * seq, width)


def _fourier_dense_body(x_ref, c_ref, s_ref, o_ref, *, width, norm):
    x = x_ref[...]
    o_ref[...] = (_dot(c_ref[...], x[:, :width]) + _dot(s_ref[...], x[:, width:])) * norm


def _fourier_dense(f, batch, length, width):
    c, s = _dft_tables(length)
    return pl.pallas_call(
        functools.partial(_fourier_dense_body, width=width, norm=float((length * HEAD_DIM) ** -0.5)),
        grid=(batch,),
        in_specs=[
            pl.BlockSpec((length, 2 * width), lambda b: (b, 0)),
            pl.BlockSpec((length, length), lambda b: (0, 0)),
            pl.BlockSpec((length, length), lambda b: (0, 0)),
        ],
        out_specs=pl.BlockSpec((length, width), lambda b: (b, 0)),
        out_shape=jax.ShapeDtypeStruct((batch * length, width), F32),
        compiler_params=_cparams("arbitrary"),
        name="fourier_dense",
    )(f, jnp.asarray(c, F32), jnp.asarray(s, F32))


def _natten_tables(rows, rpb):
    w = GRID_W
    qb, kw = NA_ROWS_PER_BLOCK, NA_WIN_ROWS
    nh, ndr, ndc = rpb.shape
    tq, tk = np.divmod(np.arange(w * w), w)
    dc_i = np.clip(tk - tq, -(NA_KC - 1), NA_KC - 1) + (NA_KC - 1)
    onehot = np.zeros((LANES, w * w), np.float32)
    onehot[dc_i, np.arange(w * w)] = 1.0
    c_start = np.clip(tq - NA_KC // 2, 0, w - NA_KC)
    col_ok = ((tk >= c_start) & (tk < c_start + NA_KC)).reshape(w, w)
    rp = jnp.zeros((LANES, LANES), F32).at[:nh * ndr, :ndc].set(rpb.reshape(nh * ndr, ndc).astype(F32))
    tiles = _matmul_f32(rp, jnp.asarray(onehot))[:nh * ndr].reshape(nh, ndr, w, w)
    tiles = jnp.where(jnp.asarray(col_ok), tiles, NEG)
    dr = np.arange(kw)[None, :] - np.arange(qb)[:, None] - (kw - qb) // 2
    dr_i = np.clip(dr + (NA_KR - 1), 0, ndr - 1)
    col_bias = jnp.transpose(tiles[:, dr_i], (0, 1, 3, 2, 4)).reshape(nh, qb * w, kw * w)
    qi = np.repeat(np.arange(qb), w)
    km = np.repeat(np.arange(kw), w)
    nblk = rows // qb
    masks = []
    for j in range(nblk):
        qr = j * qb + qi
        kr = j * qb - (kw - qb) // 2 + km
        r_start = np.clip(qr - NA_KR // 2, 0, rows - NA_KR)
        ok = (kr[None, :] >= r_start[:, None]) & (kr[None, :] < r_start[:, None] + NA_KR)
        masks.append(np.where(ok, 0.0, NEG).astype(np.float32))
    kinds = [(j > 0) + (j == nblk - 1) for j in range(nblk)]
    table = np.zeros((3,) + masks[0].shape, np.float32)
    for j, kd in enumerate(kinds):
        table[kd] = masks[j]
    for j, kd in enumerate(kinds):
        assert np.array_equal(table[kd], masks[j])
    return col_bias, jnp.asarray(table)


def _natten_body(q_ref, k0_ref, k1_ref, k2_ref, k3_ref, v0_ref, v1_ref, v2_ref, v3_ref,
                 kc_ref, vc_ref, cb_ref, rm_ref, o_ref):
    k_refs = (k0_ref, k1_ref, k2_ref, k3_ref)
    v_refs = (v0_ref, v1_ref, v2_ref, v3_ref)
    outs = []
    for hh in range(q_ref.shape[0]):
        q = q_ref[hh]
        s = jnp.concatenate([_dot_nt(q, kr[hh]) for kr in k_refs], axis=1)
        s = s + cb_ref[hh] + rm_ref[0]
        sc = _dot_nt(q, kc_ref[hh])
        m = jnp.maximum(jnp.max(s, axis=-1, keepdims=True), jnp.max(sc, axis=-1, keepdims=True))
        p = jnp.exp(s - m)
        pc = jnp.exp(sc - m)
        l = jnp.sum(p, axis=-1, keepdims=True) + jnp.sum(pc, axis=-1, keepdims=True)
        kb = k0_ref.shape[1]
        o = _dot(pc.astype(BF16), vc_ref[hh])
        for c, vr in enumerate(v_refs):
            o = o + _dot(p[:, c * kb:(c + 1) * kb].astype(BF16), vr[hh])
        outs.append(o / l)
    o_ref[...] = jnp.concatenate(outs, axis=1).astype(o_ref.dtype)


def _natten(q, k, v, kc, vc, col_bias, row_mask, batch, rows, ctx_len):
    nh, n, dh = q.shape
    hp = 2
    qt = NA_ROWS_PER_BLOCK * GRID_W
    kt = NA_WIN_ROWS * GRID_W // 4
    nblk = rows // NA_ROWS_PER_BLOCK
    kblocks = rows * GRID_W // kt

    def kv_spec(c):
        return pl.BlockSpec(
            (hp, kt, dh),
            lambda b, j, h: (h, b * kblocks + jnp.clip(2 * j - 1 + c, 0, kblocks - 1), 0))

    in_specs = [pl.BlockSpec((hp, qt, dh), lambda b, j, h: (h, b * nblk + j, 0))]
    in_specs += [kv_spec(c) for c in range(4)] * 2
    in_specs += [
        pl.BlockSpec((hp, ctx_len, dh), lambda b, j, h: (h, b, 0)),
        pl.BlockSpec((hp, ctx_len, dh), lambda b, j, h: (h, b, 0)),
        pl.BlockSpec((hp, qt, 4 * kt), lambda b, j, h: (h, 0, 0)),
        pl.BlockSpec((1, qt, 4 * kt), lambda b, j, h: ((j > 0).astype(jnp.int32) + (j == nblk - 1).astype(jnp.int32), 0, 0)),
    ]
    return pl.pallas_call(
        _natten_body,
        grid=(batch, nblk, nh // hp),
        in_specs=in_specs,
        out_specs=pl.BlockSpec((qt, hp * dh), lambda b, j, h: (b * nblk + j, h)),
        out_shape=jax.ShapeDtypeStruct((n, nh * dh), BF16),
        compiler_params=_cparams("arbitrary", "arbitrary", "arbitrary"),
        name="neighbourhood_attention",
    )(q, k, k, k, k, v, v, v, v, kc, vc, col_bias, row_mask)


def _ctx_attn_body(q_ref, k_ref, v_ref, o_ref):
    outs = []
    for hh in range(q_ref.shape[0]):
        s = _dot_nt(q_ref[hh], k_ref[hh])
        m = jnp.max(s, axis=-1, keepdims=True)
        p = jnp.exp(s - m)
        l = jnp.sum(p, axis=-1, keepdims=True)
        outs.append(_dot(p.astype(BF16), v_ref[hh]) / l)
    o_ref[...] = jnp.concatenate(outs, axis=1).astype(o_ref.dtype)


def _ctx_attn(q, k, v, batch, ctx_len):
    nh, n, dh = q.shape
    hp = 2
    spec = pl.BlockSpec((hp, ctx_len, dh), lambda b, h: (h, b, 0))
    return pl.pallas_call(
        _ctx_attn_body,
        grid=(batch, nh // hp),
        in_specs=[spec, spec, spec],
        out_specs=pl.BlockSpec((ctx_len, hp * dh), lambda b, h: (b, h)),
        out_shape=jax.ShapeDtypeStruct((n, nh * dh), BF16),
        compiler_params=_cparams("arbitrary", "arbitrary"),
        name="context_attention",
    )(q, k, v)


def _gqa_body(sink_ref, q_ref, kp_ref, kc_ref, kn_ref, vp_ref, vc_ref, vn_ref, kx_ref, vx_ref, o_ref, *, nb):
    g, blk, dh = q_ref.shape
    n = pl.program_id(1)
    kvh = pl.program_id(2)
    q = q_ref[...].reshape(g * blk, dh)
    s = jnp.concatenate([_dot_nt(q, r[0]) for r in (kp_ref, kc_ref, kn_ref)], axis=1)
    qi = lax.broadcasted_iota(jnp.int32, s.shape, 0) % blk
    kp = lax.broadcasted_iota(jnp.int32, s.shape, 1) - blk
    ok = (jnp.abs(qi - kp) <= GQA_WINDOW) & ((kp >= 0) | (n > 0)) & ((kp < blk) | (n < nb - 1))
    s = jnp.where(ok, s, NEG)
    sx = _dot_nt(q, kx_ref[0])
    grp = lax.broadcasted_iota(jnp.int32, (g * blk, 1), 0) // blk
    sk = jnp.zeros((g * blk, 1), F32)
    for gi in range(g):
        sk = jnp.where(grp == gi, sink_ref[kvh * g + gi], sk)
    m = jnp.maximum(jnp.maximum(jnp.max(s, axis=-1, keepdims=True), jnp.max(sx, axis=-1, keepdims=True)), sk)
    p = jnp.exp(s - m)
    px = jnp.exp(sx - m)
    l = jnp.sum(p, axis=-1, keepdims=True) + jnp.sum(px, axis=-1, keepdims=True) + jnp.exp(sk - m)
    o = _dot(px.astype(BF16), vx_ref[0])
    for c, vr in enumerate((vp_ref, vc_ref, vn_ref)):
        o = o + _dot(p[:, c * blk:(c + 1) * blk].astype(BF16), vr[0])
    o = o / l
    o_ref[...] = jnp.concatenate([o[gi * blk:(gi + 1) * blk] for gi in range(g)], axis=1).astype(o_ref.dtype)


def _gqa(q, k, v, kx, vx, sink, batch, seq, ctx_len):
    hq, n, dh = q.shape
    hkv = k.shape[0]
    g = hq // hkv
    blk = GQA_BLOCK
    nb = seq // blk

    def kv_spec(c):
        return pl.BlockSpec((1, blk, dh), lambda b, i, h, s: (h, b * nb + jnp.clip(i - 1 + c, 0, nb - 1), 0))

    x_spec = pl.BlockSpec((1, ctx_len, dh), lambda b, i, h, s: (h, b, 0))
    grid_spec = pltpu.PrefetchScalarGridSpec(
        num_scalar_prefetch=1,
        grid=(batch, nb, hkv),
        in_specs=[pl.BlockSpec((g, blk, dh), lambda b, i, h, s: (h, b * nb + i, 0))]
        + [kv_spec(c) for c in range(3)] * 2 + [x_spec, x_spec],
        out_specs=pl.BlockSpec((blk, g * dh), lambda b, i, h, s: (b * nb + i, h)),
    )
    return pl.pallas_call(
        functools.partial(_gqa_body, nb=nb),
        grid_spec=grid_spec,
        out_shape=jax.ShapeDtypeStruct((n, hq * dh), BF16),
        compiler_params=_cparams("arbitrary", "arbitrary", "arbitrary"),
        name="window_gqa",
    )(sink, q, k, k, k, v, v, v, kx, vx)


def _top_values(s, k):
    tops = []
    for _ in range(k):
        m = jnp.max(s, axis=0, keepdims=True)
        tops.append(m)
        s = jnp.where(s == m, -jnp.inf, s)
    return tops


def _peer_route_body(x_ref, sh_ref, sc_ref, wq_ref, keys_ref, xm_ref, th_ref, s2_ref, e1_ref, e2_ref):
    nheads = s2_ref.shape[0]
    nk = keys_ref.shape[1]
    h = _ln(x_ref[...]) * (1.0 + sc_ref[0]) + sh_ref[0]
    ht = h.T.astype(BF16)
    xm_ref[...] = ht
    qt = _dot(wq_ref[...], ht).astype(BF16)
    qd = keys_ref.shape[2]
    nt = PEER_TOPK + 1
    pad = -nt % 8
    row = lax.broadcasted_iota(jnp.int32, (8, 1), 0)
    for hd in range(nheads):
        halves = []
        for half in range(2):
            hp = hd * 2 + half
            s = _dot(keys_ref[hp], qt[hp * qd:(hp + 1) * qd, :])
            halves.append((s, _top_values(s, nt)))
        (s1, t1), (s2, t2) = halves
        t2a = jnp.concatenate(t2 + [jnp.full_like(t2[0], -jnp.inf)] * pad, axis=0)
        slab_rows = [nt + pad] + [8] * (nt - 1)
        slabs = [t1[0] + t2a]
        for a in range(1, nt):
            slabs.append(jnp.where(row < nt // (a + 1), t1[a] + t2a[:8], -jnp.inf))
        cand = jnp.concatenate(slabs, axis=0)
        ctop = _top_values(cand, nt)
        tau = 0.5 * (ctop[PEER_TOPK - 1] + ctop[PEER_TOPK])
        cmax = t1[0] + t2[0]
        picked = jnp.concatenate([t2a[:rows] >= tau - t1[a] for a, rows in enumerate(slab_rows)], axis=0)
        z = jnp.sum(jnp.where(picked, jnp.exp(cand - cmax), 0.0), axis=0, keepdims=True)
        th_ref[hd] = tau - s1
        s2_ref[hd] = s2
        e1_ref[hd] = jnp.exp(s1 - t1[0])
        e2_ref[hd] = jnp.exp(s2 - t2[0]) * (0.5 / z)


def _peer_route(x, shift, scale, wq_t, keys, tokens_per_batch, tt=256):
    n, d = x.shape
    tt = min(tt, tokens_per_batch)
    per = tokens_per_batch // tt
    nh = keys.shape[0] // 2
    nk = keys.shape[1]
    big = pl.BlockSpec((nh, nk, tt), lambda i: (0, 0, i))
    big_shape = jax.ShapeDtypeStruct((nh, nk, n), F32)
    return pl.pallas_call(
        _peer_route_body,
        grid=(n // tt,),
        in_specs=[
            pl.BlockSpec((tt, d), lambda i: (i, 0)),
            pl.BlockSpec((1, 1, d), lambda i: (i // per, 0, 0)),
            pl.BlockSpec((1, 1, d), lambda i: (i // per, 0, 0)),
            pl.BlockSpec(wq_t.shape, lambda i: (0, 0)),
            pl.BlockSpec(keys.shape, lambda i: (0, 0, 0)),
        ],
        out_specs=[pl.BlockSpec((d, tt), lambda i: (0, i)), big, big, big, big],
        out_shape=[jax.ShapeDtypeStruct((d, n), BF16), big_shape, big_shape, big_shape, big_shape],
        compiler_params=_cparams("arbitrary"),
        name="peer_route",
    )(x, shift, scale, wq_t, keys)


PEER_COLS = 256


PEER_UNIT_ROWS = 4


def _peer_expert_body(xm_ref, u_ref, vt_ref, th_ref, s2_ref, e1_ref, e2_ref,
                      x_ref, gate_ref, g_ref, b_ref, o_ref, acc_ref, a_ref, w_ref, *, alpha):
    e = pl.program_id(1)
    nheads, n1, tt = th_ref.shape
    nk = s2_ref.shape[1]
    cw = min(PEER_COLS, tt)
    ur = min(PEER_UNIT_ROWS, n1)
    ue = ur * nk
    units = [(c, eh) for c in range(tt // cw) for eh in range(n1 // ur)]

    @pl.when(e == 0)
    def _():
        acc_ref[...] = jnp.zeros_like(acc_ref)

    def first_matmul(k):
        c, eh = units[k]
        a_ref[k % 2] = _dot(u_ref[eh * ue:(eh + 1) * ue, :], xm_ref[:, c * cw:(c + 1) * cw])

    def second_matmul(k):
        c, eh = units[k]
        cols = slice(c * cw, (c + 1) * cw)
        acc_ref[:, cols] += _dot(vt_ref[:, eh * ue:(eh + 1) * ue], w_ref[k % 2])

    def gate_and_activate(k):
        c, eh = units[k]
        slot = k % 2
        cols = slice(c * cw, (c + 1) * cw)
        for r in range(ur):
            row = eh * ur + r
            gsum = None
            for hd in range(nheads):
                keep = s2_ref[hd, :, cols] >= th_ref[hd, row:row + 1, cols]
                val = e2_ref[hd, :, cols] * e1_ref[hd, row:row + 1, cols]
                term = jnp.where(keep, val, 0.0)
                gsum = term if gsum is None else gsum + term
            a = a_ref[slot, r * nk:(r + 1) * nk, :]
            w_ref[slot, r * nk:(r + 1) * nk, :] = (gsum * a * (1.0 + lax.erf(a * (2.0 ** -0.5)))).astype(BF16)

    first_matmul(0)
    for k in range(len(units)):
        if k + 1 < len(units):
            first_matmul(k + 1)
        gate_and_activate(k)
        if k > 0:
            second_matmul(k - 1)
    second_matmul(len(units) - 1)

    @pl.when(e == pl.num_programs(1) - 1)
    def _():
        f = acc_ref[...].T
        z = alpha * x_ref[...] + gate_ref[0] * f
        o_ref[...] = _ln(z) * g_ref[...] + b_ref[...]


def _peer_experts(xm_t, u, v_t, th, s2, e1, e2, x, gate, g, b, tokens_per_batch, alpha, tt=512, et=1024):
    n, d = x.shape
    tt = min(tt, tokens_per_batch)
    per = tokens_per_batch // tt
    nh, nk, _ = s2.shape
    n1 = et // nk
    ne = u.shape[0] // et
    cw = min(PEER_COLS, tt)
    ur = min(PEER_UNIT_ROWS, n1)
    ue = ur * nk
    sel = pl.BlockSpec((nh, n1, tt), lambda i, e: (0, e, i))
    full = pl.BlockSpec((nh, nk, tt), lambda i, e: (0, 0, i))
    return pl.pallas_call(
        functools.partial(_peer_expert_body, alpha=alpha),
        grid=(n // tt, ne),
        in_specs=[
            pl.BlockSpec((d, tt), lambda i, e: (0, i)),
            pl.BlockSpec((et, d), lambda i, e: (e, 0)),
            pl.BlockSpec((d, et), lambda i, e: (0, e)),
            sel, full, sel, full,
            pl.BlockSpec((tt, d), lambda i, e: (i, 0)),
            pl.BlockSpec((1, 1, d), lambda i, e: (i // per, 0, 0)),
            pl.BlockSpec((1, d), lambda i, e: (0, 0)),
            pl.BlockSpec((1, d), lambda i, e: (0, 0)),
        ],
        out_specs=pl.BlockSpec((tt, d), lambda i, e: (i, 0)),
        out_shape=jax.ShapeDtypeStruct((n, d), F32),
        scratch_shapes=[
            pltpu.VMEM((d, tt), F32),
            pltpu.VMEM((2, ue, cw), F32),
            pltpu.VMEM((2, ue, cw), BF16),
        ],
        compiler_params=_cparams("arbitrary", "arbitrary"),
        name="peer_experts",
    )(xm_t, u, v_t, th, s2, e1, e2, x, gate, g.reshape(1, d), b.reshape(1, d))


def _peer_layer(x, shift, scale, gate, g, b, tables, tokens_per_batch, alpha):
    wq_t, keys, u, v_t = tables
    xm_t, th, s2, e1, e2 = _peer_route(x, shift, scale, wq_t, keys, tokens_per_batch)
    return _peer_experts(xm_t, u, v_t, th, s2, e1, e2, x, gate, g, b, tokens_per_batch, alpha)


def _rope_tables(seq):
    t = jnp.arange(seq)
    row = (t // GRID_W).astype(F32)
    col = (t % GRID_W).astype(F32)
    n_freq = HEAD_DIM // 4
    inv_freq = ROPE_THETA ** (-jnp.arange(n_freq, dtype=F32) / n_freq)
    ang = jnp.concatenate([row[:, None] * inv_freq, col[:, None] * inv_freq], -1)
    cos, sin = jnp.cos(ang), jnp.sin(ang)
    reps = LANES // (HEAD_DIM // 2)
    return jnp.tile(cos, (1, reps)), jnp.tile(sin, (1, reps))


def _rotate_half_columns(w, n_heads):
    d = w.shape[0]
    wh = w.reshape(d, n_heads, 2, HEAD_DIM // 2)
    return jnp.concatenate([-wh[:, :, 1], wh[:, :, 0]], axis=-1).reshape(d, n_heads * HEAD_DIM)


def _peer_tables(w_q, sub_keys, u, v):
    nh, _, nk, qd = sub_keys.shape
    return (w_q.T.astype(BF16), sub_keys.reshape(nh * 2, nk, qd).astype(BF16), u.astype(BF16), v.T.astype(BF16))


def kernel(x, c, ctx, c_ctx, ada_w, ada_b, post_ln_g, post_ln_b, even_w_in, even_w_out, na_rpb,
           odd_w_in, odd_w_out, gqa_sink, peer_w_q, peer_sub_keys, peer_u, peer_v):
    batch, seq, d = x.shape
    ctx_len = ctx.shape[1]
    depth = ada_w.shape[0]
    rows = seq // GRID_W
    alpha = float((2 * depth) ** 0.25)
    fw = FNET_GROUPS * HEAD_DIM
    nw = NA_HEADS * HEAD_DIM
    qw = GQA_Q_HEADS * HEAD_DIM
    kvw = GQA_KV_HEADS * HEAD_DIM
    qscale = HEAD_DIM ** -0.5

    cond = jnp.zeros((8, d), F32).at[:batch].set(c).at[batch].set(c_ctx)
    mods = _ada(cond, ada_w, ada_b)

    xl = x.reshape(batch * seq, d)
    hc = ctx.reshape(batch * ctx_len, d)
    cos_t, sin_t = _rope_tables(seq)

    cg, sg = _dft_tables(HEAD_DIM)
    eye = np.eye(FNET_GROUPS)
    chan = jnp.asarray(np.concatenate([np.kron(eye, cg), -np.kron(eye, sg)], axis=1), F32)

    for layer in range(depth):
        ctx_out = layer < depth - 1
        i = layer // 2
        m_l = [m.reshape(batch, 1, d) for m in jnp.split(mods[layer, :batch], 6, axis=-1)]
        m_c = [jnp.broadcast_to(m.reshape(1, 1, d), (batch, 1, d)) for m in jnp.split(mods[layer, batch], 6, axis=-1)]
        g0, b0 = post_ln_g[layer, 0], post_ln_b[layer, 0]
        g1, b1 = post_ln_g[layer, 1], post_ln_b[layer, 1]

        if layer % 2 == 0:
            w_in, w_out = even_w_in[i], even_w_out[i]
            w_f = _matmul_f32(w_in[:, :fw], chan)
            w_aug = jnp.concatenate([w_f, w_in[:, fw:]], axis=1).astype(BF16)
            plan = (("nat", 0, 2 * fw, 1.0, None),
                    ("heads", 2 * fw, nw, qscale, None),
                    ("heads", 2 * fw + nw, nw, 1.0, None),
                    ("heads", 2 * fw + 2 * nw, nw, 1.0, None))
            f_l, q_l, k_l, v_l = _proj(xl, m_l[0], m_l[1], w_aug, plan, seq)
            f_c, q_c, k_c, v_c = _proj(hc, m_c[0], m_c[1], w_aug, plan, ctx_len)
            col_bias, row_mask = _natten_tables(rows, na_rpb[i])
            na_l = _natten(q_l, k_l, v_l, k_c, v_c, col_bias, row_mask, batch, rows, ctx_len)
            fm_l = _fourier_latent(f_l, batch, rows, fw)
            w_out_b = w_out.astype(BF16)
            ws = [w_out_b[:fw], w_out_b[fw:]]
            xl_new = _outproj_ln([fm_l, na_l], ws, xl, m_l[2], g0, b0, seq, alpha)
            if ctx_out:
                na_c = _ctx_attn(q_c, k_c, v_c, batch, ctx_len)
                fm_c = _fourier_dense(f_c, batch, ctx_len, fw)
                hc_new = _outproj_ln([fm_c, na_c], ws, hc, m_c[2], g0, b0, ctx_len, alpha)
        else:
            w_in, w_out = odd_w_in[i], odd_w_out[i]
            wq, wk, wv = w_in[:, :qw], w_in[:, qw:qw + kvw], w_in[:, qw + kvw:]
            w_aug = jnp.concatenate([wq, wk, wv, _rotate_half_columns(wq, GQA_Q_HEADS),
                                     _rotate_half_columns(wk, GQA_KV_HEADS)], axis=1).astype(BF16)
            plan = (("heads", 0, qw, qscale, qw + 2 * kvw),
                    ("heads", qw, kvw, 1.0, 2 * qw + 2 * kvw),
                    ("heads", qw + kvw, kvw, 1.0, None))
            q_l, k_l, v_l = _proj(xl, m_l[0], m_l[1], w_aug, plan, seq, rope=(cos_t, sin_t))
            plan_c = (("heads", 0, kvw, 1.0, None), ("heads", kvw, kvw, 1.0, None))
            if ctx_out:
                raise NotImplementedError("an odd layer must be the last layer (no context output path)")
            k_c, v_c = _proj(hc, m_c[0], m_c[1], w_in[:, qw:].astype(BF16), plan_c, ctx_len)
            y_l = _gqa(q_l, k_l, v_l, k_c, v_c, gqa_sink[i], batch, seq, ctx_len)
            xl_new = _outproj_ln([y_l], [w_out.astype(BF16)], xl, m_l[2], g0, b0, seq, alpha)

        tables = _peer_tables(peer_w_q[layer], peer_sub_keys[layer], peer_u[layer], peer_v[layer])
        xl = _peer_layer(xl_new, m_l[3], m_l[4], m_l[5], g1, b1, tables, seq, alpha)
        if ctx_out:
            hc = _peer_layer(hc_new, m_c[3], m_c[4], m_c[5], g1, b1, tables, ctx_len, alpha)

    return xl.reshape(batch, seq, d)
```

```python
import functools
import math

import numpy as np
import jax
import jax.numpy as jnp
from jax import lax
from jax.experimental import pallas as pl
from jax.experimental.pallas import tpu as pltpu

F32 = jnp.float32
BF16 = jnp.bfloat16

HEAD_DIM = 64
GRID_W = 64
FNET_GROUPS = 8
NA_HEADS = 8
NA_KR = 8
NA_KC = 16
NA_ROWS_PER_BLOCK = 8
NA_WIN_ROWS = 16
GQA_Q_HEADS = 16
GQA_KV_HEADS = 4
GQA_WINDOW = 128
GQA_BLOCK = 128
ROPE_THETA = 10000.0
PEER_HEADS = 8
PEER_NKEYS = 128
PEER_TOPK = 16
LN_EPS = 1e-6
NEG = -1e30

PEER_COLS = 256
LANES = 128
VMEM_LIMIT = 56 * 1024 * 1024


def _cparams(*sem):
    return pltpu.CompilerParams(dimension_semantics=sem, vmem_limit_bytes=VMEM_LIMIT)


def _ln(x):
    mu = jnp.mean(x, axis=-1, keepdims=True)
    xc = x - mu
    var = jnp.mean(xc * xc, axis=-1, keepdims=True)
    return xc * lax.rsqrt(var + LN_EPS)


def _dot(a, b):
    return jnp.dot(a, b, preferred_element_type=F32)


def _dot_nt(a, b):
    return lax.dot_general(a, b, (((1,), (1,)), ((), ())), preferred_element_type=F32)


def _ada_body(c_ref, w_ref, b_ref, o_ref):
    c = c_ref[...]
    o_ref[0] = _dot(c * jax.nn.sigmoid(c), w_ref[0]) + b_ref[0]


def _ada(cond, ada_w, ada_b):
    depth, d, n = ada_w.shape
    tn = 1536
    return pl.pallas_call(
        _ada_body,
        grid=(depth, n // tn),
        in_specs=[
            pl.BlockSpec((8, d), lambda l, j: (0, 0)),
            pl.BlockSpec((1, d, tn), lambda l, j: (l, 0, j)),
            pl.BlockSpec((1, 1, tn), lambda l, j: (l, 0, j)),
        ],
        out_specs=pl.BlockSpec((1, 8, tn), lambda l, j: (l, 0, j)),
        out_shape=jax.ShapeDtypeStruct((depth, 8, n), F32),
        compiler_params=_cparams("arbitrary", "arbitrary"),
        name="ada_modulation",
    )(cond, ada_w, ada_b.reshape(depth, 1, n))


def _matmul_f32_body(a_ref, b_ref, o_ref):
    o_ref[...] = jnp.dot(a_ref[...], b_ref[...], preferred_element_type=F32, precision=lax.Precision.HIGHEST)


def _matmul_f32(a, b):
    m, k = a.shape
    n = b.shape[1]
    tm = min(256, m)
    return pl.pallas_call(
        _matmul_f32_body,
        grid=(m // tm,),
        in_specs=[pl.BlockSpec((tm, k), lambda i: (i, 0)), pl.BlockSpec((k, n), lambda i: (0, 0))],
        out_specs=pl.BlockSpec((tm, n), lambda i: (i, 0)),
        out_shape=jax.ShapeDtypeStruct((m, n), F32),
        compiler_params=_cparams("arbitrary"),
        name="small_matmul_f32",
    )(a, b)


def _proj_body(*refs, plan, use_rope):
    x_ref, sh_ref, sc_ref, w_ref = refs[:4]
    rest = refs[4:]
    if use_rope:
        cos_ref, sin_ref = rest[:2]
        rest = rest[2:]
    h = _ln(x_ref[...]) * (1.0 + sc_ref[0]) + sh_ref[0]
    acc = _dot(h.astype(BF16), w_ref[...])
    for o_ref, (kind, start, width, scale, rot_start) in zip(rest, plan):
        y = acc[:, start:start + width]
        if rot_start is not None:
            reps = width // LANES
            cos = jnp.tile(cos_ref[...], (1, reps))
            sin = jnp.tile(sin_ref[...], (1, reps))
            y = y * cos + acc[:, rot_start:rot_start + width] * sin
        if scale != 1.0:
            y = y * scale
        if kind == "nat":
            o_ref[...] = y.astype(o_ref.dtype)
        else:
            for hh in range(width // HEAD_DIM):
                o_ref[hh] = y[:, hh * HEAD_DIM:(hh + 1) * HEAD_DIM].astype(o_ref.dtype)


def _proj(x, shift, scale, w, plan, tokens_per_batch, rope=None, tm=512):
    n, d = x.shape
    tm = min(tm, tokens_per_batch)
    per = tokens_per_batch // tm
    in_specs = [
        pl.BlockSpec((tm, d), lambda i: (i, 0)),
        pl.BlockSpec((1, 1, d), lambda i: (i // per, 0, 0)),
        pl.BlockSpec((1, 1, d), lambda i: (i // per, 0, 0)),
        pl.BlockSpec(w.shape, lambda i: (0, 0)),
    ]
    args = [x, shift, scale, w]
    if rope is not None:
        in_specs += [pl.BlockSpec((tm, LANES), lambda i: (i % per, 0))] * 2
        args += list(rope)
    out_specs, out_shape = [], []
    for kind, start, width, sc, rot in plan:
        if kind == "nat":
            out_specs.append(pl.BlockSpec((tm, width), lambda i: (i, 0)))
            out_shape.append(jax.ShapeDtypeStruct((n, width), F32))
        else:
            nh = width // HEAD_DIM
            out_specs.append(pl.BlockSpec((nh, tm, HEAD_DIM), lambda i: (0, i, 0)))
            out_shape.append(jax.ShapeDtypeStruct((nh, n, HEAD_DIM), BF16))
    return pl.pallas_call(
        functools.partial(_proj_body, plan=plan, use_rope=rope is not None),
        grid=(n // tm,),
        in_specs=in_specs,
        out_specs=out_specs,
        out_shape=out_shape,
        compiler_params=_cparams("arbitrary"),
        name="modln_proj",
    )(*args)


def _outproj_body(*refs, n_in, alpha):
    ys, ws = refs[:n_in], refs[n_in:2 * n_in]
    x_ref, gate_ref, g_ref, b_ref, o_ref = refs[2 * n_in:]
    acc = None
    for y_ref, w_ref in zip(ys, ws):
        t = _dot(y_ref[...].astype(BF16), w_ref[...])
        acc = t if acc is None else acc + t
    z = alpha * x_ref[...] + gate_ref[0] * acc
    o_ref[...] = _ln(z) * g_ref[...] + b_ref[...]


def _outproj_ln(ys, ws, x, gate, g, b, tokens_per_batch, alpha, tm=512):
    n, d = x.shape
    tm = min(tm, tokens_per_batch)
    per = tokens_per_batch // tm
    in_specs = [pl.BlockSpec((tm, y.shape[1]), lambda i: (i, 0)) for y in ys]
    in_specs += [pl.BlockSpec(w.shape, lambda i: (0, 0)) for w in ws]
    in_specs += [
        pl.BlockSpec((tm, d), lambda i: (i, 0)),
        pl.BlockSpec((1, 1, d), lambda i: (i // per, 0, 0)),
        pl.BlockSpec((1, d), lambda i: (0, 0)),
        pl.BlockSpec((1, d), lambda i: (0, 0)),
    ]
    return pl.pallas_call(
        functools.partial(_outproj_body, n_in=len(ys), alpha=alpha),
        grid=(n // tm,),
        in_specs=in_specs,
        out_specs=pl.BlockSpec((tm, d), lambda i: (i, 0)),
        out_shape=jax.ShapeDtypeStruct((n, d), F32),
        compiler_params=_cparams("arbitrary"),
        name="outproj_residual_ln",
    )(*ys, *ws, x, gate, g.reshape(1, d), b.reshape(1, d))


def _dft_tables(n):
    idx = np.arange(n)
    ang = 2.0 * np.pi * ((idx[:, None] * idx[None, :]) % n) / n
    return np.cos(ang), np.sin(ang)


def _fourier_rows_body(x_ref, cs_ref, tc_ref, ts_ref, o_ref, *, cb, width):
    nr = cs_ref.shape[1]
    pq = _dot(cs_ref[...], x_ref[0])
    for j in range(cb):
        a0 = j * 2 * width
        pa, pb = pq[:nr, a0:a0 + width], pq[:nr, a0 + width:a0 + 2 * width]
        qa, qb = pq[nr:, a0:a0 + width], pq[nr:, a0 + width:a0 + 2 * width]
        yr = pa + qb
        yi = pb - qa
        tc, ts = tc_ref[j], ts_ref[j]
        o_ref[0, j, :, :width] = yr * tc + yi * ts
        o_ref[0, j, :, width:] = yi * tc - yr * ts


def _fourier_cols_body(y_ref, cs_ref, o_ref, *, kb, width, norm):
    nc = cs_ref.shape[1]
    pq = _dot(cs_ref[...], y_ref[0])
    for j in range(kb):
        a0 = j * 2 * width
        z = pq[:nc, a0:a0 + width] + pq[nc:, a0 + width:a0 + 2 * width]
        o_ref[0, :, j * width:(j + 1) * width] = z * norm


def _fourier_latent(f, batch, rows, width):
    cols = GRID_W
    seq = rows * cols
    c_r, s_r = _dft_tables(rows)
    c_c, s_c = _dft_tables(cols)
    k1 = np.arange(rows)[None, :]
    cc = np.arange(cols)[:, None]
    tw = 2.0 * np.pi * ((cc * k1) % seq) / seq
    cs_r = jnp.asarray(np.concatenate([c_r, s_r], 0), F32)
    cs_c = jnp.asarray(np.concatenate([c_c, s_c], 0), F32)
    tc = jnp.asarray(np.cos(tw)[:, :, None], F32)
    ts = jnp.asarray(np.sin(tw)[:, :, None], F32)
    cb = 4
    kb = 8
    lane_w = 2 * width
    y = pl.pallas_call(
        functools.partial(_fourier_rows_body, cb=cb, width=width),
        grid=(batch, cols // cb),
        in_specs=[
            pl.BlockSpec((1, rows, cb * lane_w), lambda b, j: (b, 0, j)),
            pl.BlockSpec((2 * rows, rows), lambda b, j: (0, 0)),
            pl.BlockSpec((cb, rows, 1), lambda b, j: (j, 0, 0)),
            pl.BlockSpec((cb, rows, 1), lambda b, j: (j, 0, 0)),
        ],
        out_specs=pl.BlockSpec((1, cb, rows, lane_w), lambda b, j: (b, j, 0, 0)),
        out_shape=jax.ShapeDtypeStruct((batch, cols, rows, lane_w), F32),
        compiler_params=_cparams("arbitrary", "arbitrary"),
        name="fourier_rows",
    )(f.reshape(batch, rows, cols * lane_w), cs_r, tc, ts)
    z = pl.pallas_call(
        functools.partial(_fourier_cols_body, kb=kb, width=width, norm=float((seq * HEAD_DIM) ** -0.5)),
        grid=(batch, rows // kb),
        in_specs=[
            pl.BlockSpec((1, cols, kb * lane_w), lambda b, j: (b, 0, j)),
            pl.BlockSpec((2 * cols, cols), lambda b, j: (0, 0)),
        ],
        out_specs=pl.BlockSpec((1, cols, kb * width), lambda b, j: (b, 0, j)),
        out_shape=jax.ShapeDtypeStruct((batch, cols, rows * width), F32),
        compiler_params=_cparams("arbitrary", "arbitrary"),
        name="fourier_cols",
    )(y.reshape(batch, cols, rows * lane_w), cs_c)
    return z.reshape(batch * seq, width)


def _fourier_dense_body(x_ref, c_ref, s_ref, o_ref, *, width, norm):
    x = x_ref[...]
    o_ref[...] = (_dot(c_ref[...], x[:, :width]) + _dot(s_ref[...], x[:, width:])) * norm


def _fourier_dense(f, batch, length, width):
    c, s = _dft_tables(length)
    return pl.pallas_call(
        functools.partial(_fourier_dense_body, width=width, norm=float((length * HEAD_DIM) ** -0.5)),
        grid=(batch,),
        in_specs=[
            pl.BlockSpec((length, 2 * width), lambda b: (b, 0)),
            pl.BlockSpec((length, length), lambda b: (0, 0)),
            pl.BlockSpec((length, length), lambda b: (0, 0)),
        ],
        out_specs=pl.BlockSpec((length, width), lambda b: (b, 0)),
        out_shape=jax.ShapeDtypeStruct((batch * length, width), F32),
        compiler_params=_cparams("arbitrary"),
        name="fourier_dense",
    )(f, jnp.asarray(c, F32), jnp.asarray(s, F32))


def _natten_tables(rows, rpb):
    w = GRID_W
    qb, kw = NA_ROWS_PER_BLOCK, NA_WIN_ROWS
    nh, ndr, ndc = rpb.shape
    tq, tk = np.divmod(np.arange(w * w), w)
    dc_i = np.clip(tk - tq, -(NA_KC - 1), NA_KC - 1) + (NA_KC - 1)
    onehot = np.zeros((LANES, w * w), np.float32)
    onehot[dc_i, np.arange(w * w)] = 1.0
    c_start = np.clip(tq - NA_KC // 2, 0, w - NA_KC)
    col_ok = ((tk >= c_start) & (tk < c_start + NA_KC)).reshape(w, w)
    rp = jnp.zeros((LANES, LANES), F32).at[:nh * ndr, :ndc].set(rpb.reshape(nh * ndr, ndc).astype(F32))
    tiles = _matmul_f32(rp, jnp.asarray(onehot))[:nh * ndr].reshape(nh, ndr, w, w)
    tiles = jnp.where(jnp.asarray(col_ok), tiles, NEG)
    dr = np.arange(kw)[None, :] - np.arange(qb)[:, None] - (kw - qb) // 2
    dr_i = np.clip(dr + (NA_KR - 1), 0, ndr - 1)
    col_bias = jnp.transpose(tiles[:, dr_i], (0, 1, 3, 2, 4)).reshape(nh, qb * w, kw * w)
    qi = np.repeat(np.arange(qb), w)
    km = np.repeat(np.arange(kw), w)
    nblk = rows // qb
    masks = []
    for j in range(nblk):
        qr = j * qb + qi
        kr = j * qb - (kw - qb) // 2 + km
        r_start = np.clip(qr - NA_KR // 2, 0, rows - NA_KR)
        ok = (kr[None, :] >= r_start[:, None]) & (kr[None, :] < r_start[:, None] + NA_KR)
        masks.append(np.where(ok, 0.0, NEG).astype(np.float32))
    kinds = [(j > 0) + (j == nblk - 1) for j in range(nblk)]
    table = np.zeros((3,) + masks[0].shape, np.float32)
    for j, kd in enumerate(kinds):
        table[kd] = masks[j]
    for j, kd in enumerate(kinds):
        assert np.array_equal(table[kd], masks[j])
    return col_bias, jnp.asarray(table)


def _natten_body(q_ref, k0_ref, k1_ref, k2_ref, k3_ref, v0_ref, v1_ref, v2_ref, v3_ref,
                 kc_ref, vc_ref, cb_ref, rm_ref, o_ref):
    k_refs = (k0_ref, k1_ref, k2_ref, k3_ref)
    v_refs = (v0_ref, v1_ref, v2_ref, v3_ref)
    outs = []
    for hh in range(q_ref.shape[0]):
        q = q_ref[hh]
        s = jnp.concatenate([_dot_nt(q, kr[hh]) for kr in k_refs], axis=1)
        s = s + cb_ref[hh] + rm_ref[0]
        sc = _dot_nt(q, kc_ref[hh])
        m = jnp.maximum(jnp.max(s, axis=-1, keepdims=True), jnp.max(sc, axis=-1, keepdims=True))
        p = jnp.exp(s - m)
        pc = jnp.exp(sc - m)
        l = jnp.sum(p, axis=-1, keepdims=True) + jnp.sum(pc, axis=-1, keepdims=True)
        kb = k0_ref.shape[1]
        o = _dot(pc.astype(BF16), vc_ref[hh])
        for c, vr in enumerate(v_refs):
            o = o + _dot(p[:, c * kb:(c + 1) * kb].astype(BF16), vr[hh])
        outs.append(o / l)
    o_ref[...] = jnp.concatenate(outs, axis=1).astype(o_ref.dtype)


def _natten(q, k, v, kc, vc, col_bias, row_mask, batch, rows, ctx_len):
    nh, n, dh = q.shape
    hp = 2
    qt = NA_ROWS_PER_BLOCK * GRID_W
    kt = NA_WIN_ROWS * GRID_W // 4
    nblk = rows // NA_ROWS_PER_BLOCK
    kblocks = rows * GRID_W // kt

    def kv_spec(c):
        return pl.BlockSpec(
            (hp, kt, dh),
            lambda b, j, h: (h, b * kblocks + jnp.clip(2 * j - 1 + c, 0, kblocks - 1), 0))

    in_specs = [pl.BlockSpec((hp, qt, dh), lambda b, j, h: (h, b * nblk + j, 0))]
    in_specs += [kv_spec(c) for c in range(4)] * 2
    in_specs += [
        pl.BlockSpec((hp, ctx_len, dh), lambda b, j, h: (h, b, 0)),
        pl.BlockSpec((hp, ctx_len, dh), lambda b, j, h: (h, b, 0)),
        pl.BlockSpec((hp, qt, 4 * kt), lambda b, j, h: (h, 0, 0)),
        pl.BlockSpec((1, qt, 4 * kt), lambda b, j, h: ((j > 0).astype(jnp.int32) + (j == nblk - 1).astype(jnp.int32), 0, 0)),
    ]
    return pl.pallas_call(
        _natten_body,
        grid=(batch, nblk, nh // hp),
        in_specs=in_specs,
        out_specs=pl.BlockSpec((qt, hp * dh), lambda b, j, h: (b * nblk + j, h)),
        out_shape=jax.ShapeDtypeStruct((n, nh * dh), BF16),
        compiler_params=_cparams("arbitrary", "arbitrary", "arbitrary"),
        name="neighbourhood_attention",
    )(q, k, k, k, k, v, v, v, v, kc, vc, col_bias, row_mask)


def _ctx_attn_body(q_ref, k_ref, v_ref, o_ref):
    outs = []
    for hh in range(q_ref.shape[0]):
        s = _dot_nt(q_ref[hh], k_ref[hh])
        m = jnp.max(s, axis=-1, keepdims=True)
        p = jnp.exp(s - m)
        l = jnp.sum(p, axis=-1, keepdims=True)
        outs.append(_dot(p.astype(BF16), v_ref[hh]) / l)
    o_ref[...] = jnp.concatenate(outs, axis=1).astype(o_ref.dtype)


def _ctx_attn(q, k, v, batch, ctx_len):
    nh, n, dh = q.shape
    hp = 2
    spec = pl.BlockSpec((hp, ctx_len, dh), lambda b, h: (h, b, 0))
    return pl.pallas_call(
        _ctx_attn_body,
        grid=(batch, nh // hp),
        in_specs=[spec, spec, spec],
        out_specs=pl.BlockSpec((ctx_len, hp * dh), lambda b, h: (b, h)),
        out_shape=jax.ShapeDtypeStruct((n, nh * dh), BF16),
        compiler_params=_cparams("arbitrary", "arbitrary"),
        name="context_attention",
    )(q, k, v)


def _gqa_body(sink_ref, q_ref, kp_ref, kc_ref, kn_ref, vp_ref, vc_ref, vn_ref, kx_ref, vx_ref, o_ref, *, nb):
    g, blk, dh = q_ref.shape
    n = pl.program_id(1)
    kvh = pl.program_id(2)
    q = q_ref[...].reshape(g * blk, dh)
    s = jnp.concatenate([_dot_nt(q, r[0]) for r in (kp_ref, kc_ref, kn_ref)], axis=1)
    qi = lax.broadcasted_iota(jnp.int32, s.shape, 0) % blk
    kp = lax.broadcasted_iota(jnp.int32, s.shape, 1) - blk
    ok = (jnp.abs(qi - kp) <= GQA_WINDOW) & ((kp >= 0) | (n > 0)) & ((kp < blk) | (n < nb - 1))
    s = jnp.where(ok, s, NEG)
    sx = _dot_nt(q, kx_ref[0])
    grp = lax.broadcasted_iota(jnp.int32, (g * blk, 1), 0) // blk
    sk = jnp.zeros((g * blk, 1), F32)
    for gi in range(g):
        sk = jnp.where(grp == gi, sink_ref[kvh * g + gi], sk)
    m = jnp.maximum(jnp.maximum(jnp.max(s, axis=-1, keepdims=True), jnp.max(sx, axis=-1, keepdims=True)), sk)
    p = jnp.exp(s - m)
    px = jnp.exp(sx - m)
    l = jnp.sum(p, axis=-1, keepdims=True) + jnp.sum(px, axis=-1, keepdims=True) + jnp.exp(sk - m)
    o = _dot(px.astype(BF16), vx_ref[0])
    for c, vr in enumerate((vp_ref, vc_ref, vn_ref)):
        o = o + _dot(p[:, c * blk:(c + 1) * blk].astype(BF16), vr[0])
    o = o / l
    o_ref[...] = jnp.concatenate([o[gi * blk:(gi + 1) * blk] for gi in range(g)], axis=1).astype(o_ref.dtype)


def _gqa(q, k, v, kx, vx, sink, batch, seq, ctx_len):
    hq, n, dh = q.shape
    hkv = k.shape[0]
    g = hq // hkv
    blk = GQA_BLOCK
    nb = seq // blk

    def kv_spec(c):
        return pl.BlockSpec((1, blk, dh), lambda b, i, h, s: (h, b * nb + jnp.clip(i - 1 + c, 0, nb - 1), 0))

    x_spec = pl.BlockSpec((1, ctx_len, dh), lambda b, i, h, s: (h, b, 0))
    grid_spec = pltpu.PrefetchScalarGridSpec(
        num_scalar_prefetch=1,
        grid=(batch, nb, hkv),
        in_specs=[pl.BlockSpec((g, blk, dh), lambda b, i, h, s: (h, b * nb + i, 0))]
        + [kv_spec(c) for c in range(3)] * 2 + [x_spec, x_spec],
        out_specs=pl.BlockSpec((blk, g * dh), lambda b, i, h, s: (b * nb + i, h)),
    )
    return pl.pallas_call(
        functools.partial(_gqa_body, nb=nb),
        grid_spec=grid_spec,
        out_shape=jax.ShapeDtypeStruct((n, hq * dh), BF16),
        compiler_params=_cparams("arbitrary", "arbitrary", "arbitrary"),
        name="window_gqa",
    )(sink, q, k, k, k, v, v, v, kx, vx)


def _top_values(s, k):
    tops = []
    for _ in range(k):
        m = jnp.max(s, axis=0, keepdims=True)
        tops.append(m)
        s = jnp.where(s == m, -jnp.inf, s)
    return tops


def _peer_route_body(x_ref, sh_ref, sc_ref, wq_ref, keys_ref, xm_ref, th_ref, s2_ref, e1_ref, e2_ref):
    xm_ref, th_ref, s2_ref, e1_ref, e2_ref = (r.at[0] for r in (xm_ref, th_ref, s2_ref, e1_ref, e2_ref))
    nheads = s2_ref.shape[0]
    nk = keys_ref.shape[1]
    h = _ln(x_ref[...]) * (1.0 + sc_ref[0]) + sh_ref[0]
    ht = h.T.astype(BF16)
    xm_ref[...] = ht
    qt = _dot(wq_ref[...], ht).astype(BF16)
    qd = keys_ref.shape[2]
    nt = PEER_TOPK + 1
    pad = -nt % 8
    row = lax.broadcasted_iota(jnp.int32, (8, 1), 0)
    for hd in range(nheads):
        halves = []
        for half in range(2):
            hp = hd * 2 + half
            s = _dot(keys_ref[hp], qt[hp * qd:(hp + 1) * qd, :])
            halves.append((s, _top_values(s, nt)))
        (s1, t1), (s2, t2) = halves
        t2a = jnp.concatenate(t2 + [jnp.full_like(t2[0], -jnp.inf)] * pad, axis=0)
        slab_rows = [nt + pad] + [8] * (nt - 1)
        slabs = [t1[0] + t2a]
        for a in range(1, nt):
            slabs.append(jnp.where(row < nt // (a + 1), t1[a] + t2a[:8], -jnp.inf))
        cand = jnp.concatenate(slabs, axis=0)
        ctop = _top_values(cand, nt)
        tau = 0.5 * (ctop[PEER_TOPK - 1] + ctop[PEER_TOPK])
        cmax = t1[0] + t2[0]
        picked = jnp.concatenate([t2a[:rows] >= tau - t1[a] for a, rows in enumerate(slab_rows)], axis=0)
        z = jnp.sum(jnp.where(picked, jnp.exp(cand - cmax), 0.0), axis=0, keepdims=True)
        th_ref[hd] = tau - s1
        s2_ref[hd] = s2
        e1_ref[hd] = jnp.exp(s1 - t1[0])
        e2_ref[hd] = jnp.exp(s2 - t2[0]) * (0.5 / z)


def _peer_route(x, shift, scale, wq_t, keys, tokens_per_batch):
    n, d = x.shape
    tt = PEER_COLS
    per = tokens_per_batch // tt
    nh = keys.shape[0] // 2
    nk = keys.shape[1]
    big = pl.BlockSpec((1, nh, nk, tt), lambda i: (i, 0, 0, 0))
    big_shape = jax.ShapeDtypeStruct((n // tt, nh, nk, tt), F32)
    return pl.pallas_call(
        _peer_route_body,
        grid=(n // tt,),
        in_specs=[
            pl.BlockSpec((tt, d), lambda i: (i, 0)),
            pl.BlockSpec((1, 1, d), lambda i: (i // per, 0, 0)),
            pl.BlockSpec((1, 1, d), lambda i: (i // per, 0, 0)),
            pl.BlockSpec(wq_t.shape, lambda i: (0, 0)),
            pl.BlockSpec(keys.shape, lambda i: (0, 0, 0)),
        ],
        out_specs=[pl.BlockSpec((1, d, tt), lambda i: (i, 0, 0)), big, big, big, big],
        out_shape=[jax.ShapeDtypeStruct((n // tt, d, tt), BF16), big_shape, big_shape, big_shape, big_shape],
        compiler_params=_cparams("arbitrary"),
        name="peer_route",
    )(x, shift, scale, wq_t, keys)


PEER_UNIT_ROWS = 4


def _peer_expert_body(xm_ref, u_ref, vt_ref, th_ref, s2_ref, e1_ref, e2_ref,
                      x_ref, gate_ref, g_ref, b_ref, o_ref, acc_ref, a_ref, w_ref, *, alpha):
    e = pl.program_id(1)
    ncb, nheads, n1, cw = th_ref.shape
    nk = s2_ref.shape[2]
    ur = PEER_UNIT_ROWS
    ue = ur * nk
    nrp = n1 // ur
    n_units = ncb * nrp

    @pl.when(e == 0)
    def _():
        acc_ref[...] = jnp.zeros_like(acc_ref)

    def first_matmul(i, slot):
        rows = pl.ds(pl.multiple_of((i % nrp) * ue, ue), ue)
        a_ref[slot] = _dot(u_ref[rows, :], xm_ref[i // nrp])

    def second_matmul(i, slot):
        acc_ref[i // nrp] += _dot(vt_ref[i % nrp], w_ref[slot])

    def gate_and_activate(i, slot):
        c = i // nrp
        for r in range(ur):
            row = pl.ds((i % nrp) * ur + r, 1)
            gsum = None
            for hd in range(nheads):
                keep = s2_ref[c, hd] >= th_ref[c, hd, row, :]
                val = e2_ref[c, hd] * e1_ref[c, hd, row, :]
                term = jnp.where(keep, val, 0.0)
                gsum = term if gsum is None else gsum + term
            a = a_ref[slot, r * nk:(r + 1) * nk, :]
            w_ref[slot, r * nk:(r + 1) * nk, :] = (gsum * a * (1.0 + lax.erf(a * (2.0 ** -0.5)))).astype(BF16)

    def steady(j, carry):
        i = 2 * j + 1
        first_matmul(i + 1, 0)
        gate_and_activate(i, 1)
        second_matmul(i - 1, 0)
        first_matmul(i + 2, 1)
        gate_and_activate(i + 1, 0)
        second_matmul(i, 1)
        return carry

    assert n_units % 2 == 0
    first_matmul(0, 0)
    first_matmul(1, 1)
    gate_and_activate(0, 0)
    lax.fori_loop(0, n_units // 2 - 1, steady, 0)
    gate_and_activate(n_units - 1, 1)
    second_matmul(n_units - 2, 0)
    second_matmul(n_units - 1, 1)

    @pl.when(e == pl.num_programs(1) - 1)
    def _():
        for c in range(ncb):
            f = acc_ref[c].T
            z = alpha * x_ref[c * cw:(c + 1) * cw, :] + gate_ref[0] * f
            o_ref[c * cw:(c + 1) * cw, :] = _ln(z) * g_ref[...] + b_ref[...]


def _peer_experts(xm_t, u, v_t, th, s2, e1, e2, x, gate, g, b, tokens_per_batch, alpha, tt=512, et=2048):
    n, d = x.shape
    tt = min(tt, tokens_per_batch)
    per = tokens_per_batch // tt
    _, nh, nk, cw = s2.shape
    ncb = tt // cw
    n1 = et // nk
    ue = PEER_UNIT_ROWS * nk
    sel = pl.BlockSpec((ncb, nh, n1, cw), lambda i, e: (i, 0, e, 0))
    full = pl.BlockSpec((ncb, nh, nk, cw), lambda i, e: (i, 0, 0, 0))
    return pl.pallas_call(
        functools.partial(_peer_expert_body, alpha=alpha),
        grid=(n // tt, u.shape[0] // et),
        in_specs=[
            pl.BlockSpec((ncb, d, cw), lambda i, e: (i, 0, 0)),
            pl.BlockSpec((et, d), lambda i, e: (e, 0)),
            pl.BlockSpec((et // ue, d, ue), lambda i, e: (e, 0, 0)),
            sel, full, sel, full,
            pl.BlockSpec((tt, d), lambda i, e: (i, 0)),
            pl.BlockSpec((1, 1, d), lambda i, e: (i // per, 0, 0)),
            pl.BlockSpec((1, d), lambda i, e: (0, 0)),
            pl.BlockSpec((1, d), lambda i, e: (0, 0)),
        ],
        out_specs=pl.BlockSpec((tt, d), lambda i, e: (i, 0)),
        out_shape=jax.ShapeDtypeStruct((n, d), F32),
        scratch_shapes=[
            pltpu.VMEM((ncb, d, cw), F32),
            pltpu.VMEM((2, ue, cw), F32),
            pltpu.VMEM((2, ue, cw), BF16),
        ],
        compiler_params=_cparams("arbitrary", "arbitrary"),
        name="peer_experts",
    )(xm_t, u, v_t, th, s2, e1, e2, x, gate, g.reshape(1, d), b.reshape(1, d))


def _peer_layer(x, shift, scale, gate, g, b, tables, tokens_per_batch, alpha):
    wq_t, keys, u, v_t = tables
    xm_t, th, s2, e1, e2 = _peer_route(x, shift, scale, wq_t, keys, tokens_per_batch)
    return _peer_experts(xm_t, u, v_t, th, s2, e1, e2, x, gate, g, b, tokens_per_batch, alpha)


def _rope_tables(seq):
    t = jnp.arange(seq)
    row = (t // GRID_W).astype(F32)
    col = (t % GRID_W).astype(F32)
    n_freq = HEAD_DIM // 4
    inv_freq = ROPE_THETA ** (-jnp.arange(n_freq, dtype=F32) / n_freq)
    ang = jnp.concatenate([row[:, None] * inv_freq, col[:, None] * inv_freq], -1)
    cos, sin = jnp.cos(ang), jnp.sin(ang)
    reps = LANES // (HEAD_DIM // 2)
    return jnp.tile(cos, (1, reps)), jnp.tile(sin, (1, reps))


def _rotate_half_columns(w, n_heads):
    d = w.shape[0]
    wh = w.reshape(d, n_heads, 2, HEAD_DIM // 2)
    return jnp.concatenate([-wh[:, :, 1], wh[:, :, 0]], axis=-1).reshape(d, n_heads * HEAD_DIM)


def _peer_tables(w_q, sub_keys, u, v):
    nh, _, nk, qd = sub_keys.shape
    ue = PEER_UNIT_ROWS * nk
    v_chunks = jnp.transpose(v.astype(BF16).reshape(v.shape[0] // ue, ue, v.shape[1]), (0, 2, 1))
    return (w_q.T.astype(BF16), sub_keys.reshape(nh * 2, nk, qd).astype(BF16), u.astype(BF16), v_chunks)


def kernel(x, c, ctx, c_ctx, ada_w, ada_b, post_ln_g, post_ln_b, even_w_in, even_w_out, na_rpb,
           odd_w_in, odd_w_out, gqa_sink, peer_w_q, peer_sub_keys, peer_u, peer_v):
    batch, seq, d = x.shape
    ctx_len = ctx.shape[1]
    depth = ada_w.shape[0]
    rows = seq // GRID_W
    alpha = float((2 * depth) ** 0.25)
    fw = FNET_GROUPS * HEAD_DIM
    nw = NA_HEADS * HEAD_DIM
    qw = GQA_Q_HEADS * HEAD_DIM
    kvw = GQA_KV_HEADS * HEAD_DIM
    qscale = HEAD_DIM ** -0.5

    cond = jnp.zeros((8, d), F32).at[:batch].set(c).at[batch].set(c_ctx)
    mods = _ada(cond, ada_w, ada_b)

    xl = x.reshape(batch * seq, d)
    hc = ctx.reshape(batch * ctx_len, d)
    cos_t, sin_t = _rope_tables(seq)

    cg, sg = _dft_tables(HEAD_DIM)
    eye = np.eye(FNET_GROUPS)
    chan = jnp.asarray(np.concatenate([np.kron(eye, cg), -np.kron(eye, sg)], axis=1), F32)

    for layer in range(depth):
        ctx_out = layer < depth - 1
        i = layer // 2
        m_l = [m.reshape(batch, 1, d) for m in jnp.split(mods[layer, :batch], 6, axis=-1)]
        m_c = [jnp.broadcast_to(m.reshape(1, 1, d), (batch, 1, d)) for m in jnp.split(mods[layer, batch], 6, axis=-1)]
        g0, b0 = post_ln_g[layer, 0], post_ln_b[layer, 0]
        g1, b1 = post_ln_g[layer, 1], post_ln_b[layer, 1]

        if layer % 2 == 0:
            w_in, w_out = even_w_in[i], even_w_out[i]
            w_f = _matmul_f32(w_in[:, :fw], chan)
            w_aug = jnp.concatenate([w_f, w_in[:, fw:]], axis=1).astype(BF16)
            plan = (("nat", 0, 2 * fw, 1.0, None),
                    ("heads", 2 * fw, nw, qscale, None),
                    ("heads", 2 * fw + nw, nw, 1.0, None),
                    ("heads", 2 * fw + 2 * nw, nw, 1.0, None))
            f_l, q_l, k_l, v_l = _proj(xl, m_l[0], m_l[1], w_aug, plan, seq)
            f_c, q_c, k_c, v_c = _proj(hc, m_c[0], m_c[1], w_aug, plan, ctx_len)
            col_bias, row_mask = _natten_tables(rows, na_rpb[i])
            na_l = _natten(q_l, k_l, v_l, k_c, v_c, col_bias, row_mask, batch, rows, ctx_len)
            fm_l = _fourier_latent(f_l, batch, rows, fw)
            w_out_b = w_out.astype(BF16)
            ws = [w_out_b[:fw], w_out_b[fw:]]
            xl_new = _outproj_ln([fm_l, na_l], ws, xl, m_l[2], g0, b0, seq, alpha)
            if ctx_out:
                na_c = _ctx_attn(q_c, k_c, v_c, batch, ctx_len)
                fm_c = _fourier_dense(f_c, batch, ctx_len, fw)
                hc_new = _outproj_ln([fm_c, na_c], ws, hc, m_c[2], g0, b0, ctx_len, alpha)
        else:
            w_in, w_out = odd_w_in[i], odd_w_out[i]
            wq, wk, wv = w_in[:, :qw], w_in[:, qw:qw + kvw], w_in[:, qw + kvw:]
            w_aug = jnp.concatenate([wq, wk, wv, _rotate_half_columns(wq, GQA_Q_HEADS),
                                     _rotate_half_columns(wk, GQA_KV_HEADS)], axis=1).astype(BF16)
            plan = (("heads", 0, qw, qscale, qw + 2 * kvw),
                    ("heads", qw, kvw, 1.0, 2 * qw + 2 * kvw),
                    ("heads", qw + kvw, kvw, 1.0, None))
            q_l, k_l, v_l = _proj(xl, m_l[0], m_l[1], w_aug, plan, seq, rope=(cos_t, sin_t))
            plan_c = (("heads", 0, kvw, 1.0, None), ("heads", kvw, kvw, 1.0, None))
            if ctx_out:
                raise NotImplementedError("an odd layer must be the last layer (no context output path)")
            k_c, v_c = _proj(hc, m_c[0], m_c[1], w_in[:, qw:].astype(BF16), plan_c, ctx_len)
            y_l = _gqa(q_l, k_l, v_l, k_c, v_c, gqa_sink[i], batch, seq, ctx_len)
            xl_new = _outproj_ln([y_l], [w_out.astype(BF16)], xl, m_l[2], g0, b0, seq, alpha)

        tables = _peer_tables(peer_w_q[layer], peer_sub_keys[layer], peer_u[layer], peer_v[layer])
        xl = _peer_layer(xl_new, m_l[3], m_l[4], m_l[5], g1, b1, tables, seq, alpha)
        if ctx_out:
            hc = _peer_layer(hc_new, m_c[3], m_c[4], m_c[5], g1, b1, tables, ctx_len, alpha)

    return xl.reshape(batch, seq, d)
```

```python
import functools
import math

import numpy as np
import jax
import jax.numpy as jnp
from jax import lax
from jax.experimental import pallas as pl
from jax.experimental.pallas import tpu as pltpu

F32 = jnp.float32
BF16 = jnp.bfloat16

HEAD_DIM = 64
GRID_W = 64
FNET_GROUPS = 8
NA_HEADS = 8
NA_KR = 8
NA_KC = 16
NA_ROWS_PER_BLOCK = 8
NA_WIN_ROWS = 16
GQA_Q_HEADS = 16
GQA_KV_HEADS = 4
GQA_WINDOW = 128
GQA_BLOCK = 128
ROPE_THETA = 10000.0
PEER_HEADS = 8
PEER_NKEYS = 128
PEER_TOPK = 16
LN_EPS = 1e-6
NEG = -1e30

PEER_COLS = 256
LANES = 128
VMEM_LIMIT = 56 * 1024 * 1024


def _cparams(*sem):
    return pltpu.CompilerParams(dimension_semantics=sem, vmem_limit_bytes=VMEM_LIMIT)


def _ln(x):
    mu = jnp.mean(x, axis=-1, keepdims=True)
    xc = x - mu
    var = jnp.mean(xc * xc, axis=-1, keepdims=True)
    return xc * lax.rsqrt(var + LN_EPS)


def _dot(a, b):
    return jnp.dot(a, b, preferred_element_type=F32)


def _dot_nt(a, b):
    return lax.dot_general(a, b, (((1,), (1,)), ((), ())), preferred_element_type=F32)


def _ada_body(c_ref, w_ref, b_ref, o_ref):
    c = c_ref[...]
    o_ref[0] = _dot(c * jax.nn.sigmoid(c), w_ref[0]) + b_ref[0]


def _ada(cond, ada_w, ada_b):
    depth, d, n = ada_w.shape
    tn = 1536
    return pl.pallas_call(
        _ada_body,
        grid=(depth, n // tn),
        in_specs=[
            pl.BlockSpec((8, d), lambda l, j: (0, 0)),
            pl.BlockSpec((1, d, tn), lambda l, j: (l, 0, j)),
            pl.BlockSpec((1, 1, tn), lambda l, j: (l, 0, j)),
        ],
        out_specs=pl.BlockSpec((1, 8, tn), lambda l, j: (l, 0, j)),
        out_shape=jax.ShapeDtypeStruct((depth, 8, n), F32),
        compiler_params=_cparams("arbitrary", "arbitrary"),
        name="ada_modulation",
    )(cond, ada_w, ada_b.reshape(depth, 1, n))


def _matmul_f32_body(a_ref, b_ref, o_ref):
    o_ref[...] = jnp.dot(a_ref[...], b_ref[...], preferred_element_type=F32, precision=lax.Precision.HIGHEST)


def _matmul_f32(a, b):
    m, k = a.shape
    n = b.shape[1]
    tm = min(256, m)
    return pl.pallas_call(
        _matmul_f32_body,
        grid=(m // tm,),
        in_specs=[pl.BlockSpec((tm, k), lambda i: (i, 0)), pl.BlockSpec((k, n), lambda i: (0, 0))],
        out_specs=pl.BlockSpec((tm, n), lambda i: (i, 0)),
        out_shape=jax.ShapeDtypeStruct((m, n), F32),
        compiler_params=_cparams("arbitrary"),
        name="small_matmul_f32",
    )(a, b)


def _proj_body(*refs, plan, use_rope):
    x_ref, sh_ref, sc_ref, w_ref = refs[:4]
    rest = refs[4:]
    if use_rope:
        cos_ref, sin_ref = rest[:2]
        rest = rest[2:]
    h = _ln(x_ref[...]) * (1.0 + sc_ref[0]) + sh_ref[0]
    acc = _dot(h.astype(BF16), w_ref[...])
    for o_ref, (kind, start, width, scale, rot_start) in zip(rest, plan):
        y = acc[:, start:start + width]
        if rot_start is not None:
            reps = width // LANES
            cos = jnp.tile(cos_ref[...], (1, reps))
            sin = jnp.tile(sin_ref[...], (1, reps))
            y = y * cos + acc[:, rot_start:rot_start + width] * sin
        if scale != 1.0:
            y = y * scale
        if kind == "nat":
            o_ref[...] = y.astype(o_ref.dtype)
        else:
            for hh in range(width // HEAD_DIM):
                o_ref[hh] = y[:, hh * HEAD_DIM:(hh + 1) * HEAD_DIM].astype(o_ref.dtype)


def _proj(x, shift, scale, w, plan, tokens_per_batch, rope=None, tm=512):
    n, d = x.shape
    tm = min(tm, tokens_per_batch)
    per = tokens_per_batch // tm
    in_specs = [
        pl.BlockSpec((tm, d), lambda i: (i, 0)),
        pl.BlockSpec((1, 1, d), lambda i: (i // per, 0, 0)),
        pl.BlockSpec((1, 1, d), lambda i: (i // per, 0, 0)),
        pl.BlockSpec(w.shape, lambda i: (0, 0)),
    ]
    args = [x, shift, scale, w]
    if rope is not None:
        in_specs += [pl.BlockSpec((tm, LANES), lambda i: (i % per, 0))] * 2
        args += list(rope)
    out_specs, out_shape = [], []
    for kind, start, width, sc, rot in plan:
        if kind == "nat":
            out_specs.append(pl.BlockSpec((tm, width), lambda i: (i, 0)))
            out_shape.append(jax.ShapeDtypeStruct((n, width), F32))
        else:
            nh = width // HEAD_DIM
            out_specs.append(pl.BlockSpec((nh, tm, HEAD_DIM), lambda i: (0, i, 0)))
            out_shape.append(jax.ShapeDtypeStruct((nh, n, HEAD_DIM), BF16))
    return pl.pallas_call(
        functools.partial(_proj_body, plan=plan, use_rope=rope is not None),
        grid=(n // tm,),
        in_specs=in_specs,
        out_specs=out_specs,
        out_shape=out_shape,
        compiler_params=_cparams("arbitrary"),
        name="modln_proj",
    )(*args)


def _outproj_body(*refs, n_in, alpha):
    ys, ws = refs[:n_in], refs[n_in:2 * n_in]
    x_ref, gate_ref, g_ref, b_ref, o_ref = refs[2 * n_in:]
    acc = None
    for y_ref, w_ref in zip(ys, ws):
        t = _dot(y_ref[...].astype(BF16), w_ref[...])
        acc = t if acc is None else acc + t
    z = alpha * x_ref[...] + gate_ref[0] * acc
    o_ref[...] = _ln(z) * g_ref[...] + b_ref[...]


def _outproj_ln(ys, ws, x, gate, g, b, tokens_per_batch, alpha, tm=512):
    n, d = x.shape
    tm = min(tm, tokens_per_batch)
    per = tokens_per_batch // tm
    in_specs = [pl.BlockSpec((tm, y.shape[1]), lambda i: (i, 0)) for y in ys]
    in_specs += [pl.BlockSpec(w.shape, lambda i: (0, 0)) for w in ws]
    in_specs += [
        pl.BlockSpec((tm, d), lambda i: (i, 0)),
        pl.BlockSpec((1, 1, d), lambda i: (i // per, 0, 0)),
        pl.BlockSpec((1, d), lambda i: (0, 0)),
        pl.BlockSpec((1, d), lambda i: (0, 0)),
    ]
    return pl.pallas_call(
        functools.partial(_outproj_body, n_in=len(ys), alpha=alpha),
        grid=(n // tm,),
        in_specs=in_specs,
        out_specs=pl.BlockSpec((tm, d), lambda i: (i, 0)),
        out_shape=jax.ShapeDtypeStruct((n, d), F32),
        compiler_params=_cparams("arbitrary"),
        name="outproj_residual_ln",
    )(*ys, *ws, x, gate, g.reshape(1, d), b.reshape(1, d))


def _dft_tables(n):
    idx = np.arange(n)
    ang = 2.0 * np.pi * ((idx[:, None] * idx[None, :]) % n) / n
    return np.cos(ang), np.sin(ang)


def _fourier_rows_body(x_ref, cs_ref, tc_ref, ts_ref, o_ref, *, cb, width):
    nr = cs_ref.shape[1]
    pq = _dot(cs_ref[...], x_ref[0])
    for j in range(cb):
        a0 = j * 2 * width
        pa, pb = pq[:nr, a0:a0 + width], pq[:nr, a0 + width:a0 + 2 * width]
        qa, qb = pq[nr:, a0:a0 + width], pq[nr:, a0 + width:a0 + 2 * width]
        yr = pa + qb
        yi = pb - qa
        tc, ts = tc_ref[j], ts_ref[j]
        o_ref[0, j, :, :width] = yr * tc + yi * ts
        o_ref[0, j, :, width:] = yi * tc - yr * ts


def _fourier_cols_body(y_ref, cs_ref, o_ref, *, kb, width, norm):
    nc = cs_ref.shape[1]
    pq = _dot(cs_ref[...], y_ref[0])
    for j in range(kb):
        a0 = j * 2 * width
        z = pq[:nc, a0:a0 + width] + pq[nc:, a0 + width:a0 + 2 * width]
        o_ref[0, :, j * width:(j + 1) * width] = z * norm


def _fourier_latent(f, batch, rows, width):
    cols = GRID_W
    seq = rows * cols
    c_r, s_r = _dft_tables(rows)
    c_c, s_c = _dft_tables(cols)
    k1 = np.arange(rows)[None, :]
    cc = np.arange(cols)[:, None]
    tw = 2.0 * np.pi * ((cc * k1) % seq) / seq
    cs_r = jnp.asarray(np.concatenate([c_r, s_r], 0), F32)
    cs_c = jnp.asarray(np.concatenate([c_c, s_c], 0), F32)
    tc = jnp.asarray(np.cos(tw)[:, :, None], F32)
    ts = jnp.asarray(np.sin(tw)[:, :, None], F32)
    cb = 4
    kb = 8
    lane_w = 2 * width
    y = pl.pallas_call(
        functools.partial(_fourier_rows_body, cb=cb, width=width),
        grid=(batch, cols // cb),
        in_specs=[
            pl.BlockSpec((1, rows, cb * lane_w), lambda b, j: (b, 0, j)),
            pl.BlockSpec((2 * rows, rows), lambda b, j: (0, 0)),
            pl.BlockSpec((cb, rows, 1), lambda b, j: (j, 0, 0)),
            pl.BlockSpec((cb, rows, 1), lambda b, j: (j, 0, 0)),
        ],
        out_specs=pl.BlockSpec((1, cb, rows, lane_w), lambda b, j: (b, j, 0, 0)),
        out_shape=jax.ShapeDtypeStruct((batch, cols, rows, lane_w), F32),
        compiler_params=_cparams("arbitrary", "arbitrary"),
        name="fourier_rows",
    )(f.reshape(batch, rows, cols * lane_w), cs_r, tc, ts)
    z = pl.pallas_call(
        functools.partial(_fourier_cols_body, kb=kb, width=width, norm=float((seq * HEAD_DIM) ** -0.5)),
        grid=(batch, rows // kb),
        in_specs=[
            pl.BlockSpec((1, cols, kb * lane_w), lambda b, j: (b, 0, j)),
            pl.BlockSpec((2 * cols, cols), lambda b, j: (0, 0)),
        ],
        out_specs=pl.BlockSpec((1, cols, kb * width), lambda b, j: (b, 0, j)),
        out_shape=jax.ShapeDtypeStruct((batch, cols, rows * width), F32),
        compiler_params=_cparams("arbitrary", "arbitrary"),
        name="fourier_cols",
    )(y.reshape(batch, cols, rows * lane_w), cs_c)
    return z.reshape(batch * seq, width)


def _fourier_dense_body(x_ref, c_ref, s_ref, o_ref, *, width, norm):
    x = x_ref[...]
    o_ref[...] = (_dot(c_ref[...], x[:, :width]) + _dot(s_ref[...], x[:, width:])) * norm


def _fourier_dense(f, batch, length, width):
    c, s = _dft_tables(length)
    return pl.pallas_call(
        functools.partial(_fourier_dense_body, width=width, norm=float((length * HEAD_DIM) ** -0.5)),
        grid=(batch,),
        in_specs=[
            pl.BlockSpec((length, 2 * width), lambda b: (b, 0)),
            pl.BlockSpec((length, length), lambda b: (0, 0)),
            pl.BlockSpec((length, length), lambda b: (0, 0)),
        ],
        out_specs=pl.BlockSpec((length, width), lambda b: (b, 0)),
        out_shape=jax.ShapeDtypeStruct((batch * length, width), F32),
        compiler_params=_cparams("arbitrary"),
        name="fourier_dense",
    )(f, jnp.asarray(c, F32), jnp.asarray(s, F32))


def _natten_tables(rows, rpb):
    w = GRID_W
    qb, kw = NA_ROWS_PER_BLOCK, NA_WIN_ROWS
    nh, ndr, ndc = rpb.shape
    tq, tk = np.divmod(np.arange(w * w), w)
    dc_i = np.clip(tk - tq, -(NA_KC - 1), NA_KC - 1) + (NA_KC - 1)
    onehot = np.zeros((LANES, w * w), np.float32)
    onehot[dc_i, np.arange(w * w)] = 1.0
    c_start = np.clip(tq - NA_KC // 2, 0, w - NA_KC)
    col_ok = ((tk >= c_start) & (tk < c_start + NA_KC)).reshape(w, w)
    rp = jnp.zeros((LANES, LANES), F32).at[:nh * ndr, :ndc].set(rpb.reshape(nh * ndr, ndc).astype(F32))
    tiles = _matmul_f32(rp, jnp.asarray(onehot))[:nh * ndr].reshape(nh, ndr, w, w)
    tiles = jnp.where(jnp.asarray(col_ok), tiles, NEG)
    dr = np.arange(kw)[None, :] - np.arange(qb)[:, None] - (kw - qb) // 2
    dr_i = np.clip(dr + (NA_KR - 1), 0, ndr - 1)
    col_bias = jnp.transpose(tiles[:, dr_i], (0, 1, 3, 2, 4)).reshape(nh, qb * w, kw * w)
    qi = np.repeat(np.arange(qb), w)
    km = np.repeat(np.arange(kw), w)
    nblk = rows // qb
    masks = []
    for j in range(nblk):
        qr = j * qb + qi
        kr = j * qb - (kw - qb) // 2 + km
        r_start = np.clip(qr - NA_KR // 2, 0, rows - NA_KR)
        ok = (kr[None, :] >= r_start[:, None]) & (kr[None, :] < r_start[:, None] + NA_KR)
        masks.append(np.where(ok, 0.0, NEG).astype(np.float32))
    kinds = [(j > 0) + (j == nblk - 1) for j in range(nblk)]
    table = np.zeros((3,) + masks[0].shape, np.float32)
    for j, kd in enumerate(kinds):
        table[kd] = masks[j]
    for j, kd in enumerate(kinds):
        assert np.array_equal(table[kd], masks[j])
    return col_bias, jnp.asarray(table)


def _natten_body(q_ref, k0_ref, k1_ref, k2_ref, k3_ref, v0_ref, v1_ref, v2_ref, v3_ref,
                 kc_ref, vc_ref, cb_ref, rm_ref, o_ref):
    k_refs = (k0_ref, k1_ref, k2_ref, k3_ref)
    v_refs = (v0_ref, v1_ref, v2_ref, v3_ref)
    outs = []
    for hh in range(q_ref.shape[0]):
        q = q_ref[hh]
        s = jnp.concatenate([_dot_nt(q, kr[hh]) for kr in k_refs], axis=1)
        s = s + cb_ref[hh] + rm_ref[0]
        sc = _dot_nt(q, kc_ref[hh])
        m = jnp.maximum(jnp.max(s, axis=-1, keepdims=True), jnp.max(sc, axis=-1, keepdims=True))
        p = jnp.exp(s - m)
        pc = jnp.exp(sc - m)
        l = jnp.sum(p, axis=-1, keepdims=True) + jnp.sum(pc, axis=-1, keepdims=True)
        kb = k0_ref.shape[1]
        o = _dot(pc.astype(BF16), vc_ref[hh])
        for c, vr in enumerate(v_refs):
            o = o + _dot(p[:, c * kb:(c + 1) * kb].astype(BF16), vr[hh])
        outs.append(o / l)
    o_ref[...] = jnp.concatenate(outs, axis=1).astype(o_ref.dtype)


def _natten(q, k, v, kc, vc, col_bias, row_mask, batch, rows, ctx_len):
    nh, n, dh = q.shape
    hp = 2
    qt = NA_ROWS_PER_BLOCK * GRID_W
    kt = NA_WIN_ROWS * GRID_W // 4
    nblk = rows // NA_ROWS_PER_BLOCK
    kblocks = rows * GRID_W // kt

    def kv_spec(c):
        return pl.BlockSpec(
            (hp, kt, dh),
            lambda b, j, h: (h, b * kblocks + jnp.clip(2 * j - 1 + c, 0, kblocks - 1), 0))

    in_specs = [pl.BlockSpec((hp, qt, dh), lambda b, j, h: (h, b * nblk + j, 0))]
    in_specs += [kv_spec(c) for c in range(4)] * 2
    in_specs += [
        pl.BlockSpec((hp, ctx_len, dh), lambda b, j, h: (h, b, 0)),
        pl.BlockSpec((hp, ctx_len, dh), lambda b, j, h: (h, b, 0)),
        pl.BlockSpec((hp, qt, 4 * kt), lambda b, j, h: (h, 0, 0)),
        pl.BlockSpec((1, qt, 4 * kt), lambda b, j, h: ((j > 0).astype(jnp.int32) + (j == nblk - 1).astype(jnp.int32), 0, 0)),
    ]
    return pl.pallas_call(
        _natten_body,
        grid=(batch, nblk, nh // hp),
        in_specs=in_specs,
        out_specs=pl.BlockSpec((qt, hp * dh), lambda b, j, h: (b * nblk + j, h)),
        out_shape=jax.ShapeDtypeStruct((n, nh * dh), BF16),
        compiler_params=_cparams("arbitrary", "arbitrary", "arbitrary"),
        name="neighbourhood_attention",
    )(q, k, k, k, k, v, v, v, v, kc, vc, col_bias, row_mask)


def _ctx_attn_body(q_ref, k_ref, v_ref, o_ref):
    outs = []
    for hh in range(q_ref.shape[0]):
        s = _dot_nt(q_ref[hh], k_ref[hh])
        m = jnp.max(s, axis=-1, keepdims=True)
        p = jnp.exp(s - m)
        l = jnp.sum(p, axis=-1, keepdims=True)
        outs.append(_dot(p.astype(BF16), v_ref[hh]) / l)
    o_ref[...] = jnp.concatenate(outs, axis=1).astype(o_ref.dtype)


def _ctx_attn(q, k, v, batch, ctx_len):
    nh, n, dh = q.shape
    hp = 2
    spec = pl.BlockSpec((hp, ctx_len, dh), lambda b, h: (h, b, 0))
    return pl.pallas_call(
        _ctx_attn_body,
        grid=(batch, nh // hp),
        in_specs=[spec, spec, spec],
        out_specs=pl.BlockSpec((ctx_len, hp * dh), lambda b, h: (b, h)),
        out_shape=jax.ShapeDtypeStruct((n, nh * dh), BF16),
        compiler_params=_cparams("arbitrary", "arbitrary"),
        name="context_attention",
    )(q, k, v)


def _gqa_body(sink_ref, q_ref, kp_ref, kc_ref, kn_ref, vp_ref, vc_ref, vn_ref, kx_ref, vx_ref, o_ref, *, nb):
    g, blk, dh = q_ref.shape
    n = pl.program_id(1)
    kvh = pl.program_id(2)
    q = q_ref[...].reshape(g * blk, dh)
    s = jnp.concatenate([_dot_nt(q, r[0]) for r in (kp_ref, kc_ref, kn_ref)], axis=1)
    qi = lax.broadcasted_iota(jnp.int32, s.shape, 0) % blk
    kp = lax.broadcasted_iota(jnp.int32, s.shape, 1) - blk
    ok = (jnp.abs(qi - kp) <= GQA_WINDOW) & ((kp >= 0) | (n > 0)) & ((kp < blk) | (n < nb - 1))
    s = jnp.where(ok, s, NEG)
    sx = _dot_nt(q, kx_ref[0])
    grp = lax.broadcasted_iota(jnp.int32, (g * blk, 1), 0) // blk
    sk = jnp.zeros((g * blk, 1), F32)
    for gi in range(g):
        sk = jnp.where(grp == gi, sink_ref[kvh * g + gi], sk)
    m = jnp.maximum(jnp.maximum(jnp.max(s, axis=-1, keepdims=True), jnp.max(sx, axis=-1, keepdims=True)), sk)
    p = jnp.exp(s - m)
    px = jnp.exp(sx - m)
    l = jnp.sum(p, axis=-1, keepdims=True) + jnp.sum(px, axis=-1, keepdims=True) + jnp.exp(sk - m)
    o = _dot(px.astype(BF16), vx_ref[0])
    for c, vr in enumerate((vp_ref, vc_ref, vn_ref)):
        o = o + _dot(p[:, c * blk:(c + 1) * blk].astype(BF16), vr[0])
    o = o / l
    o_ref[...] = jnp.concatenate([o[gi * blk:(gi + 1) * blk] for gi in range(g)], axis=1).astype(o_ref.dtype)


def _gqa(q, k, v, kx, vx, sink, batch, seq, ctx_len):
    hq, n, dh = q.shape
    hkv = k.shape[0]
    g = hq // hkv
    blk = GQA_BLOCK
    nb = seq // blk

    def kv_spec(c):
        return pl.BlockSpec((1, blk, dh), lambda b, i, h, s: (h, b * nb + jnp.clip(i - 1 + c, 0, nb - 1), 0))

    x_spec = pl.BlockSpec((1, ctx_len, dh), lambda b, i, h, s: (h, b, 0))
    grid_spec = pltpu.PrefetchScalarGridSpec(
        num_scalar_prefetch=1,
        grid=(batch, nb, hkv),
        in_specs=[pl.BlockSpec((g, blk, dh), lambda b, i, h, s: (h, b * nb + i, 0))]
        + [kv_spec(c) for c in range(3)] * 2 + [x_spec, x_spec],
        out_specs=pl.BlockSpec((blk, g * dh), lambda b, i, h, s: (b * nb + i, h)),
    )
    return pl.pallas_call(
        functools.partial(_gqa_body, nb=nb),
        grid_spec=grid_spec,
        out_shape=jax.ShapeDtypeStruct((n, hq * dh), BF16),
        compiler_params=_cparams("arbitrary", "arbitrary", "arbitrary"),
        name="window_gqa",
    )(sink, q, k, k, k, v, v, v, kx, vx)


def _top_values(s, k):
    tops = []
    for _ in range(k):
        m = jnp.max(s, axis=0, keepdims=True)
        tops.append(m)
        s = jnp.where(s == m, -jnp.inf, s)
    return tops


def _peer_route_body(x_ref, sh_ref, sc_ref, wq_ref, keys_ref, xm_ref, kap_ref, e1_ref, p2_ref):
    xm_ref, kap_ref, e1_ref, p2_ref = (r.at[0] for r in (xm_ref, kap_ref, e1_ref, p2_ref))
    nheads = p2_ref.shape[0]
    nk = keys_ref.shape[1]
    h = _ln(x_ref[...]) * (1.0 + sc_ref[0]) + sh_ref[0]
    ht = h.T.astype(BF16)
    xm_ref[...] = ht
    qt = _dot(wq_ref[...], ht).astype(BF16)
    qd = keys_ref.shape[2]
    nt = PEER_TOPK + 1
    pad = -nt % 8
    row = lax.broadcasted_iota(jnp.int32, (8, 1), 0)
    for hd in range(nheads):
        halves = []
        for half in range(2):
            hp = hd * 2 + half
            s = _dot(keys_ref[hp], qt[hp * qd:(hp + 1) * qd, :])
            halves.append((s, _top_values(s, nt)))
        (s1, t1), (s2, t2) = halves
        t2a = jnp.concatenate(t2 + [jnp.full_like(t2[0], -jnp.inf)] * pad, axis=0)
        slab_rows = [nt + pad] + [8] * (nt - 1)
        slabs = [t1[0] + t2a]
        for a in range(1, nt):
            slabs.append(jnp.where(row < nt // (a + 1), t1[a] + t2a[:8], -jnp.inf))
        cand = jnp.concatenate(slabs, axis=0)
        ctop = _top_values(cand, nt)
        tau = 0.5 * (ctop[PEER_TOPK - 1] + ctop[PEER_TOPK])
        cmax = t1[0] + t2[0]
        def rounded(v):
            return v.astype(BF16).astype(F32)

        p2_top = rounded(jnp.exp(t2a - t2[0]))
        picked = jnp.concatenate(
            [p2_top[:rows] >= rounded(jnp.exp(tau - t1[a] - t2[0])) for a, rows in enumerate(slab_rows)], axis=0)
        z = jnp.sum(jnp.where(picked, jnp.exp(cand - cmax), 0.0), axis=0, keepdims=True)
        kap_ref[hd] = jnp.exp(tau - s1 - t2[0])
        e1_ref[hd] = jnp.exp(s1 - t1[0]) * (0.5 / z)
        p2_ref[hd] = jnp.exp(s2 - t2[0]).astype(BF16)


def _peer_route(x, shift, scale, wq_t, keys, tokens_per_batch):
    n, d = x.shape
    tt = PEER_COLS
    per = tokens_per_batch // tt
    nh = keys.shape[0] // 2
    nk = keys.shape[1]
    big = pl.BlockSpec((1, nh, nk, tt), lambda i: (i, 0, 0, 0))
    big_shape = jax.ShapeDtypeStruct((n // tt, nh, nk, tt), F32)
    return pl.pallas_call(
        _peer_route_body,
        grid=(n // tt,),
        in_specs=[
            pl.BlockSpec((tt, d), lambda i: (i, 0)),
            pl.BlockSpec((1, 1, d), lambda i: (i // per, 0, 0)),
            pl.BlockSpec((1, 1, d), lambda i: (i // per, 0, 0)),
            pl.BlockSpec(wq_t.shape, lambda i: (0, 0)),
            pl.BlockSpec(keys.shape, lambda i: (0, 0, 0)),
        ],
        out_specs=[pl.BlockSpec((1, d, tt), lambda i: (i, 0, 0)), big, big, big],
        out_shape=[jax.ShapeDtypeStruct((n // tt, d, tt), BF16), big_shape, big_shape,
                   jax.ShapeDtypeStruct(big_shape.shape, BF16)],
        compiler_params=_cparams("arbitrary"),
        name="peer_route",
    )(x, shift, scale, wq_t, keys)


PEER_UNIT_ROWS = 4


def _peer_expert_body(xm_ref, u_ref, vt_ref, kap_ref, e1_ref, p2_ref,
                      x_ref, gate_ref, g_ref, b_ref, o_ref, acc_ref, a_ref, w_ref, *, alpha):
    e = pl.program_id(1)
    ncb, nheads, n1, cw = kap_ref.shape
    nk = p2_ref.shape[2]
    ur = PEER_UNIT_ROWS
    ue = ur * nk
    nrp = n1 // ur
    n_units = ncb * nrp

    @pl.when(e == 0)
    def _():
        acc_ref[...] = jnp.zeros_like(acc_ref)

    def first_matmul(i, slot):
        rows = pl.ds(pl.multiple_of((i % nrp) * ue, ue), ue)
        a_ref[slot] = _dot(u_ref[rows, :], xm_ref[i // nrp])

    def second_matmul(i, slot):
        acc_ref[i // nrp] += _dot(vt_ref[i % nrp], w_ref[slot])

    def gate_and_activate(i, slot):
        c = i // nrp
        for r in range(ur):
            row = pl.ds((i % nrp) * ur + r, 1)
            gsum = None
            for hd in range(nheads):
                p2 = p2_ref[c, hd]
                keep = p2 >= kap_ref[c, hd, row, :].astype(BF16)
                term = jnp.where(keep, p2 * e1_ref[c, hd, row, :].astype(BF16), jnp.zeros_like(p2))
                gsum = term if gsum is None else gsum + term
            a = a_ref[slot, r * nk:(r + 1) * nk, :]
            act = (a * (1.0 + lax.erf(a * (2.0 ** -0.5)))).astype(BF16)
            w_ref[slot, r * nk:(r + 1) * nk, :] = gsum * act

    def steady(j, carry):
        i = 2 * j + 1
        first_matmul(i + 1, 0)
        gate_and_activate(i, 1)
        second_matmul(i - 1, 0)
        first_matmul(i + 2, 1)
        gate_and_activate(i + 1, 0)
        second_matmul(i, 1)
        return carry

    assert n_units % 2 == 0
    first_matmul(0, 0)
    first_matmul(1, 1)
    gate_and_activate(0, 0)
    lax.fori_loop(0, n_units // 2 - 1, steady, 0)
    gate_and_activate(n_units - 1, 1)
    second_matmul(n_units - 2, 0)
    second_matmul(n_units - 1, 1)

    @pl.when(e == pl.num_programs(1) - 1)
    def _():
        for c in range(ncb):
            f = acc_ref[c].T
            z = alpha * x_ref[c * cw:(c + 1) * cw, :] + gate_ref[0] * f
            o_ref[c * cw:(c + 1) * cw, :] = _ln(z) * g_ref[...] + b_ref[...]


def _peer_experts(xm_t, u, v_t, kap, e1, p2, x, gate, g, b, tokens_per_batch, alpha, tt=512, et=2048):
    n, d = x.shape
    tt = min(tt, tokens_per_batch)
    per = tokens_per_batch // tt
    _, nh, nk, cw = p2.shape
    ncb = tt // cw
    n1 = et // nk
    ue = PEER_UNIT_ROWS * nk
    sel = pl.BlockSpec((ncb, nh, n1, cw), lambda i, e: (i, 0, e, 0))
    full = pl.BlockSpec((ncb, nh, nk, cw), lambda i, e: (i, 0, 0, 0))
    return pl.pallas_call(
        functools.partial(_peer_expert_body, alpha=alpha),
        grid=(n // tt, u.shape[0] // et),
        in_specs=[
            pl.BlockSpec((ncb, d, cw), lambda i, e: (i, 0, 0)),
            pl.BlockSpec((et, d), lambda i, e: (e, 0)),
            pl.BlockSpec((et // ue, d, ue), lambda i, e: (e, 0, 0)),
            sel, sel, full,
            pl.BlockSpec((tt, d), lambda i, e: (i, 0)),
            pl.BlockSpec((1, 1, d), lambda i, e: (i // per, 0, 0)),
            pl.BlockSpec((1, d), lambda i, e: (0, 0)),
            pl.BlockSpec((1, d), lambda i, e: (0, 0)),
        ],
        out_specs=pl.BlockSpec((tt, d), lambda i, e: (i, 0)),
        out_shape=jax.ShapeDtypeStruct((n, d), F32),
        scratch_shapes=[
            pltpu.VMEM((ncb, d, cw), F32),
            pltpu.VMEM((2, ue, cw), F32),
            pltpu.VMEM((2, ue, cw), BF16),
        ],
        compiler_params=_cparams("arbitrary", "arbitrary"),
        name="peer_experts",
    )(xm_t, u, v_t, kap, e1, p2, x, gate, g.reshape(1, d), b.reshape(1, d))


def _peer_layer(x, shift, scale, gate, g, b, tables, tokens_per_batch, alpha):
    wq_t, keys, u, v_t = tables
    xm_t, kap, e1, p2 = _peer_route(x, shift, scale, wq_t, keys, tokens_per_batch)
    return _peer_experts(xm_t, u, v_t, kap, e1, p2, x, gate, g, b, tokens_per_batch, alpha)


def _rope_tables(seq):
    t = jnp.arange(seq)
    row = (t // GRID_W).astype(F32)
    col = (t % GRID_W).astype(F32)
    n_freq = HEAD_DIM // 4
    inv_freq = ROPE_THETA ** (-jnp.arange(n_freq, dtype=F32) / n_freq)
    ang = jnp.concatenate([row[:, None] * inv_freq, col[:, None] * inv_freq], -1)
    cos, sin = jnp.cos(ang), jnp.sin(ang)
    reps = LANES // (HEAD_DIM // 2)
    return jnp.tile(cos, (1, reps)), jnp.tile(sin, (1, reps))


def _rotate_half_columns(w, n_heads):
    d = w.shape[0]
    wh = w.reshape(d, n_heads, 2, HEAD_DIM // 2)
    return jnp.concatenate([-wh[:, :, 1], wh[:, :, 0]], axis=-1).reshape(d, n_heads * HEAD_DIM)


def _peer_tables(w_q, sub_keys, u, v):
    nh, _, nk, qd = sub_keys.shape
    ue = PEER_UNIT_ROWS * nk
    v_chunks = jnp.transpose(v.astype(BF16).reshape(v.shape[0] // ue, ue, v.shape[1]), (0, 2, 1))
    return (w_q.T.astype(BF16), sub_keys.reshape(nh * 2, nk, qd).astype(BF16), u.astype(BF16), v_chunks)


def kernel(x, c, ctx, c_ctx, ada_w, ada_b, post_ln_g, post_ln_b, even_w_in, even_w_out, na_rpb,
           odd_w_in, odd_w_out, gqa_sink, peer_w_q, peer_sub_keys, peer_u, peer_v):
    batch, seq, d = x.shape
    ctx_len = ctx.shape[1]
    depth = ada_w.shape[0]
    rows = seq // GRID_W
    alpha = float((2 * depth) ** 0.25)
    fw = FNET_GROUPS * HEAD_DIM
    nw = NA_HEADS * HEAD_DIM
    qw = GQA_Q_HEADS * HEAD_DIM
    kvw = GQA_KV_HEADS * HEAD_DIM
    qscale = HEAD_DIM ** -0.5

    cond = jnp.zeros((8, d), F32).at[:batch].set(c).at[batch].set(c_ctx)
    mods = _ada(cond, ada_w, ada_b)

    xl = x.reshape(batch * seq, d)
    hc = ctx.reshape(batch * ctx_len, d)
    cos_t, sin_t = _rope_tables(seq)

    cg, sg = _dft_tables(HEAD_DIM)
    eye = np.eye(FNET_GROUPS)
    chan = jnp.asarray(np.concatenate([np.kron(eye, cg), -np.kron(eye, sg)], axis=1), F32)

    for layer in range(depth):
        ctx_out = layer < depth - 1
        i = layer // 2
        m_l = [m.reshape(batch, 1, d) for m in jnp.split(mods[layer, :batch], 6, axis=-1)]
        m_c = [jnp.broadcast_to(m.reshape(1, 1, d), (batch, 1, d)) for m in jnp.split(mods[layer, batch], 6, axis=-1)]
        g0, b0 = post_ln_g[layer, 0], post_ln_b[layer, 0]
        g1, b1 = post_ln_g[layer, 1], post_ln_b[layer, 1]

        if layer % 2 == 0:
            w_in, w_out = even_w_in[i], even_w_out[i]
            w_f = _matmul_f32(w_in[:, :fw], chan)
            w_aug = jnp.concatenate([w_f, w_in[:, fw:]], axis=1).astype(BF16)
            plan = (("nat", 0, 2 * fw, 1.0, None),
                    ("heads", 2 * fw, nw, qscale, None),
                    ("heads", 2 * fw + nw, nw, 1.0, None),
                    ("heads", 2 * fw + 2 * nw, nw, 1.0, None))
            f_l, q_l, k_l, v_l = _proj(xl, m_l[0], m_l[1], w_aug, plan, seq)
            f_c, q_c, k_c, v_c = _proj(hc, m_c[0], m_c[1], w_aug, plan, ctx_len)
            col_bias, row_mask = _natten_tables(rows, na_rpb[i])
            na_l = _natten(q_l, k_l, v_l, k_c, v_c, col_bias, row_mask, batch, rows, ctx_len)
            fm_l = _fourier_latent(f_l, batch, rows, fw)
            w_out_b = w_out.astype(BF16)
            ws = [w_out_b[:fw], w_out_b[fw:]]
            xl_new = _outproj_ln([fm_l, na_l], ws, xl, m_l[2], g0, b0, seq, alpha)
            if ctx_out:
                na_c = _ctx_attn(q_c, k_c, v_c, batch, ctx_len)
                fm_c = _fourier_dense(f_c, batch, ctx_len, fw)
                hc_new = _outproj_ln([fm_c, na_c], ws, hc, m_c[2], g0, b0, ctx_len, alpha)
        else:
            w_in, w_out = odd_w_in[i], odd_w_out[i]
            wq, wk, wv = w_in[:, :qw], w_in[:, qw:qw + kvw], w_in[:, qw + kvw:]
            w_aug = jnp.concatenate([wq, wk, wv, _rotate_half_columns(wq, GQA_Q_HEADS),
                                     _rotate_half_columns(wk, GQA_KV_HEADS)], axis=1).astype(BF16)
            plan = (("heads", 0, qw, qscale, qw + 2 * kvw),
                    ("heads", qw, kvw, 1.0, 2 * qw + 2 * kvw),
                    ("heads", qw + kvw, kvw, 1.0, None))
            q_l, k_l, v_l = _proj(xl, m_l[0], m_l[1], w_aug, plan, seq, rope=(cos_t, sin_t))
            plan_c = (("heads", 0, kvw, 1.0, None), ("heads", kvw, kvw, 1.0, None))
            if ctx_out:
                raise NotImplementedError("an odd layer must be the last layer (no context output path)")
            k_c, v_c = _proj(hc, m_c[0], m_c[1], w_in[:, qw:].astype(BF16), plan_c, ctx_len)
            y_l = _gqa(q_l, k_l, v_l, k_c, v_c, gqa_sink[i], batch, seq, ctx_len)
            xl_new = _outproj_ln([y_l], [w_out.astype(BF16)], xl, m_l[2], g0, b0, seq, alpha)

        tables = _peer_tables(peer_w_q[layer], peer_sub_keys[layer], peer_u[layer], peer_v[layer])
        xl = _peer_layer(xl_new, m_l[3], m_l[4], m_l[5], g1, b1, tables, seq, alpha)
        if ctx_out:
            hc = _peer_layer(hc_new, m_c[3], m_c[4], m_c[5], g1, b1, tables, ctx_len, alpha)

    return xl.reshape(batch, seq, d)
```

```python
import functools
import math

import numpy as np
import jax
import jax.numpy as jnp
from jax import lax
from jax.experimental import pallas as pl
from jax.experimental.pallas import tpu as pltpu

F32 = jnp.float32
BF16 = jnp.bfloat16

HEAD_DIM = 64
GRID_W = 64
FNET_GROUPS = 8
NA_HEADS = 8
NA_KR = 8
NA_KC = 16
NA_ROWS_PER_BLOCK = 8
NA_WIN_ROWS = 16
GQA_Q_HEADS = 16
GQA_KV_HEADS = 4
GQA_WINDOW = 128
GQA_BLOCK = 128
ROPE_THETA = 10000.0
PEER_HEADS = 8
PEER_NKEYS = 128
PEER_TOPK = 16
LN_EPS = 1e-6
NEG = -1e30

PEER_COLS = 256
LANES = 128
VMEM_LIMIT = 56 * 1024 * 1024


def _cparams(*sem):
    return pltpu.CompilerParams(dimension_semantics=sem, vmem_limit_bytes=VMEM_LIMIT)


def _ln(x):
    mu = jnp.mean(x, axis=-1, keepdims=True)
    xc = x - mu
    var = jnp.mean(xc * xc, axis=-1, keepdims=True)
    return xc * lax.rsqrt(var + LN_EPS)


def _dot(a, b):
    return jnp.dot(a, b, preferred_element_type=F32)


def _dot_nt(a, b):
    return lax.dot_general(a, b, (((1,), (1,)), ((), ())), preferred_element_type=F32)


def _ada_body(c_ref, w_ref, b_ref, o_ref):
    c = c_ref[...]
    o_ref[0] = _dot(c * jax.nn.sigmoid(c), w_ref[0]) + b_ref[0]


def _ada(cond, ada_w, ada_b):
    depth, d, n = ada_w.shape
    tn = 1536
    return pl.pallas_call(
        _ada_body,
        grid=(depth, n // tn),
        in_specs=[
            pl.BlockSpec((8, d), lambda l, j: (0, 0)),
            pl.BlockSpec((1, d, tn), lambda l, j: (l, 0, j)),
            pl.BlockSpec((1, 1, tn), lambda l, j: (l, 0, j)),
        ],
        out_specs=pl.BlockSpec((1, 8, tn), lambda l, j: (l, 0, j)),
        out_shape=jax.ShapeDtypeStruct((depth, 8, n), F32),
        compiler_params=_cparams("arbitrary", "arbitrary"),
        name="ada_modulation",
    )(cond, ada_w, ada_b.reshape(depth, 1, n))


def _matmul_f32_body(a_ref, b_ref, o_ref):
    o_ref[...] = jnp.dot(a_ref[...], b_ref[...], preferred_element_type=F32, precision=lax.Precision.HIGHEST)


def _matmul_f32(a, b):
    m, k = a.shape
    n = b.shape[1]
    tm = min(256, m)
    return pl.pallas_call(
        _matmul_f32_body,
        grid=(m // tm,),
        in_specs=[pl.BlockSpec((tm, k), lambda i: (i, 0)), pl.BlockSpec((k, n), lambda i: (0, 0))],
        out_specs=pl.BlockSpec((tm, n), lambda i: (i, 0)),
        out_shape=jax.ShapeDtypeStruct((m, n), F32),
        compiler_params=_cparams("arbitrary"),
        name="small_matmul_f32",
    )(a, b)


def _proj_body(*refs, plan, use_rope):
    x_ref, sh_ref, sc_ref, w_ref = refs[:4]
    rest = refs[4:]
    if use_rope:
        cos_ref, sin_ref = rest[:2]
        rest = rest[2:]
    h = _ln(x_ref[...]) * (1.0 + sc_ref[0]) + sh_ref[0]
    acc = _dot(h.astype(BF16), w_ref[...])
    for o_ref, (kind, start, width, scale, rot_start) in zip(rest, plan):
        y = acc[:, start:start + width]
        if rot_start is not None:
            reps = width // LANES
            cos = jnp.tile(cos_ref[...], (1, reps))
            sin = jnp.tile(sin_ref[...], (1, reps))
            y = y * cos + acc[:, rot_start:rot_start + width] * sin
        if scale != 1.0:
            y = y * scale
        if kind == "nat":
            o_ref[...] = y.astype(o_ref.dtype)
        else:
            for hh in range(width // HEAD_DIM):
                o_ref[hh] = y[:, hh * HEAD_DIM:(hh + 1) * HEAD_DIM].astype(o_ref.dtype)


def _proj(x, shift, scale, w, plan, tokens_per_batch, rope=None, tm=512):
    n, d = x.shape
    tm = min(tm, tokens_per_batch)
    per = tokens_per_batch // tm
    in_specs = [
        pl.BlockSpec((tm, d), lambda i: (i, 0)),
        pl.BlockSpec((1, 1, d), lambda i: (i // per, 0, 0)),
        pl.BlockSpec((1, 1, d), lambda i: (i // per, 0, 0)),
        pl.BlockSpec(w.shape, lambda i: (0, 0)),
    ]
    args = [x, shift, scale, w]
    if rope is not None:
        in_specs += [pl.BlockSpec((tm, LANES), lambda i: (i % per, 0))] * 2
        args += list(rope)
    out_specs, out_shape = [], []
    for kind, start, width, sc, rot in plan:
        if kind == "nat":
            out_specs.append(pl.BlockSpec((tm, width), lambda i: (i, 0)))
            out_shape.append(jax.ShapeDtypeStruct((n, width), F32))
        else:
            nh = width // HEAD_DIM
            out_specs.append(pl.BlockSpec((nh, tm, HEAD_DIM), lambda i: (0, i, 0)))
            out_shape.append(jax.ShapeDtypeStruct((nh, n, HEAD_DIM), BF16))
    return pl.pallas_call(
        functools.partial(_proj_body, plan=plan, use_rope=rope is not None),
        grid=(n // tm,),
        in_specs=in_specs,
        out_specs=out_specs,
        out_shape=out_shape,
        compiler_params=_cparams("arbitrary"),
        name="modln_proj",
    )(*args)


def _outproj_body(*refs, n_in, alpha):
    ys, ws = refs[:n_in], refs[n_in:2 * n_in]
    x_ref, gate_ref, g_ref, b_ref, o_ref = refs[2 * n_in:]
    acc = None
    for y_ref, w_ref in zip(ys, ws):
        t = _dot(y_ref[...].astype(BF16), w_ref[...])
        acc = t if acc is None else acc + t
    z = alpha * x_ref[...] + gate_ref[0] * acc
    o_ref[...] = _ln(z) * g_ref[...] + b_ref[...]


def _outproj_ln(ys, ws, x, gate, g, b, tokens_per_batch, alpha, tm=512):
    n, d = x.shape
    tm = min(tm, tokens_per_batch)
    per = tokens_per_batch // tm
    in_specs = [pl.BlockSpec((tm, y.shape[1]), lambda i: (i, 0)) for y in ys]
    in_specs += [pl.BlockSpec(w.shape, lambda i: (0, 0)) for w in ws]
    in_specs += [
        pl.BlockSpec((tm, d), lambda i: (i, 0)),
        pl.BlockSpec((1, 1, d), lambda i: (i // per, 0, 0)),
        pl.BlockSpec((1, d), lambda i: (0, 0)),
        pl.BlockSpec((1, d), lambda i: (0, 0)),
    ]
    return pl.pallas_call(
        functools.partial(_outproj_body, n_in=len(ys), alpha=alpha),
        grid=(n // tm,),
        in_specs=in_specs,
        out_specs=pl.BlockSpec((tm, d), lambda i: (i, 0)),
        out_shape=jax.ShapeDtypeStruct((n, d), F32),
        compiler_params=_cparams("arbitrary"),
        name="outproj_residual_ln",
    )(*ys, *ws, x, gate, g.reshape(1, d), b.reshape(1, d))


def _dft_tables(n):
    idx = np.arange(n)
    ang = 2.0 * np.pi * ((idx[:, None] * idx[None, :]) % n) / n
    return np.cos(ang), np.sin(ang)


def _fourier_rows_body(x_ref, cs_ref, tc_ref, ts_ref, o_ref, *, cb, width):
    nr = cs_ref.shape[1]
    pq = _dot(cs_ref[...], x_ref[0])
    for j in range(cb):
        a0 = j * 2 * width
        pa, pb = pq[:nr, a0:a0 + width], pq[:nr, a0 + width:a0 + 2 * width]
        qa, qb = pq[nr:, a0:a0 + width], pq[nr:, a0 + width:a0 + 2 * width]
        yr = pa + qb
        yi = pb - qa
        tc, ts = tc_ref[j], ts_ref[j]
        o_ref[0, j, :, :width] = yr * tc + yi * ts
        o_ref[0, j, :, width:] = yi * tc - yr * ts


def _fourier_cols_body(y_ref, cs_ref, o_ref, *, kb, width, norm):
    nc = cs_ref.shape[1]
    pq = _dot(cs_ref[...], y_ref[0])
    for j in range(kb):
        a0 = j * 2 * width
        z = pq[:nc, a0:a0 + width] + pq[nc:, a0 + width:a0 + 2 * width]
        o_ref[0, :, j * width:(j + 1) * width] = z * norm


def _fourier_latent(f, batch, rows, width):
    cols = GRID_W
    seq = rows * cols
    c_r, s_r = _dft_tables(rows)
    c_c, s_c = _dft_tables(cols)
    k1 = np.arange(rows)[None, :]
    cc = np.arange(cols)[:, None]
    tw = 2.0 * np.pi * ((cc * k1) % seq) / seq
    cs_r = jnp.asarray(np.concatenate([c_r, s_r], 0), F32)
    cs_c = jnp.asarray(np.concatenate([c_c, s_c], 0), F32)
    tc = jnp.asarray(np.cos(tw)[:, :, None], F32)
    ts = jnp.asarray(np.sin(tw)[:, :, None], F32)
    cb = 4
    kb = 8
    lane_w = 2 * width
    y = pl.pallas_call(
        functools.partial(_fourier_rows_body, cb=cb, width=width),
        grid=(batch, cols // cb),
        in_specs=[
            pl.BlockSpec((1, rows, cb * lane_w), lambda b, j: (b, 0, j)),
            pl.BlockSpec((2 * rows, rows), lambda b, j: (0, 0)),
            pl.BlockSpec((cb, rows, 1), lambda b, j: (j, 0, 0)),
            pl.BlockSpec((cb, rows, 1), lambda b, j: (j, 0, 0)),
        ],
        out_specs=pl.BlockSpec((1, cb, rows, lane_w), lambda b, j: (b, j, 0, 0)),
        out_shape=jax.ShapeDtypeStruct((batch, cols, rows, lane_w), F32),
        compiler_params=_cparams("arbitrary", "arbitrary"),
        name="fourier_rows",
    )(f.reshape(batch, rows, cols * lane_w), cs_r, tc, ts)
    z = pl.pallas_call(
        functools.partial(_fourier_cols_body, kb=kb, width=width, norm=float((seq * HEAD_DIM) ** -0.5)),
        grid=(batch, rows // kb),
        in_specs=[
            pl.BlockSpec((1, cols, kb * lane_w), lambda b, j: (b, 0, j)),
            pl.BlockSpec((2 * cols, cols), lambda b, j: (0, 0)),
        ],
        out_specs=pl.BlockSpec((1, cols, kb * width), lambda b, j: (b, 0, j)),
        out_shape=jax.ShapeDtypeStruct((batch, cols, rows * width), F32),
        compiler_params=_cparams("arbitrary", "arbitrary"),
        name="fourier_cols",
    )(y.reshape(batch, cols, rows * lane_w), cs_c)
    return z.reshape(batch * seq, width)


def _fourier_dense_body(x_ref, c_ref, s_ref, o_ref, *, width, norm):
    x = x_ref[...]
    o_ref[...] = (_dot(c_ref[...], x[:, :width]) + _dot(s_ref[...], x[:, width:])) * norm


def _fourier_dense(f, batch, length, width):
    c, s = _dft_tables(length)
    return pl.pallas_call(
        functools.partial(_fourier_dense_body, width=width, norm=float((length * HEAD_DIM) ** -0.5)),
        grid=(batch,),
        in_specs=[
            pl.BlockSpec((length, 2 * width), lambda b: (b, 0)),
            pl.BlockSpec((length, length), lambda b: (0, 0)),
            pl.BlockSpec((length, length), lambda b: (0, 0)),
        ],
        out_specs=pl.BlockSpec((length, width), lambda b: (b, 0)),
        out_shape=jax.ShapeDtypeStruct((batch * length, width), F32),
        compiler_params=_cparams("arbitrary"),
        name="fourier_dense",
    )(f, jnp.asarray(c, F32), jnp.asarray(s, F32))


def _natten_tables(rows, rpb):
    w = GRID_W
    qb, kw = NA_ROWS_PER_BLOCK, NA_WIN_ROWS
    nh, ndr, ndc = rpb.shape
    tq, tk = np.divmod(np.arange(w * w), w)
    dc_i = np.clip(tk - tq, -(NA_KC - 1), NA_KC - 1) + (NA_KC - 1)
    onehot = np.zeros((LANES, w * w), np.float32)
    onehot[dc_i, np.arange(w * w)] = 1.0
    c_start = np.clip(tq - NA_KC // 2, 0, w - NA_KC)
    col_ok = ((tk >= c_start) & (tk < c_start + NA_KC)).reshape(w, w)
    rp = jnp.zeros((LANES, LANES), F32).at[:nh * ndr, :ndc].set(rpb.reshape(nh * ndr, ndc).astype(F32))
    tiles = _matmul_f32(rp, jnp.asarray(onehot))[:nh * ndr].reshape(nh, ndr, w, w)
    tiles = jnp.where(jnp.asarray(col_ok), tiles, NEG)
    dr = np.arange(kw)[None, :] - np.arange(qb)[:, None] - (kw - qb) // 2
    dr_i = np.clip(dr + (NA_KR - 1), 0, ndr - 1)
    col_bias = jnp.transpose(tiles[:, dr_i], (0, 1, 3, 2, 4)).reshape(nh, qb * w, kw * w)
    qi = np.repeat(np.arange(qb), w)
    km = np.repeat(np.arange(kw), w)
    nblk = rows // qb
    masks = []
    for j in range(nblk):
        qr = j * qb + qi
        kr = j * qb - (kw - qb) // 2 + km
        r_start = np.clip(qr - NA_KR // 2, 0, rows - NA_KR)
        ok = (kr[None, :] >= r_start[:, None]) & (kr[None, :] < r_start[:, None] + NA_KR)
        masks.append(np.where(ok, 0.0, NEG).astype(np.float32))
    kinds = [(j > 0) + (j == nblk - 1) for j in range(nblk)]
    table = np.zeros((3,) + masks[0].shape, np.float32)
    for j, kd in enumerate(kinds):
        table[kd] = masks[j]
    for j, kd in enumerate(kinds):
        assert np.array_equal(table[kd], masks[j])
    half, quarter = qb * w // 2, kw * w // 4
    assert (table[:, :half, 3 * quarter:] == NEG).all() and (table[:, half:, :quarter] == NEG).all()
    return col_bias, jnp.asarray(table)


def _natten_body(q_ref, k0_ref, k1_ref, k2_ref, k3_ref, v0_ref, v1_ref, v2_ref, v3_ref,
                 kc_ref, vc_ref, cb_ref, rm_ref, o_ref):
    k_refs = (k0_ref, k1_ref, k2_ref, k3_ref)
    v_refs = (v0_ref, v1_ref, v2_ref, v3_ref)
    qt, kb = q_ref.shape[1], k0_ref.shape[1]
    hq = qt // 2
    outs = []
    for hh in range(q_ref.shape[0]):
        halves = []
        for part in range(2):
            rows = slice(part * hq, (part + 1) * hq)
            cols = slice(part * kb, (part + 3) * kb)
            q = q_ref[hh, rows, :]
            s = jnp.concatenate([_dot_nt(q, kr[hh]) for kr in k_refs[part:part + 3]], axis=1)
            s = s + cb_ref[hh, rows, cols] + rm_ref[0, rows, cols]
            sc = _dot_nt(q, kc_ref[hh])
            m = jnp.maximum(jnp.max(s, axis=-1, keepdims=True), jnp.max(sc, axis=-1, keepdims=True))
            p = jnp.exp(s - m)
            pc = jnp.exp(sc - m)
            l = jnp.sum(p, axis=-1, keepdims=True) + jnp.sum(pc, axis=-1, keepdims=True)
            o = _dot(pc.astype(BF16), vc_ref[hh])
            for c, vr in enumerate(v_refs[part:part + 3]):
                o = o + _dot(p[:, c * kb:(c + 1) * kb].astype(BF16), vr[hh])
            halves.append(o / l)
        outs.append(jnp.concatenate(halves, axis=0))
    o_ref[...] = jnp.concatenate(outs, axis=1).astype(o_ref.dtype)


def _natten(q, k, v, kc, vc, col_bias, row_mask, batch, rows, ctx_len):
    nh, n, dh = q.shape
    hp = 4
    qt = NA_ROWS_PER_BLOCK * GRID_W
    kt = NA_WIN_ROWS * GRID_W // 4
    nblk = rows // NA_ROWS_PER_BLOCK
    kblocks = rows * GRID_W // kt

    def kv_spec(c):
        return pl.BlockSpec(
            (hp, kt, dh),
            lambda b, j, h: (h, b * kblocks + jnp.clip(2 * j - 1 + c, 0, kblocks - 1), 0))

    in_specs = [pl.BlockSpec((hp, qt, dh), lambda b, j, h: (h, b * nblk + j, 0))]
    in_specs += [kv_spec(c) for c in range(4)] * 2
    in_specs += [
        pl.BlockSpec((hp, ctx_len, dh), lambda b, j, h: (h, b, 0)),
        pl.BlockSpec((hp, ctx_len, dh), lambda b, j, h: (h, b, 0)),
        pl.BlockSpec((hp, qt, 4 * kt), lambda b, j, h: (h, 0, 0)),
        pl.BlockSpec((1, qt, 4 * kt), lambda b, j, h: ((j > 0).astype(jnp.int32) + (j == nblk - 1).astype(jnp.int32), 0, 0)),
    ]
    return pl.pallas_call(
        _natten_body,
        grid=(batch, nblk, nh // hp),
        in_specs=in_specs,
        out_specs=pl.BlockSpec((qt, hp * dh), lambda b, j, h: (b * nblk + j, h)),
        out_shape=jax.ShapeDtypeStruct((n, nh * dh), BF16),
        compiler_params=_cparams("arbitrary", "arbitrary", "arbitrary"),
        name="neighbourhood_attention",
    )(q, k, k, k, k, v, v, v, v, kc, vc, col_bias, row_mask)


def _ctx_attn_body(q_ref, k_ref, v_ref, o_ref):
    outs = []
    for hh in range(q_ref.shape[0]):
        s = _dot_nt(q_ref[hh], k_ref[hh])
        m = jnp.max(s, axis=-1, keepdims=True)
        p = jnp.exp(s - m)
        l = jnp.sum(p, axis=-1, keepdims=True)
        outs.append(_dot(p.astype(BF16), v_ref[hh]) / l)
    o_ref[...] = jnp.concatenate(outs, axis=1).astype(o_ref.dtype)


def _ctx_attn(q, k, v, batch, ctx_len):
    nh, n, dh = q.shape
    hp = 2
    spec = pl.BlockSpec((hp, ctx_len, dh), lambda b, h: (h, b, 0))
    return pl.pallas_call(
        _ctx_attn_body,
        grid=(batch, nh // hp),
        in_specs=[spec, spec, spec],
        out_specs=pl.BlockSpec((ctx_len, hp * dh), lambda b, h: (b, h)),
        out_shape=jax.ShapeDtypeStruct((n, nh * dh), BF16),
        compiler_params=_cparams("arbitrary", "arbitrary"),
        name="context_attention",
    )(q, k, v)


def _gqa_body(sink_ref, q_ref, kp_ref, kc_ref, kn_ref, vp_ref, vc_ref, vn_ref, kx_ref, vx_ref, o_ref, *, nb):
    hkv, blk, dh = kc_ref.shape
    g = q_ref.shape[0] // hkv
    n = pl.program_id(1)
    shape = (g * blk, 3 * blk)
    qi = lax.broadcasted_iota(jnp.int32, shape, 0) % blk
    kp = lax.broadcasted_iota(jnp.int32, shape, 1) - blk
    ok = (jnp.abs(qi - kp) <= GQA_WINDOW) & ((kp >= 0) | (n > 0)) & ((kp < blk) | (n < nb - 1))
    grp = lax.broadcasted_iota(jnp.int32, (g * blk, 1), 0) // blk
    for kvh in range(hkv):
        q = q_ref[kvh * g:(kvh + 1) * g].reshape(g * blk, dh)
        s = jnp.concatenate([_dot_nt(q, r[kvh]) for r in (kp_ref, kc_ref, kn_ref)], axis=1)
        s = jnp.where(ok, s, NEG)
        sx = _dot_nt(q, kx_ref[kvh])
        sk = jnp.zeros((g * blk, 1), F32)
        for gi in range(g):
            sk = jnp.where(grp == gi, sink_ref[kvh * g + gi], sk)
        m = jnp.maximum(jnp.maximum(jnp.max(s, axis=-1, keepdims=True), jnp.max(sx, axis=-1, keepdims=True)), sk)
        p = jnp.exp(s - m)
        px = jnp.exp(sx - m)
        l = jnp.sum(p, axis=-1, keepdims=True) + jnp.sum(px, axis=-1, keepdims=True) + jnp.exp(sk - m)
        o = _dot(px.astype(BF16), vx_ref[kvh])
        for c, vr in enumerate((vp_ref, vc_ref, vn_ref)):
            o = o + _dot(p[:, c * blk:(c + 1) * blk].astype(BF16), vr[kvh])
        o = o / l
        o_ref[:, kvh * g * dh:(kvh + 1) * g * dh] = jnp.concatenate(
            [o[gi * blk:(gi + 1) * blk] for gi in range(g)], axis=1).astype(o_ref.dtype)


def _gqa(q, k, v, kx, vx, sink, batch, seq, ctx_len):
    hq, n, dh = q.shape
    hkv = k.shape[0]
    g = hq // hkv
    blk = GQA_BLOCK
    nb = seq // blk

    def kv_spec(c):
        return pl.BlockSpec((hkv, blk, dh), lambda b, i, s: (0, b * nb + jnp.clip(i - 1 + c, 0, nb - 1), 0))

    x_spec = pl.BlockSpec((hkv, ctx_len, dh), lambda b, i, s: (0, b, 0))
    grid_spec = pltpu.PrefetchScalarGridSpec(
        num_scalar_prefetch=1,
        grid=(batch, nb),
        in_specs=[pl.BlockSpec((hq, blk, dh), lambda b, i, s: (0, b * nb + i, 0))]
        + [kv_spec(c) for c in range(3)] * 2 + [x_spec, x_spec],
        out_specs=pl.BlockSpec((blk, hq * dh), lambda b, i, s: (b * nb + i, 0)),
    )
    return pl.pallas_call(
        functools.partial(_gqa_body, nb=nb),
        grid_spec=grid_spec,
        out_shape=jax.ShapeDtypeStruct((n, hq * dh), BF16),
        compiler_params=_cparams("arbitrary", "arbitrary"),
        name="window_gqa",
    )(sink, q, k, k, k, v, v, v, kx, vx)


SUBLANES = 8


def _oddeven_merge(lo, hi, r):
    step = r * 2
    if step < hi - lo:
        yield from _oddeven_merge(lo, hi, step)
        yield from _oddeven_merge(lo + r, hi, step)
        yield from [(i, i + r) for i in range(lo + r, hi - r, step)]
    else:
        yield (lo, lo + r)


def _oddeven_sort(lo, hi):
    if hi - lo >= 1:
        mid = lo + (hi - lo) // 2
        yield from _oddeven_sort(lo, mid)
        yield from _oddeven_sort(mid + 1, hi)
        yield from _oddeven_merge(lo, hi, 1)


def _exchange(t, i, j):
    t[i], t[j] = jnp.maximum(t[i], t[j]), jnp.minimum(t[i], t[j])


def _sort_tiles(tiles):
    t = list(tiles)
    for i, j in _oddeven_sort(0, PEER_TOPK - 1):
        if j < len(t):
            _exchange(t, i, j)
    return t


def _top_tiles(sorted_tiles):
    w = list(sorted_tiles)
    n = len(w)
    shift = SUBLANES // 2
    while shift:
        other = [pltpu.roll(x, shift, 0) for x in w]
        w = [jnp.maximum(w[k], other[n - 1 - k]) for k in range(n)]
        d = n // 2
        while d:
            for k in range(n):
                if not k & d:
                    _exchange(w, k, k + d)
            d //= 2
        shift //= 2
    return w


def _next_below(tiles, bound):
    m = None
    for t in tiles:
        v = jnp.where(t < bound, t, -jnp.inf)
        m = v if m is None else jnp.maximum(m, v)
    return jnp.max(m, axis=0, keepdims=True)


def _stack_sublanes(vals, sub):
    out = vals[0]
    for j in range(1, len(vals)):
        out = jnp.where(sub == j, vals[j], out)
    return out


def _bf16_rounded(v):
    return v.astype(BF16).astype(F32)


def _peer_route_body(x_ref, sh_ref, sc_ref, wq_ref, keys_ref, xm_ref, kap_ref, e1_ref, p2_ref):
    xm_ref, kap_ref, e1_ref, p2_ref = (r.at[0] for r in (xm_ref, kap_ref, e1_ref, p2_ref))
    nheads = p2_ref.shape[0]
    nk = keys_ref.shape[1]
    h = _ln(x_ref[...]) * (1.0 + sc_ref[0]) + sh_ref[0]
    ht = h.T.astype(BF16)
    xm_ref[...] = ht
    qt = _dot(wq_ref[...], ht).astype(BF16)
    qd = keys_ref.shape[2]
    k16 = PEER_TOPK
    sub = lax.broadcasted_iota(jnp.int32, (SUBLANES, 1), 0)
    for hd in range(nheads):
        halves = []
        for half in range(2):
            hp = hd * 2 + half
            s = _dot(keys_ref[hp], qt[hp * qd:(hp + 1) * qd, :])
            tiles = [s[SUBLANES * k:SUBLANES * (k + 1)] for k in range(nk // SUBLANES)]
            top = _top_tiles(_sort_tiles(tiles))
            halves.append((s, top, _next_below(tiles, top[-1])))
        (s1, t1, t1_17), (s2, t2, t2_17) = halves
        m1, m2 = t1[0][:1], t2[0][:1]
        cmax = m1 + m2
        t2_lo, t2_hi = _stack_sublanes(t2[:SUBLANES], sub), _stack_sublanes(t2[SUBLANES:], sub)
        t1_hi = _stack_sublanes(t1[SUBLANES:], sub)
        pairs = [(t1[0], t2_lo), (t1[0], t2_hi)]
        pairs += [(t1[a], jnp.where(sub < (k16 + 1) // (a + 1), t2_lo, -jnp.inf)) for a in range(1, SUBLANES)]
        pairs += [(t1_hi, t2[0])]
        pairs += [(jnp.where(sub == 0, t1[0], t1_17), jnp.where(sub == 0, t2_17, jnp.where(sub == 1, t2[0], -jnp.inf)))]
        cands = [a + b for a, b in pairs]
        ctop = _top_tiles(_sort_tiles(cands) + [jnp.full_like(cands[0], -jnp.inf)] * (k16 - len(cands)))
        c16 = ctop[-1][:1]
        tau = 0.5 * (c16 + _next_below(cands, c16))
        z = None
        for (a, b), c in zip(pairs, cands):
            picked = _bf16_rounded(jnp.exp(b - m2)) >= _bf16_rounded(jnp.exp(tau - a - m2))
            zc = jnp.where(picked, jnp.exp(c - cmax), 0.0)
            z = zc if z is None else z + zc
        z = jnp.sum(z, axis=0, keepdims=True)
        kap_ref[hd] = jnp.exp(tau - s1 - m2)
        e1_ref[hd] = jnp.exp(s1 - m1) * (0.5 / z)
        p2_ref[hd] = jnp.exp(s2 - m2).astype(BF16)


def _peer_route(x, shift, scale, wq_t, keys, tokens_per_batch):
    n, d = x.shape
    tt = PEER_COLS
    per = tokens_per_batch // tt
    nh = keys.shape[0] // 2
    nk = keys.shape[1]
    big = pl.BlockSpec((1, nh, nk, tt), lambda i: (i, 0, 0, 0))
    big_shape = jax.ShapeDtypeStruct((n // tt, nh, nk, tt), F32)
    return pl.pallas_call(
        _peer_route_body,
        grid=(n // tt,),
        in_specs=[
            pl.BlockSpec((tt, d), lambda i: (i, 0)),
            pl.BlockSpec((1, 1, d), lambda i: (i // per, 0, 0)),
            pl.BlockSpec((1, 1, d), lambda i: (i // per, 0, 0)),
            pl.BlockSpec(wq_t.shape, lambda i: (0, 0)),
            pl.BlockSpec(keys.shape, lambda i: (0, 0, 0)),
        ],
        out_specs=[pl.BlockSpec((1, d, tt), lambda i: (i, 0, 0)), big, big, big],
        out_shape=[jax.ShapeDtypeStruct((n // tt, d, tt), BF16), big_shape, big_shape,
                   jax.ShapeDtypeStruct(big_shape.shape, BF16)],
        compiler_params=_cparams("arbitrary"),
        name="peer_route",
    )(x, shift, scale, wq_t, keys)


PEER_UNIT_ROWS = 4


def _peer_expert_body(xm_ref, u_ref, vt_ref, kap_ref, e1_ref, p2_ref,
                      x_ref, gate_ref, g_ref, b_ref, o_ref, acc_ref, a_ref, w_ref, *, alpha):
    e = pl.program_id(1)
    ncb, nheads, n1, cw = kap_ref.shape
    nk = p2_ref.shape[2]
    ur = PEER_UNIT_ROWS
    ue = ur * nk
    nrp = n1 // ur
    n_units = ncb * nrp

    @pl.when(e == 0)
    def _():
        acc_ref[...] = jnp.zeros_like(acc_ref)

    def first_matmul(i, slot):
        rows = pl.ds(pl.multiple_of((i % nrp) * ue, ue), ue)
        a_ref[slot] = _dot(u_ref[rows, :], xm_ref[i // nrp])

    def second_matmul(i, slot):
        acc_ref[i // nrp] += _dot(vt_ref[i % nrp], w_ref[slot])

    def gate_and_activate(i, slot):
        c = i // nrp
        for r in range(ur):
            row = pl.ds((i % nrp) * ur + r, 1)
            gsum = None
            for hd in range(nheads):
                p2 = p2_ref[c, hd]
                keep = p2 >= kap_ref[c, hd, row, :].astype(BF16)
                term = jnp.where(keep, p2 * e1_ref[c, hd, row, :].astype(BF16), jnp.zeros_like(p2))
                gsum = term if gsum is None else gsum + term
            a = a_ref[slot, r * nk:(r + 1) * nk, :]
            act = (a * (1.0 + lax.erf(a * (2.0 ** -0.5)))).astype(BF16)
            w_ref[slot, r * nk:(r + 1) * nk, :] = gsum * act

    def steady(j, carry):
        i = 2 * j + 1
        first_matmul(i + 1, 0)
        gate_and_activate(i, 1)
        second_matmul(i - 1, 0)
        first_matmul(i + 2, 1)
        gate_and_activate(i + 1, 0)
        second_matmul(i, 1)
        return carry

    assert n_units % 2 == 0
    first_matmul(0, 0)
    first_matmul(1, 1)
    gate_and_activate(0, 0)
    lax.fori_loop(0, n_units // 2 - 1, steady, 0)
    gate_and_activate(n_units - 1, 1)
    second_matmul(n_units - 2, 0)
    second_matmul(n_units - 1, 1)

    @pl.when(e == pl.num_programs(1) - 1)
    def _():
        for c in range(ncb):
            f = acc_ref[c].T
            z = alpha * x_ref[c * cw:(c + 1) * cw, :] + gate_ref[0] * f
            o_ref[c * cw:(c + 1) * cw, :] = _ln(z) * g_ref[...] + b_ref[...]


def _peer_experts(xm_t, u, v_t, kap, e1, p2, x, gate, g, b, tokens_per_batch, alpha, tt=512, et=2048):
    n, d = x.shape
    tt = min(tt, tokens_per_batch)
    per = tokens_per_batch // tt
    _, nh, nk, cw = p2.shape
    ncb = tt // cw
    n1 = et // nk
    ue = PEER_UNIT_ROWS * nk
    sel = pl.BlockSpec((ncb, nh, n1, cw), lambda i, e: (i, 0, e, 0))
    full = pl.BlockSpec((ncb, nh, nk, cw), lambda i, e: (i, 0, 0, 0))
    return pl.pallas_call(
        functools.partial(_peer_expert_body, alpha=alpha),
        grid=(n // tt, u.shape[0] // et),
        in_specs=[
            pl.BlockSpec((ncb, d, cw), lambda i, e: (i, 0, 0)),
            pl.BlockSpec((et, d), lambda i, e: (e, 0)),
            pl.BlockSpec((et // ue, d, ue), lambda i, e: (e, 0, 0)),
            sel, sel, full,
            pl.BlockSpec((tt, d), lambda i, e: (i, 0)),
            pl.BlockSpec((1, 1, d), lambda i, e: (i // per, 0, 0)),
            pl.BlockSpec((1, d), lambda i, e: (0, 0)),
            pl.BlockSpec((1, d), lambda i, e: (0, 0)),
        ],
        out_specs=pl.BlockSpec((tt, d), lambda i, e: (i, 0)),
        out_shape=jax.ShapeDtypeStruct((n, d), F32),
        scratch_shapes=[
            pltpu.VMEM((ncb, d, cw), F32),
            pltpu.VMEM((2, ue, cw), F32),
            pltpu.VMEM((2, ue, cw), BF16),
        ],
        compiler_params=_cparams("arbitrary", "arbitrary"),
        name="peer_experts",
    )(xm_t, u, v_t, kap, e1, p2, x, gate, g.reshape(1, d), b.reshape(1, d))


def _peer_layer(x, shift, scale, gate, g, b, tables, tokens_per_batch, alpha):
    wq_t, keys, u, v_t = tables
    xm_t, kap, e1, p2 = _peer_route(x, shift, scale, wq_t, keys, tokens_per_batch)
    return _peer_experts(xm_t, u, v_t, kap, e1, p2, x, gate, g, b, tokens_per_batch, alpha)


def _rope_tables(seq):
    t = jnp.arange(seq)
    row = (t // GRID_W).astype(F32)
    col = (t % GRID_W).astype(F32)
    n_freq = HEAD_DIM // 4
    inv_freq = ROPE_THETA ** (-jnp.arange(n_freq, dtype=F32) / n_freq)
    ang = jnp.concatenate([row[:, None] * inv_freq, col[:, None] * inv_freq], -1)
    cos, sin = jnp.cos(ang), jnp.sin(ang)
    reps = LANES // (HEAD_DIM // 2)
    return jnp.tile(cos, (1, reps)), jnp.tile(sin, (1, reps))


def _rotate_half_columns(w, n_heads):
    d = w.shape[0]
    wh = w.reshape(d, n_heads, 2, HEAD_DIM // 2)
    return jnp.concatenate([-wh[:, :, 1], wh[:, :, 0]], axis=-1).reshape(d, n_heads * HEAD_DIM)


def _peer_tables(w_q, sub_keys, u, v):
    nh, _, nk, qd = sub_keys.shape
    ue = PEER_UNIT_ROWS * nk
    v_chunks = jnp.transpose(v.astype(BF16).reshape(v.shape[0] // ue, ue, v.shape[1]), (0, 2, 1))
    return (w_q.T.astype(BF16), sub_keys.reshape(nh * 2, nk, qd).astype(BF16), u.astype(BF16), v_chunks)


def kernel(x, c, ctx, c_ctx, ada_w, ada_b, post_ln_g, post_ln_b, even_w_in, even_w_out, na_rpb,
           odd_w_in, odd_w_out, gqa_sink, peer_w_q, peer_sub_keys, peer_u, peer_v):
    batch, seq, d = x.shape
    ctx_len = ctx.shape[1]
    depth = ada_w.shape[0]
    rows = seq // GRID_W
    alpha = float((2 * depth) ** 0.25)
    fw = FNET_GROUPS * HEAD_DIM
    nw = NA_HEADS * HEAD_DIM
    qw = GQA_Q_HEADS * HEAD_DIM
    kvw = GQA_KV_HEADS * HEAD_DIM
    qscale = HEAD_DIM ** -0.5

    cond = jnp.zeros((8, d), F32).at[:batch].set(c).at[batch].set(c_ctx)
    mods = _ada(cond, ada_w, ada_b)

    xl = x.reshape(batch * seq, d)
    hc = ctx.reshape(batch * ctx_len, d)
    cos_t, sin_t = _rope_tables(seq)

    cg, sg = _dft_tables(HEAD_DIM)
    eye = np.eye(FNET_GROUPS)
    chan = jnp.asarray(np.concatenate([np.kron(eye, cg), -np.kron(eye, sg)], axis=1), F32)

    for layer in range(depth):
        ctx_out = layer < depth - 1
        i = layer // 2
        m_l = [m.reshape(batch, 1, d) for m in jnp.split(mods[layer, :batch], 6, axis=-1)]
        m_c = [jnp.broadcast_to(m.reshape(1, 1, d), (batch, 1, d)) for m in jnp.split(mods[layer, batch], 6, axis=-1)]
        g0, b0 = post_ln_g[layer, 0], post_ln_b[layer, 0]
        g1, b1 = post_ln_g[layer, 1], post_ln_b[layer, 1]

        if layer % 2 == 0:
            w_in, w_out = even_w_in[i], even_w_out[i]
            w_f = _matmul_f32(w_in[:, :fw], chan)
            w_aug = jnp.concatenate([w_f, w_in[:, fw:]], axis=1).astype(BF16)
            plan = (("nat", 0, 2 * fw, 1.0, None),
                    ("heads", 2 * fw, nw, qscale, None),
                    ("heads", 2 * fw + nw, nw, 1.0, None),
                    ("heads", 2 * fw + 2 * nw, nw, 1.0, None))
            f_l, q_l, k_l, v_l = _proj(xl, m_l[0], m_l[1], w_aug, plan, seq)
            f_c, q_c, k_c, v_c = _proj(hc, m_c[0], m_c[1], w_aug, plan, ctx_len)
            col_bias, row_mask = _natten_tables(rows, na_rpb[i])
            na_l = _natten(q_l, k_l, v_l, k_c, v_c, col_bias, row_mask, batch, rows, ctx_len)
            fm_l = _fourier_latent(f_l, batch, rows, fw)
            w_out_b = w_out.astype(BF16)
            ws = [w_out_b[:fw], w_out_b[fw:]]
            xl_new = _outproj_ln([fm_l, na_l], ws, xl, m_l[2], g0, b0, seq, alpha)
            if ctx_out:
                na_c = _ctx_attn(q_c, k_c, v_c, batch, ctx_len)
                fm_c = _fourier_dense(f_c, batch, ctx_len, fw)
                hc_new = _outproj_ln([fm_c, na_c], ws, hc, m_c[2], g0, b0, ctx_len, alpha)
        else:
            w_in, w_out = odd_w_in[i], odd_w_out[i]
            wq, wk, wv = w_in[:, :qw], w_in[:, qw:qw + kvw], w_in[:, qw + kvw:]
            w_aug = jnp.concatenate([wq, wk, wv, _rotate_half_columns(wq, GQA_Q_HEADS),
                                     _rotate_half_columns(wk, GQA_KV_HEADS)], axis=1).astype(BF16)
            plan = (("heads", 0, qw, qscale, qw + 2 * kvw),
                    ("heads", qw, kvw, 1.0, 2 * qw + 2 * kvw),
                    ("heads", qw + kvw, kvw, 1.0, None))
            q_l, k_l, v_l = _proj(xl, m_l[0], m_l[1], w_aug, plan, seq, rope=(cos_t, sin_t))
            plan_c = (("heads", 0, kvw, 1.0, None), ("heads", kvw, kvw, 1.0, None))
            if ctx_out:
                raise NotImplementedError("an odd layer must be the last layer (no context output path)")
            k_c, v_c = _proj(hc, m_c[0], m_c[1], w_in[:, qw:].astype(BF16), plan_c, ctx_len)
            y_l = _gqa(q_l, k_l, v_l, k_c, v_c, gqa_sink[i], batch, seq, ctx_len)
            xl_new = _outproj_ln([y_l], [w_out.astype(BF16)], xl, m_l[2], g0, b0, seq, alpha)

        tables = _peer_tables(peer_w_q[layer], peer_sub_keys[layer], peer_u[layer], peer_v[layer])
        xl = _peer_layer(xl_new, m_l[3], m_l[4], m_l[5], g1, b1, tables, seq, alpha)
        if ctx_out:
            hc = _peer_layer(hc_new, m_c[3], m_c[4], m_c[5], g1, b1, tables, ctx_len, alpha)

    return xl.reshape(batch, seq, d)
```

```python
import functools
import math

import numpy as np
import jax
import jax.numpy as jnp
from jax import lax
from jax.experimental import pallas as pl
from jax.experimental.pallas import tpu as pltpu

F32 = jnp.float32
BF16 = jnp.bfloat16

HEAD_DIM = 64
GRID_W = 64
FNET_GROUPS = 8
NA_HEADS = 8
NA_KR = 8
NA_KC = 16
NA_ROWS_PER_BLOCK = 8
NA_WIN_ROWS = 16
GQA_Q_HEADS = 16
GQA_KV_HEADS = 4
GQA_WINDOW = 128
GQA_BLOCK = 128
ROPE_THETA = 10000.0
PEER_HEADS = 8
PEER_NKEYS = 128
PEER_TOPK = 16
LN_EPS = 1e-6
NEG = -1e30

PEER_COLS = 256
LANES = 128
VMEM_LIMIT = 56 * 1024 * 1024


def _cparams(*sem):
    return pltpu.CompilerParams(dimension_semantics=sem, vmem_limit_bytes=VMEM_LIMIT)


def _ln(x):
    mu = jnp.mean(x, axis=-1, keepdims=True)
    xc = x - mu
    var = jnp.mean(xc * xc, axis=-1, keepdims=True)
    return xc * lax.rsqrt(var + LN_EPS)


def _dot(a, b):
    return jnp.dot(a, b, preferred_element_type=F32)


def _dot_nt(a, b):
    return lax.dot_general(a, b, (((1,), (1,)), ((), ())), preferred_element_type=F32)


def _ada_body(c_ref, w_ref, b_ref, o_ref):
    c = c_ref[...]
    o_ref[0] = _dot(c * jax.nn.sigmoid(c), w_ref[0]) + b_ref[0]


def _ada(cond, ada_w, ada_b):
    depth, d, n = ada_w.shape
    tn = 1536
    return pl.pallas_call(
        _ada_body,
        grid=(depth, n // tn),
        in_specs=[
            pl.BlockSpec((8, d), lambda l, j: (0, 0)),
            pl.BlockSpec((1, d, tn), lambda l, j: (l, 0, j)),
            pl.BlockSpec((1, 1, tn), lambda l, j: (l, 0, j)),
        ],
        out_specs=pl.BlockSpec((1, 8, tn), lambda l, j: (l, 0, j)),
        out_shape=jax.ShapeDtypeStruct((depth, 8, n), F32),
        compiler_params=_cparams("arbitrary", "arbitrary"),
        name="ada_modulation",
    )(cond, ada_w, ada_b.reshape(depth, 1, n))


def _matmul_f32_body(a_ref, b_ref, o_ref):
    o_ref[...] = jnp.dot(a_ref[...], b_ref[...], preferred_element_type=F32, precision=lax.Precision.HIGHEST)


def _matmul_f32(a, b):
    m, k = a.shape
    n = b.shape[1]
    tm = min(256, m)
    return pl.pallas_call(
        _matmul_f32_body,
        grid=(m // tm,),
        in_specs=[pl.BlockSpec((tm, k), lambda i: (i, 0)), pl.BlockSpec((k, n), lambda i: (0, 0))],
        out_specs=pl.BlockSpec((tm, n), lambda i: (i, 0)),
        out_shape=jax.ShapeDtypeStruct((m, n), F32),
        compiler_params=_cparams("arbitrary"),
        name="small_matmul_f32",
    )(a, b)


def _proj_body(*refs, plan, use_rope):
    x_ref, sh_ref, sc_ref, w_ref = refs[:4]
    rest = refs[4:]
    if use_rope:
        cos_ref, sin_ref = rest[:2]
        rest = rest[2:]
    h = _ln(x_ref[...]) * (1.0 + sc_ref[0]) + sh_ref[0]
    acc = _dot(h.astype(BF16), w_ref[...])
    for o_ref, (kind, start, width, scale, rot_start) in zip(rest, plan):
        y = acc[:, start:start + width]
        if rot_start is not None:
            reps = width // LANES
            cos = jnp.tile(cos_ref[...], (1, reps))
            sin = jnp.tile(sin_ref[...], (1, reps))
            y = y * cos + acc[:, rot_start:rot_start + width] * sin
        if scale != 1.0:
            y = y * scale
        if kind == "nat":
            o_ref[...] = y.astype(o_ref.dtype)
        else:
            for hh in range(width // HEAD_DIM):
                o_ref[hh] = y[:, hh * HEAD_DIM:(hh + 1) * HEAD_DIM].astype(o_ref.dtype)


def _proj(x, shift, scale, w, plan, tokens_per_batch, rope=None, tm=512):
    n, d = x.shape
    tm = min(tm, tokens_per_batch)
    per = tokens_per_batch // tm
    in_specs = [
        pl.BlockSpec((tm, d), lambda i: (i, 0)),
        pl.BlockSpec((1, 1, d), lambda i: (i // per, 0, 0)),
        pl.BlockSpec((1, 1, d), lambda i: (i // per, 0, 0)),
        pl.BlockSpec(w.shape, lambda i: (0, 0)),
    ]
    args = [x, shift, scale, w]
    if rope is not None:
        in_specs += [pl.BlockSpec((tm, LANES), lambda i: (i % per, 0))] * 2
        args += list(rope)
    out_specs, out_shape = [], []
    for kind, start, width, sc, rot in plan:
        if kind == "nat":
            out_specs.append(pl.BlockSpec((tm, width), lambda i: (i, 0)))
            out_shape.append(jax.ShapeDtypeStruct((n, width), F32))
        else:
            nh = width // HEAD_DIM
            out_specs.append(pl.BlockSpec((nh, tm, HEAD_DIM), lambda i: (0, i, 0)))
            out_shape.append(jax.ShapeDtypeStruct((nh, n, HEAD_DIM), BF16))
    return pl.pallas_call(
        functools.partial(_proj_body, plan=plan, use_rope=rope is not None),
        grid=(n // tm,),
        in_specs=in_specs,
        out_specs=out_specs,
        out_shape=out_shape,
        compiler_params=_cparams("arbitrary"),
        name="modln_proj",
    )(*args)


def _outproj_body(*refs, n_in, alpha):
    ys, ws = refs[:n_in], refs[n_in:2 * n_in]
    x_ref, gate_ref, g_ref, b_ref, o_ref = refs[2 * n_in:]
    acc = None
    for y_ref, w_ref in zip(ys, ws):
        t = _dot(y_ref[...].astype(BF16), w_ref[...])
        acc = t if acc is None else acc + t
    z = alpha * x_ref[...] + gate_ref[0] * acc
    o_ref[...] = _ln(z) * g_ref[...] + b_ref[...]


def _outproj_ln(ys, ws, x, gate, g, b, tokens_per_batch, alpha, tm=512):
    n, d = x.shape
    tm = min(tm, tokens_per_batch)
    per = tokens_per_batch // tm
    in_specs = [pl.BlockSpec((tm, y.shape[1]), lambda i: (i, 0)) for y in ys]
    in_specs += [pl.BlockSpec(w.shape, lambda i: (0, 0)) for w in ws]
    in_specs += [
        pl.BlockSpec((tm, d), lambda i: (i, 0)),
        pl.BlockSpec((1, 1, d), lambda i: (i // per, 0, 0)),
        pl.BlockSpec((1, d), lambda i: (0, 0)),
        pl.BlockSpec((1, d), lambda i: (0, 0)),
    ]
    return pl.pallas_call(
        functools.partial(_outproj_body, n_in=len(ys), alpha=alpha),
        grid=(n // tm,),
        in_specs=in_specs,
        out_specs=pl.BlockSpec((tm, d), lambda i: (i, 0)),
        out_shape=jax.ShapeDtypeStruct((n, d), F32),
        compiler_params=_cparams("arbitrary"),
        name="outproj_residual_ln",
    )(*ys, *ws, x, gate, g.reshape(1, d), b.reshape(1, d))


def _dft_tables(n):
    idx = np.arange(n)
    ang = 2.0 * np.pi * ((idx[:, None] * idx[None, :]) % n) / n
    return np.cos(ang), np.sin(ang)


def _fourier_rows_body(x_ref, cs_ref, tc_ref, ts_ref, o_ref, *, cb, width):
    nr = cs_ref.shape[1]
    pq = _dot(cs_ref[...], x_ref[0])
    for j in range(cb):
        a0 = j * 2 * width
        pa, pb = pq[:nr, a0:a0 + width], pq[:nr, a0 + width:a0 + 2 * width]
        qa, qb = pq[nr:, a0:a0 + width], pq[nr:, a0 + width:a0 + 2 * width]
        yr = pa + qb
        yi = pb - qa
        tc, ts = tc_ref[j], ts_ref[j]
        o_ref[0, j, :, :width] = yr * tc + yi * ts
        o_ref[0, j, :, width:] = yi * tc - yr * ts


def _fourier_cols_body(y_ref, cs_ref, o_ref, *, kb, width, norm):
    nc = cs_ref.shape[1]
    pq = _dot(cs_ref[...], y_ref[0])
    for j in range(kb):
        a0 = j * 2 * width
        z = pq[:nc, a0:a0 + width] + pq[nc:, a0 + width:a0 + 2 * width]
        o_ref[0, :, j * width:(j + 1) * width] = z * norm


def _fourier_latent(f, batch, rows, width):
    cols = GRID_W
    seq = rows * cols
    c_r, s_r = _dft_tables(rows)
    c_c, s_c = _dft_tables(cols)
    k1 = np.arange(rows)[None, :]
    cc = np.arange(cols)[:, None]
    tw = 2.0 * np.pi * ((cc * k1) % seq) / seq
    cs_r = jnp.asarray(np.concatenate([c_r, s_r], 0), F32)
    cs_c = jnp.asarray(np.concatenate([c_c, s_c], 0), F32)
    tc = jnp.asarray(np.cos(tw)[:, :, None], F32)
    ts = jnp.asarray(np.sin(tw)[:, :, None], F32)
    cb = 4
    kb = 8
    lane_w = 2 * width
    y = pl.pallas_call(
        functools.partial(_fourier_rows_body, cb=cb, width=width),
        grid=(batch, cols // cb),
        in_specs=[
            pl.BlockSpec((1, rows, cb * lane_w), lambda b, j: (b, 0, j)),
            pl.BlockSpec((2 * rows, rows), lambda b, j: (0, 0)),
            pl.BlockSpec((cb, rows, 1), lambda b, j: (j, 0, 0)),
            pl.BlockSpec((cb, rows, 1), lambda b, j: (j, 0, 0)),
        ],
        out_specs=pl.BlockSpec((1, cb, rows, lane_w), lambda b, j: (b, j, 0, 0)),
        out_shape=jax.ShapeDtypeStruct((batch, cols, rows, lane_w), F32),
        compiler_params=_cparams("arbitrary", "arbitrary"),
        name="fourier_rows",
    )(f.reshape(batch, rows, cols * lane_w), cs_r, tc, ts)
    z = pl.pallas_call(
        functools.partial(_fourier_cols_body, kb=kb, width=width, norm=float((seq * HEAD_DIM) ** -0.5)),
        grid=(batch, rows // kb),
        in_specs=[
            pl.BlockSpec((1, cols, kb * lane_w), lambda b, j: (b, 0, j)),
            pl.BlockSpec((2 * cols, cols), lambda b, j: (0, 0)),
        ],
        out_specs=pl.BlockSpec((1, cols, kb * width), lambda b, j: (b, 0, j)),
        out_shape=jax.ShapeDtypeStruct((batch, cols, rows * width), F32),
        compiler_params=_cparams("arbitrary", "arbitrary"),
        name="fourier_cols",
    )(y.reshape(batch, cols, rows * lane_w), cs_c)
    return z.reshape(batch * seq, width)


def _fourier_dense_body(x_ref, c_ref, s_ref, o_ref, *, width, norm):
    x = x_ref[...]
    o_ref[...] = (_dot(c_ref[...], x[:, :width]) + _dot(s_ref[...], x[:, width:])) * norm


def _fourier_dense(f, batch, length, width):
    c, s = _dft_tables(length)
    return pl.pallas_call(
        functools.partial(_fourier_dense_body, width=width, norm=float((length * HEAD_DIM) ** -0.5)),
        grid=(batch,),
        in_specs=[
            pl.BlockSpec((length, 2 * width), lambda b: (b, 0)),
            pl.BlockSpec((length, length), lambda b: (0, 0)),
            pl.BlockSpec((length, length), lambda b: (0, 0)),
        ],
        out_specs=pl.BlockSpec((length, width), lambda b: (b, 0)),
        out_shape=jax.ShapeDtypeStruct((batch * length, width), F32),
        compiler_params=_cparams("arbitrary"),
        name="fourier_dense",
    )(f, jnp.asarray(c, F32), jnp.asarray(s, F32))


def _natten_tables(rows, rpb):
    w = GRID_W
    qb, kw = NA_ROWS_PER_BLOCK, NA_WIN_ROWS
    nh, ndr, ndc = rpb.shape
    tq, tk = np.divmod(np.arange(w * w), w)
    dc_i = np.clip(tk - tq, -(NA_KC - 1), NA_KC - 1) + (NA_KC - 1)
    onehot = np.zeros((LANES, w * w), np.float32)
    onehot[dc_i, np.arange(w * w)] = 1.0
    c_start = np.clip(tq - NA_KC // 2, 0, w - NA_KC)
    col_ok = ((tk >= c_start) & (tk < c_start + NA_KC)).reshape(w, w)
    rp = jnp.zeros((LANES, LANES), F32).at[:nh * ndr, :ndc].set(rpb.reshape(nh * ndr, ndc).astype(F32))
    tiles = _matmul_f32(rp, jnp.asarray(onehot))[:nh * ndr].reshape(nh, ndr, w, w)
    tiles = jnp.where(jnp.asarray(col_ok), tiles, NEG)
    dr = np.arange(kw)[None, :] - np.arange(qb)[:, None] - (kw - qb) // 2
    dr_i = np.clip(dr + (NA_KR - 1), 0, ndr - 1)
    col_bias = jnp.transpose(tiles[:, dr_i], (0, 1, 3, 2, 4)).reshape(nh, qb * w, kw * w)
    qi = np.repeat(np.arange(qb), w)
    km = np.repeat(np.arange(kw), w)
    nblk = rows // qb
    masks = []
    for j in range(nblk):
        qr = j * qb + qi
        kr = j * qb - (kw - qb) // 2 + km
        r_start = np.clip(qr - NA_KR // 2, 0, rows - NA_KR)
        ok = (kr[None, :] >= r_start[:, None]) & (kr[None, :] < r_start[:, None] + NA_KR)
        masks.append(np.where(ok, 0.0, NEG).astype(np.float32))
    kinds = [(j > 0) + (j == nblk - 1) for j in range(nblk)]
    table = np.zeros((3,) + masks[0].shape, np.float32)
    for j, kd in enumerate(kinds):
        table[kd] = masks[j]
    for j, kd in enumerate(kinds):
        assert np.array_equal(table[kd], masks[j])
    half, quarter = qb * w // 2, kw * w // 4
    assert (table[:, :half, 3 * quarter:] == NEG).all() and (table[:, half:, :quarter] == NEG).all()
    return col_bias, jnp.asarray(table)


def _natten_body(q_ref, k0_ref, k1_ref, k2_ref, k3_ref, v0_ref, v1_ref, v2_ref, v3_ref,
                 kc_ref, vc_ref, cb_ref, rm_ref, o_ref):
    k_refs = (k0_ref, k1_ref, k2_ref, k3_ref)
    v_refs = (v0_ref, v1_ref, v2_ref, v3_ref)
    qt, kb = q_ref.shape[1], k0_ref.shape[1]
    hq = qt // 2
    outs = []
    for hh in range(q_ref.shape[0]):
        halves = []
        for part in range(2):
            rows = slice(part * hq, (part + 1) * hq)
            cols = slice(part * kb, (part + 3) * kb)
            q = q_ref[hh, rows, :]
            s = jnp.concatenate([_dot_nt(q, kr[hh]) for kr in k_refs[part:part + 3]], axis=1)
            s = s + cb_ref[hh, rows, cols] + rm_ref[0, rows, cols]
            sc = _dot_nt(q, kc_ref[hh])
            m = jnp.maximum(jnp.max(s, axis=-1, keepdims=True), jnp.max(sc, axis=-1, keepdims=True))
            p = jnp.exp(s - m)
            pc = jnp.exp(sc - m)
            l = jnp.sum(p, axis=-1, keepdims=True) + jnp.sum(pc, axis=-1, keepdims=True)
            o = _dot(pc.astype(BF16), vc_ref[hh])
            for c, vr in enumerate(v_refs[part:part + 3]):
                o = o + _dot(p[:, c * kb:(c + 1) * kb].astype(BF16), vr[hh])
            halves.append(o / l)
        outs.append(jnp.concatenate(halves, axis=0))
    o_ref[...] = jnp.concatenate(outs, axis=1).astype(o_ref.dtype)


def _natten(q, k, v, kc, vc, col_bias, row_mask, batch, rows, ctx_len):
    nh, n, dh = q.shape
    hp = 4
    qt = NA_ROWS_PER_BLOCK * GRID_W
    kt = NA_WIN_ROWS * GRID_W // 4
    nblk = rows // NA_ROWS_PER_BLOCK
    kblocks = rows * GRID_W // kt

    def kv_spec(c):
        return pl.BlockSpec(
            (hp, kt, dh),
            lambda b, j, h: (h, b * kblocks + jnp.clip(2 * j - 1 + c, 0, kblocks - 1), 0))

    in_specs = [pl.BlockSpec((hp, qt, dh), lambda b, j, h: (h, b * nblk + j, 0))]
    in_specs += [kv_spec(c) for c in range(4)] * 2
    in_specs += [
        pl.BlockSpec((hp, ctx_len, dh), lambda b, j, h: (h, b, 0)),
        pl.BlockSpec((hp, ctx_len, dh), lambda b, j, h: (h, b, 0)),
        pl.BlockSpec((hp, qt, 4 * kt), lambda b, j, h: (h, 0, 0)),
        pl.BlockSpec((1, qt, 4 * kt), lambda b, j, h: ((j > 0).astype(jnp.int32) + (j == nblk - 1).astype(jnp.int32), 0, 0)),
    ]
    return pl.pallas_call(
        _natten_body,
        grid=(batch, nblk, nh // hp),
        in_specs=in_specs,
        out_specs=pl.BlockSpec((qt, hp * dh), lambda b, j, h: (b * nblk + j, h)),
        out_shape=jax.ShapeDtypeStruct((n, nh * dh), BF16),
        compiler_params=_cparams("arbitrary", "arbitrary", "arbitrary"),
        name="neighbourhood_attention",
    )(q, k, k, k, k, v, v, v, v, kc, vc, col_bias, row_mask)


def _ctx_attn_body(q_ref, k_ref, v_ref, o_ref):
    outs = []
    for hh in range(q_ref.shape[0]):
        s = _dot_nt(q_ref[hh], k_ref[hh])
        m = jnp.max(s, axis=-1, keepdims=True)
        p = jnp.exp(s - m)
        l = jnp.sum(p, axis=-1, keepdims=True)
        outs.append(_dot(p.astype(BF16), v_ref[hh]) / l)
    o_ref[...] = jnp.concatenate(outs, axis=1).astype(o_ref.dtype)


def _ctx_attn(q, k, v, batch, ctx_len):
    nh, n, dh = q.shape
    hp = 2
    spec = pl.BlockSpec((hp, ctx_len, dh), lambda b, h: (h, b, 0))
    return pl.pallas_call(
        _ctx_attn_body,
        grid=(batch, nh // hp),
        in_specs=[spec, spec, spec],
        out_specs=pl.BlockSpec((ctx_len, hp * dh), lambda b, h: (b, h)),
        out_shape=jax.ShapeDtypeStruct((n, nh * dh), BF16),
        compiler_params=_cparams("arbitrary", "arbitrary"),
        name="context_attention",
    )(q, k, v)


def _gqa_body(sink_ref, q_ref, kp_ref, kc_ref, kn_ref, vp_ref, vc_ref, vn_ref, kx_ref, vx_ref, o_ref, *, nb):
    hkv, blk, dh = kc_ref.shape
    g = q_ref.shape[0] // hkv
    n = pl.program_id(1)
    shape = (g * blk, 3 * blk)
    qi = lax.broadcasted_iota(jnp.int32, shape, 0) % blk
    kp = lax.broadcasted_iota(jnp.int32, shape, 1) - blk
    ok = (jnp.abs(qi - kp) <= GQA_WINDOW) & ((kp >= 0) | (n > 0)) & ((kp < blk) | (n < nb - 1))
    grp = lax.broadcasted_iota(jnp.int32, (g * blk, 1), 0) // blk
    for kvh in range(hkv):
        q = q_ref[kvh * g:(kvh + 1) * g].reshape(g * blk, dh)
        s = jnp.concatenate([_dot_nt(q, r[kvh]) for r in (kp_ref, kc_ref, kn_ref)], axis=1)
        s = jnp.where(ok, s, NEG)
        sx = _dot_nt(q, kx_ref[kvh])
        sk = jnp.zeros((g * blk, 1), F32)
        for gi in range(g):
            sk = jnp.where(grp == gi, sink_ref[kvh * g + gi], sk)
        m = jnp.maximum(jnp.maximum(jnp.max(s, axis=-1, keepdims=True), jnp.max(sx, axis=-1, keepdims=True)), sk)
        p = jnp.exp(s - m)
        px = jnp.exp(sx - m)
        l = jnp.sum(p, axis=-1, keepdims=True) + jnp.sum(px, axis=-1, keepdims=True) + jnp.exp(sk - m)
        o = _dot(px.astype(BF16), vx_ref[kvh])
        for c, vr in enumerate((vp_ref, vc_ref, vn_ref)):
            o = o + _dot(p[:, c * blk:(c + 1) * blk].astype(BF16), vr[kvh])
        o = o / l
        o_ref[:, kvh * g * dh:(kvh + 1) * g * dh] = jnp.concatenate(
            [o[gi * blk:(gi + 1) * blk] for gi in range(g)], axis=1).astype(o_ref.dtype)


def _gqa(q, k, v, kx, vx, sink, batch, seq, ctx_len):
    hq, n, dh = q.shape
    hkv = k.shape[0]
    g = hq // hkv
    blk = GQA_BLOCK
    nb = seq // blk

    def kv_spec(c):
        return pl.BlockSpec((hkv, blk, dh), lambda b, i, s: (0, b * nb + jnp.clip(i - 1 + c, 0, nb - 1), 0))

    x_spec = pl.BlockSpec((hkv, ctx_len, dh), lambda b, i, s: (0, b, 0))
    grid_spec = pltpu.PrefetchScalarGridSpec(
        num_scalar_prefetch=1,
        grid=(batch, nb),
        in_specs=[pl.BlockSpec((hq, blk, dh), lambda b, i, s: (0, b * nb + i, 0))]
        + [kv_spec(c) for c in range(3)] * 2 + [x_spec, x_spec],
        out_specs=pl.BlockSpec((blk, hq * dh), lambda b, i, s: (b * nb + i, 0)),
    )
    return pl.pallas_call(
        functools.partial(_gqa_body, nb=nb),
        grid_spec=grid_spec,
        out_shape=jax.ShapeDtypeStruct((n, hq * dh), BF16),
        compiler_params=_cparams("arbitrary", "arbitrary"),
        name="window_gqa",
    )(sink, q, k, k, k, v, v, v, kx, vx)


SUBLANES = 8


def _oddeven_merge(lo, hi, r):
    step = r * 2
    if step < hi - lo:
        yield from _oddeven_merge(lo, hi, step)
        yield from _oddeven_merge(lo + r, hi, step)
        yield from [(i, i + r) for i in range(lo + r, hi - r, step)]
    else:
        yield (lo, lo + r)


def _oddeven_sort(lo, hi):
    if hi - lo >= 1:
        mid = lo + (hi - lo) // 2
        yield from _oddeven_sort(lo, mid)
        yield from _oddeven_sort(mid + 1, hi)
        yield from _oddeven_merge(lo, hi, 1)


def _exchange(t, i, j):
    t[i], t[j] = jnp.maximum(t[i], t[j]), jnp.minimum(t[i], t[j])


def _sort_tiles(tiles):
    t = list(tiles)
    for i, j in _oddeven_sort(0, PEER_TOPK - 1):
        if j < len(t):
            _exchange(t, i, j)
    return t


def _top_tiles(sorted_tiles):
    w = list(sorted_tiles)
    n = len(w)
    shift = SUBLANES // 2
    while shift:
        other = [pltpu.roll(x, shift, 0) for x in w]
        w = [jnp.maximum(w[k], other[n - 1 - k]) for k in range(n)]
        d = n // 2
        while d:
            for k in range(n):
                if not k & d:
                    _exchange(w, k, k + d)
            d //= 2
        shift //= 2
    return w


def _next_below(tiles, bound):
    m = None
    for t in tiles:
        v = jnp.where(t < bound, t, -jnp.inf)
        m = v if m is None else jnp.maximum(m, v)
    return jnp.max(m, axis=0, keepdims=True)


def _stack_sublanes(vals, sub):
    out = vals[0]
    for j in range(1, len(vals)):
        out = jnp.where(sub == j, vals[j], out)
    return out


def _bf16_rounded(v):
    return v.astype(BF16).astype(F32)


def _bf16_pair_bits(v):
    hi = pltpu.bitcast(_bf16_rounded(v), jnp.int32)
    return hi | lax.shift_right_logical(hi, 16)


def _bf16_rows_from_pair_bits(bits_row, nrows):
    packed = pltpu.bitcast(jnp.broadcast_to(bits_row, (SUBLANES, bits_row.shape[1])), BF16)
    return jnp.tile(packed, (nrows // packed.shape[0], 1))


def _peer_route_body(x_ref, sh_ref, sc_ref, wq_ref, keys_ref, xm_ref, kap_ref, e1_ref, p2_ref):
    xm_ref, kap_ref, e1_ref, p2_ref = (r.at[0] for r in (xm_ref, kap_ref, e1_ref, p2_ref))
    nheads = p2_ref.shape[0]
    nk = keys_ref.shape[1]
    h = _ln(x_ref[...]) * (1.0 + sc_ref[0]) + sh_ref[0]
    ht = h.T.astype(BF16)
    xm_ref[...] = ht
    qt = _dot(wq_ref[...], ht).astype(BF16)
    qd = keys_ref.shape[2]
    k16 = PEER_TOPK
    sub = lax.broadcasted_iota(jnp.int32, (SUBLANES, 1), 0)
    for hd in range(nheads):
        halves = []
        for half in range(2):
            hp = hd * 2 + half
            s = _dot(keys_ref[hp], qt[hp * qd:(hp + 1) * qd, :])
            tiles = [s[SUBLANES * k:SUBLANES * (k + 1)] for k in range(nk // SUBLANES)]
            top = _top_tiles(_sort_tiles(tiles))
            halves.append((s, top, _next_below(tiles, top[-1])))
        (s1, t1, t1_17), (s2, t2, t2_17) = halves
        m1, m2 = t1[0][:1], t2[0][:1]
        cmax = m1 + m2
        t2_lo, t2_hi = _stack_sublanes(t2[:SUBLANES], sub), _stack_sublanes(t2[SUBLANES:], sub)
        t1_hi = _stack_sublanes(t1[SUBLANES:], sub)
        pairs = [(t1[0], t2_lo), (t1[0], t2_hi)]
        pairs += [(t1[a], jnp.where(sub < (k16 + 1) // (a + 1), t2_lo, -jnp.inf)) for a in range(1, SUBLANES)]
        pairs += [(t1_hi, t2[0])]
        pairs += [(jnp.where(sub == 0, t1[0], t1_17), jnp.where(sub == 0, t2_17, jnp.where(sub == 1, t2[0], -jnp.inf)))]
        cands = [a + b for a, b in pairs]
        ctop = _top_tiles(_sort_tiles(cands) + [jnp.full_like(cands[0], -jnp.inf)] * (k16 - len(cands)))
        c16 = ctop[-1][:1]
        tau = 0.5 * (c16 + _next_below(cands, c16))
        z = None
        for (a, b), c in zip(pairs, cands):
            picked = _bf16_rounded(jnp.exp(b - m2)) >= _bf16_rounded(jnp.exp(tau - a - m2))
            zc = jnp.where(picked, jnp.exp(c - cmax), 0.0)
            z = zc if z is None else z + zc
        z = jnp.sum(z, axis=0, keepdims=True)
        groups = (nk // SUBLANES, SUBLANES, s1.shape[1])
        kap_ref[hd] = _bf16_pair_bits(jnp.exp(tau - s1 - m2)).reshape(groups)
        e1_ref[hd] = _bf16_pair_bits(jnp.exp(s1 - m1) * (0.5 / z)).reshape(groups)
        p2_ref[hd] = jnp.exp(s2 - m2).astype(BF16)


def _peer_route(x, shift, scale, wq_t, keys, tokens_per_batch):
    n, d = x.shape
    tt = PEER_COLS
    per = tokens_per_batch // tt
    nh = keys.shape[0] // 2
    nk = keys.shape[1]
    big = pl.BlockSpec((1, nh, nk, tt), lambda i: (i, 0, 0, 0))
    rows = pl.BlockSpec((1, nh, nk // SUBLANES, SUBLANES, tt), lambda i: (i, 0, 0, 0, 0))
    rows_shape = jax.ShapeDtypeStruct((n // tt, nh, nk // SUBLANES, SUBLANES, tt), jnp.int32)
    return pl.pallas_call(
        _peer_route_body,
        grid=(n // tt,),
        in_specs=[
            pl.BlockSpec((tt, d), lambda i: (i, 0)),
            pl.BlockSpec((1, 1, d), lambda i: (i // per, 0, 0)),
            pl.BlockSpec((1, 1, d), lambda i: (i // per, 0, 0)),
            pl.BlockSpec(wq_t.shape, lambda i: (0, 0)),
            pl.BlockSpec(keys.shape, lambda i: (0, 0, 0)),
        ],
        out_specs=[pl.BlockSpec((1, d, tt), lambda i: (i, 0, 0)), rows, rows, big],
        out_shape=[jax.ShapeDtypeStruct((n // tt, d, tt), BF16), rows_shape, rows_shape,
                   jax.ShapeDtypeStruct((n // tt, nh, nk, tt), BF16)],
        compiler_params=_cparams("arbitrary"),
        name="peer_route",
    )(x, shift, scale, wq_t, keys)


PEER_UNIT_ROWS = 4


def _peer_expert_body(xm_ref, u_ref, vt_ref, kap_ref, e1_ref, p2_ref,
                      x_ref, gate_ref, g_ref, b_ref, o_ref, acc_ref, a_ref, w_ref, *, alpha):
    e = pl.program_id(1)
    ncb, nheads, ngroups, _, cw = kap_ref.shape
    nk = p2_ref.shape[2]
    ur = PEER_UNIT_ROWS
    ue = ur * nk
    nrp = ngroups * SUBLANES // ur
    n_units = ncb * nrp
    assert 2 * ur == SUBLANES and nrp % 2 == 0

    @pl.when(e == 0)
    def _():
        acc_ref[...] = jnp.zeros_like(acc_ref)

    def first_matmul(i, slot):
        rows = pl.ds(pl.multiple_of((i % nrp) * ue, ue), ue)
        a_ref[slot] = _dot(u_ref[rows, :], xm_ref[i // nrp])

    def second_matmul(i, slot):
        acc_ref[i // nrp] += _dot(vt_ref[i % nrp], w_ref[slot])

    def gate_and_activate(i, slot):
        c, grp = i // nrp, (i % nrp) // 2
        for r in range(ur):
            row = slot * ur + r
            gsum = None
            for hd in range(nheads):
                p2 = p2_ref[c, hd]
                keep = p2 >= _bf16_rows_from_pair_bits(kap_ref[c, hd, grp, row:row + 1, :], nk)
                gate = p2 * _bf16_rows_from_pair_bits(e1_ref[c, hd, grp, row:row + 1, :], nk)
                term = jnp.where(keep, gate, jnp.zeros_like(p2))
                gsum = term if gsum is None else gsum + term
            a = a_ref[slot, r * nk:(r + 1) * nk, :]
            act = (a * (1.0 + lax.erf(a * (2.0 ** -0.5)))).astype(BF16)
            w_ref[slot, r * nk:(r + 1) * nk, :] = gsum * act

    def steady(j, carry):
        i = 2 * j + 1
        first_matmul(i + 1, 0)
        gate_and_activate(i, 1)
        second_matmul(i - 1, 0)
        first_matmul(i + 2, 1)
        gate_and_activate(i + 1, 0)
        second_matmul(i, 1)
        return carry

    assert n_units % 2 == 0
    first_matmul(0, 0)
    first_matmul(1, 1)
    gate_and_activate(0, 0)
    lax.fori_loop(0, n_units // 2 - 1, steady, 0)
    gate_and_activate(n_units - 1, 1)
    second_matmul(n_units - 2, 0)
    second_matmul(n_units - 1, 1)

    @pl.when(e == pl.num_programs(1) - 1)
    def _():
        for c in range(ncb):
            f = acc_ref[c].T
            z = alpha * x_ref[c * cw:(c + 1) * cw, :] + gate_ref[0] * f
            o_ref[c * cw:(c + 1) * cw, :] = _ln(z) * g_ref[...] + b_ref[...]


def _peer_experts(xm_t, u, v_t, kap, e1, p2, x, gate, g, b, tokens_per_batch, alpha, tt=512, et=2048):
    n, d = x.shape
    tt = min(tt, tokens_per_batch)
    per = tokens_per_batch // tt
    _, nh, nk, cw = p2.shape
    ncb = tt // cw
    n1 = et // nk
    ue = PEER_UNIT_ROWS * nk
    sel = pl.BlockSpec((ncb, nh, n1 // SUBLANES, SUBLANES, cw), lambda i, e: (i, 0, e, 0, 0))
    full = pl.BlockSpec((ncb, nh, nk, cw), lambda i, e: (i, 0, 0, 0))
    return pl.pallas_call(
        functools.partial(_peer_expert_body, alpha=alpha),
        grid=(n // tt, u.shape[0] // et),
        in_specs=[
            pl.BlockSpec((ncb, d, cw), lambda i, e: (i, 0, 0)),
            pl.BlockSpec((et, d), lambda i, e: (e, 0)),
            pl.BlockSpec((et // ue, d, ue), lambda i, e: (e, 0, 0)),
            sel, sel, full,
            pl.BlockSpec((tt, d), lambda i, e: (i, 0)),
            pl.BlockSpec((1, 1, d), lambda i, e: (i // per, 0, 0)),
            pl.BlockSpec((1, d), lambda i, e: (0, 0)),
            pl.BlockSpec((1, d), lambda i, e: (0, 0)),
        ],
        out_specs=pl.BlockSpec((tt, d), lambda i, e: (i, 0)),
        out_shape=jax.ShapeDtypeStruct((n, d), F32),
        scratch_shapes=[
            pltpu.VMEM((ncb, d, cw), F32),
            pltpu.VMEM((2, ue, cw), F32),
            pltpu.VMEM((2, ue, cw), BF16),
        ],
        compiler_params=_cparams("arbitrary", "arbitrary"),
        name="peer_experts",
    )(xm_t, u, v_t, kap, e1, p2, x, gate, g.reshape(1, d), b.reshape(1, d))


def _peer_layer(x, shift, scale, gate, g, b, tables, tokens_per_batch, alpha):
    wq_t, keys, u, v_t = tables
    xm_t, kap, e1, p2 = _peer_route(x, shift, scale, wq_t, keys, tokens_per_batch)
    return _peer_experts(xm_t, u, v_t, kap, e1, p2, x, gate, g, b, tokens_per_batch, alpha)


def _rope_tables(seq):
    t = jnp.arange(seq)
    row = (t // GRID_W).astype(F32)
    col = (t % GRID_W).astype(F32)
    n_freq = HEAD_DIM // 4
    inv_freq = ROPE_THETA ** (-jnp.arange(n_freq, dtype=F32) / n_freq)
    ang = jnp.concatenate([row[:, None] * inv_freq, col[:, None] * inv_freq], -1)
    cos, sin = jnp.cos(ang), jnp.sin(ang)
    reps = LANES // (HEAD_DIM // 2)
    return jnp.tile(cos, (1, reps)), jnp.tile(sin, (1, reps))


def _rotate_half_columns(w, n_heads):
    d = w.shape[0]
    wh = w.reshape(d, n_heads, 2, HEAD_DIM // 2)
    return jnp.concatenate([-wh[:, :, 1], wh[:, :, 0]], axis=-1).reshape(d, n_heads * HEAD_DIM)


def _expert_table_body(u_ref, v_ref, ub_ref, vt_ref):
    ub_ref[...] = u_ref[...].astype(BF16)
    vt_ref[0] = v_ref[...].T.astype(BF16)


def _peer_tables(w_q, sub_keys, u, v):
    nh, _, nk, qd = sub_keys.shape
    ne, d = u.shape
    ue = PEER_UNIT_ROWS * nk
    u_b, v_chunks = pl.pallas_call(
        _expert_table_body,
        grid=(ne // ue,),
        in_specs=[pl.BlockSpec((ue, d), lambda i: (i, 0)), pl.BlockSpec((ue, d), lambda i: (i, 0))],
        out_specs=[pl.BlockSpec((ue, d), lambda i: (i, 0)), pl.BlockSpec((1, d, ue), lambda i: (i, 0, 0))],
        out_shape=[jax.ShapeDtypeStruct((ne, d), BF16), jax.ShapeDtypeStruct((ne // ue, d, ue), BF16)],
        compiler_params=_cparams("arbitrary"),
        name="expert_table_layout",
    )(u, v)
    return (w_q.T.astype(BF16), sub_keys.reshape(nh * 2, nk, qd).astype(BF16), u_b, v_chunks)


def kernel(x, c, ctx, c_ctx, ada_w, ada_b, post_ln_g, post_ln_b, even_w_in, even_w_out, na_rpb,
           odd_w_in, odd_w_out, gqa_sink, peer_w_q, peer_sub_keys, peer_u, peer_v):
    batch, seq, d = x.shape
    ctx_len = ctx.shape[1]
    depth = ada_w.shape[0]
    rows = seq // GRID_W
    alpha = float((2 * depth) ** 0.25)
    fw = FNET_GROUPS * HEAD_DIM
    nw = NA_HEADS * HEAD_DIM
    qw = GQA_Q_HEADS * HEAD_DIM
    kvw = GQA_KV_HEADS * HEAD_DIM
    qscale = HEAD_DIM ** -0.5

    cond = jnp.zeros((8, d), F32).at[:batch].set(c).at[batch].set(c_ctx)
    mods = _ada(cond, ada_w, ada_b)

    xl = x.reshape(batch * seq, d)
    hc = ctx.reshape(batch * ctx_len, d)
    cos_t, sin_t = _rope_tables(seq)

    cg, sg = _dft_tables(HEAD_DIM)
    eye = np.eye(FNET_GROUPS)
    chan = jnp.asarray(np.concatenate([np.kron(eye, cg), -np.kron(eye, sg)], axis=1), F32)

    for layer in range(depth):
        ctx_out = layer < depth - 1
        i = layer // 2
        m_l = [m.reshape(batch, 1, d) for m in jnp.split(mods[layer, :batch], 6, axis=-1)]
        m_c = [jnp.broadcast_to(m.reshape(1, 1, d), (batch, 1, d)) for m in jnp.split(mods[layer, batch], 6, axis=-1)]
        g0, b0 = post_ln_g[layer, 0], post_ln_b[layer, 0]
        g1, b1 = post_ln_g[layer, 1], post_ln_b[layer, 1]

        if layer % 2 == 0:
            w_in, w_out = even_w_in[i], even_w_out[i]
            w_f = _matmul_f32(w_in[:, :fw], chan)
            w_aug = jnp.concatenate([w_f, w_in[:, fw:]], axis=1).astype(BF16)
            plan = (("nat", 0, 2 * fw, 1.0, None),
                    ("heads", 2 * fw, nw, qscale, None),
                    ("heads", 2 * fw + nw, nw, 1.0, None),
                    ("heads", 2 * fw + 2 * nw, nw, 1.0, None))
            f_l, q_l, k_l, v_l = _proj(xl, m_l[0], m_l[1], w_aug, plan, seq)
            f_c, q_c, k_c, v_c = _proj(hc, m_c[0], m_c[1], w_aug, plan, ctx_len)
            col_bias, row_mask = _natten_tables(rows, na_rpb[i])
            na_l = _natten(q_l, k_l, v_l, k_c, v_c, col_bias, row_mask, batch, rows, ctx_len)
            fm_l = _fourier_latent(f_l, batch, rows, fw)
            w_out_b = w_out.astype(BF16)
            ws = [w_out_b[:fw], w_out_b[fw:]]
            xl_new = _outproj_ln([fm_l, na_l], ws, xl, m_l[2], g0, b0, seq, alpha)
            if ctx_out:
                na_c = _ctx_attn(q_c, k_c, v_c, batch, ctx_len)
                fm_c = _fourier_dense(f_c, batch, ctx_len, fw)
                hc_new = _outproj_ln([fm_c, na_c], ws, hc, m_c[2], g0, b0, ctx_len, alpha)
        else:
            w_in, w_out = odd_w_in[i], odd_w_out[i]
            wq, wk, wv = w_in[:, :qw], w_in[:, qw:qw + kvw], w_in[:, qw + kvw:]
            w_aug = jnp.concatenate([wq, wk, wv, _rotate_half_columns(wq, GQA_Q_HEADS),
                                     _rotate_half_columns(wk, GQA_KV_HEADS)], axis=1).astype(BF16)
            plan = (("heads", 0, qw, qscale, qw + 2 * kvw),
                    ("heads", qw, kvw, 1.0, 2 * qw + 2 * kvw),
                    ("heads", qw + kvw, kvw, 1.0, None))
            q_l, k_l, v_l = _proj(xl, m_l[0], m_l[1], w_aug, plan, seq, rope=(cos_t, sin_t))
            plan_c = (("heads", 0, kvw, 1.0, None), ("heads", kvw, kvw, 1.0, None))
            if ctx_out:
                raise NotImplementedError("an odd layer must be the last layer (no context output path)")
            k_c, v_c = _proj(hc, m_c[0], m_c[1], w_in[:, qw:].astype(BF16), plan_c, ctx_len)
            y_l = _gqa(q_l, k_l, v_l, k_c, v_c, gqa_sink[i], batch, seq, ctx_len)
            xl_new = _outproj_ln([y_l], [w_out.astype(BF16)], xl, m_l[2], g0, b0, seq, alpha)

        tables = _peer_tables(peer_w_q[layer], peer_sub_keys[layer], peer_u[layer], peer_v[layer])
        xl = _peer_layer(xl_new, m_l[3], m_l[4], m_l[5], g1, b1, tables, seq, alpha)
        if ctx_out:
            hc = _peer_layer(hc_new, m_c[3], m_c[4], m_c[5], g1, b1, tables, ctx_len, alpha)

    return xl.reshape(batch, seq, d)
```

```python
import functools
import math

import numpy as np
import jax
import jax.numpy as jnp
from jax import lax
from jax.experimental import pallas as pl
from jax.experimental.pallas import tpu as pltpu

F32 = jnp.float32
BF16 = jnp.bfloat16

HEAD_DIM = 64
GRID_W = 64
FNET_GROUPS = 8
NA_HEADS = 8
NA_KR = 8
NA_KC = 16
NA_ROWS_PER_BLOCK = 8
NA_WIN_ROWS = 16
GQA_Q_HEADS = 16
GQA_KV_HEADS = 4
GQA_WINDOW = 128
GQA_BLOCK = 128
ROPE_THETA = 10000.0
PEER_HEADS = 8
PEER_NKEYS = 128
PEER_TOPK = 16
LN_EPS = 1e-6
NEG = -1e30

PEER_COLS = 256
LANES = 128
SUBLANES = 8
VMEM_LIMIT = 56 * 1024 * 1024


def _cparams(*sem):
    return pltpu.CompilerParams(dimension_semantics=sem, vmem_limit_bytes=VMEM_LIMIT)


def _ln(x):
    mu = jnp.mean(x, axis=-1, keepdims=True)
    xc = x - mu
    var = jnp.mean(xc * xc, axis=-1, keepdims=True)
    return xc * lax.rsqrt(var + LN_EPS)


def _dot(a, b):
    return jnp.dot(a, b, preferred_element_type=F32)


def _dot_nt(a, b):
    return lax.dot_general(a, b, (((1,), (1,)), ((), ())), preferred_element_type=F32)


def _ada_body(c_ref, w_ref, b_ref, o_ref):
    c = c_ref[...]
    o_ref[0] = _dot(c * jax.nn.sigmoid(c), w_ref[0]) + b_ref[0]


def _ada(cond, ada_w, ada_b):
    depth, d, n = ada_w.shape
    tn = 1536
    return pl.pallas_call(
        _ada_body,
        grid=(depth, n // tn),
        in_specs=[
            pl.BlockSpec((8, d), lambda l, j: (0, 0)),
            pl.BlockSpec((1, d, tn), lambda l, j: (l, 0, j)),
            pl.BlockSpec((1, 1, tn), lambda l, j: (l, 0, j)),
        ],
        out_specs=pl.BlockSpec((1, 8, tn), lambda l, j: (l, 0, j)),
        out_shape=jax.ShapeDtypeStruct((depth, 8, n), F32),
        compiler_params=_cparams("arbitrary", "arbitrary"),
        name="ada_modulation",
    )(cond, ada_w, ada_b.reshape(depth, 1, n))


def _matmul_f32_body(a_ref, b_ref, o_ref):
    o_ref[...] = jnp.dot(a_ref[...], b_ref[...], preferred_element_type=F32, precision=lax.Precision.HIGHEST)


def _matmul_f32(a, b):
    m, k = a.shape
    n = b.shape[1]
    tm = min(256, m)
    return pl.pallas_call(
        _matmul_f32_body,
        grid=(m // tm,),
        in_specs=[pl.BlockSpec((tm, k), lambda i: (i, 0)), pl.BlockSpec((k, n), lambda i: (0, 0))],
        out_specs=pl.BlockSpec((tm, n), lambda i: (i, 0)),
        out_shape=jax.ShapeDtypeStruct((m, n), F32),
        compiler_params=_cparams("arbitrary"),
        name="small_matmul_f32",
    )(a, b)


def _proj_body(*refs, plan, use_rope):
    x_ref, sh_ref, sc_ref, w_ref = refs[:4]
    rest = refs[4:]
    if use_rope:
        cos_ref, sin_ref = rest[:2]
        rest = rest[2:]
    h = _ln(x_ref[...]) * (1.0 + sc_ref[0]) + sh_ref[0]
    acc = _dot(h.astype(BF16), w_ref[...])
    for o_ref, (kind, start, width, scale, rot_start) in zip(rest, plan):
        y = acc[:, start:start + width]
        if rot_start is not None:
            reps = width // LANES
            cos = jnp.tile(cos_ref[...], (1, reps))
            sin = jnp.tile(sin_ref[...], (1, reps))
            y = y * cos + acc[:, rot_start:rot_start + width] * sin
        if scale != 1.0:
            y = y * scale
        if kind == "nat":
            o_ref[...] = y.astype(o_ref.dtype)
        else:
            for hh in range(width // HEAD_DIM):
                o_ref[hh] = y[:, hh * HEAD_DIM:(hh + 1) * HEAD_DIM].astype(o_ref.dtype)


def _proj(x, shift, scale, w, plan, tokens_per_batch, rope=None, tm=512):
    n, d = x.shape
    tm = min(tm, tokens_per_batch)
    per = tokens_per_batch // tm
    in_specs = [
        pl.BlockSpec((tm, d), lambda i: (i, 0)),
        pl.BlockSpec((1, 1, d), lambda i: (i // per, 0, 0)),
        pl.BlockSpec((1, 1, d), lambda i: (i // per, 0, 0)),
        pl.BlockSpec(w.shape, lambda i: (0, 0)),
    ]
    args = [x, shift, scale, w]
    if rope is not None:
        in_specs += [pl.BlockSpec((tm, LANES), lambda i: (i % per, 0))] * 2
        args += list(rope)
    out_specs, out_shape = [], []
    for kind, start, width, sc, rot in plan:
        if kind == "nat":
            out_specs.append(pl.BlockSpec((tm, width), lambda i: (i, 0)))
            out_shape.append(jax.ShapeDtypeStruct((n, width), F32))
        else:
            nh = width // HEAD_DIM
            out_specs.append(pl.BlockSpec((nh, tm, HEAD_DIM), lambda i: (0, i, 0)))
            out_shape.append(jax.ShapeDtypeStruct((nh, n, HEAD_DIM), BF16))
    return pl.pallas_call(
        functools.partial(_proj_body, plan=plan, use_rope=rope is not None),
        grid=(n // tm,),
        in_specs=in_specs,
        out_specs=out_specs,
        out_shape=out_shape,
        compiler_params=_cparams("arbitrary"),
        name="modln_proj",
    )(*args)


def _outproj_body(*refs, n_in, alpha):
    ys, ws = refs[:n_in], refs[n_in:2 * n_in]
    x_ref, gate_ref, g_ref, b_ref, o_ref = refs[2 * n_in:]
    acc = None
    for y_ref, w_ref in zip(ys, ws):
        t = _dot(y_ref[...].astype(BF16), w_ref[...])
        acc = t if acc is None else acc + t
    z = alpha * x_ref[...] + gate_ref[0] * acc
    o_ref[...] = _ln(z) * g_ref[...] + b_ref[...]


def _outproj_ln(ys, ws, x, gate, g, b, tokens_per_batch, alpha, tm=512):
    n, d = x.shape
    tm = min(tm, tokens_per_batch)
    per = tokens_per_batch // tm
    in_specs = [pl.BlockSpec((tm, y.shape[1]), lambda i: (i, 0)) for y in ys]
    in_specs += [pl.BlockSpec(w.shape, lambda i: (0, 0)) for w in ws]
    in_specs += [
        pl.BlockSpec((tm, d), lambda i: (i, 0)),
        pl.BlockSpec((1, 1, d), lambda i: (i // per, 0, 0)),
        pl.BlockSpec((1, d), lambda i: (0, 0)),
        pl.BlockSpec((1, d), lambda i: (0, 0)),
    ]
    return pl.pallas_call(
        functools.partial(_outproj_body, n_in=len(ys), alpha=alpha),
        grid=(n // tm,),
        in_specs=in_specs,
        out_specs=pl.BlockSpec((tm, d), lambda i: (i, 0)),
        out_shape=jax.ShapeDtypeStruct((n, d), F32),
        compiler_params=_cparams("arbitrary"),
        name="outproj_residual_ln",
    )(*ys, *ws, x, gate, g.reshape(1, d), b.reshape(1, d))


def _dft_tables(n):
    idx = np.arange(n)
    ang = 2.0 * np.pi * ((idx[:, None] * idx[None, :]) % n) / n
    return np.cos(ang), np.sin(ang)


def _fourier_rows_body(x_ref, cs_ref, tc_ref, ts_ref, o_ref, *, cb, width):
    nr = cs_ref.shape[1]
    for j in range(cb):
        pq = _dot(cs_ref[...], x_ref[0, :, j, :])
        yr = pq[:nr, :width] + pq[nr:, width:]
        yi = pq[:nr, width:] - pq[nr:, :width]
        tc, ts = tc_ref[j], ts_ref[j]
        o_ref[0, :, j, :width] = yr * tc + yi * ts
        o_ref[0, :, j, width:] = yi * tc - yr * ts


def _fourier_cols_body(y_ref, cs_ref, o_ref, *, kb, width, norm):
    nc = cs_ref.shape[1]
    for j in range(kb):
        pq = _dot(cs_ref[...], y_ref[0, j])
        o_ref[0, :, j, :] = (pq[:nc, :width] + pq[nc:, width:]) * norm


def _fourier_latent(f, batch, rows, width):
    cols = GRID_W
    seq = rows * cols
    c_r, s_r = _dft_tables(rows)
    c_c, s_c = _dft_tables(cols)
    k1 = np.arange(rows)[None, :]
    cc = np.arange(cols)[:, None]
    tw = 2.0 * np.pi * ((cc * k1) % seq) / seq
    cs_r = jnp.asarray(np.concatenate([c_r, s_r], 0), F32)
    cs_c = jnp.asarray(np.concatenate([c_c, s_c], 0), F32)
    tc = jnp.asarray(np.cos(tw)[:, :, None], F32)
    ts = jnp.asarray(np.sin(tw)[:, :, None], F32)
    cb = SUBLANES
    kb = SUBLANES
    lane_w = 2 * width
    y = pl.pallas_call(
        functools.partial(_fourier_rows_body, cb=cb, width=width),
        grid=(batch, cols // cb),
        in_specs=[
            pl.BlockSpec((1, rows, cb, lane_w), lambda b, j: (b, 0, j, 0)),
            pl.BlockSpec((2 * rows, rows), lambda b, j: (0, 0)),
            pl.BlockSpec((cb, rows, 1), lambda b, j: (j, 0, 0)),
            pl.BlockSpec((cb, rows, 1), lambda b, j: (j, 0, 0)),
        ],
        out_specs=pl.BlockSpec((1, rows, cb, lane_w), lambda b, j: (b, 0, j, 0)),
        out_shape=jax.ShapeDtypeStruct((batch, rows, cols, lane_w), F32),
        compiler_params=_cparams("arbitrary", "arbitrary"),
        name="fourier_rows",
    )(f.reshape(batch, rows, cols, lane_w), cs_r, tc, ts)
    z = pl.pallas_call(
        functools.partial(_fourier_cols_body, kb=kb, width=width, norm=float((seq * HEAD_DIM) ** -0.5)),
        grid=(batch, rows // kb),
        in_specs=[
            pl.BlockSpec((1, kb, cols, lane_w), lambda b, j: (b, j, 0, 0)),
            pl.BlockSpec((2 * cols, cols), lambda b, j: (0, 0)),
        ],
        out_specs=pl.BlockSpec((1, cols, kb, width), lambda b, j: (b, 0, j, 0)),
        out_shape=jax.ShapeDtypeStruct((batch, cols, rows, width), F32),
        compiler_params=_cparams("arbitrary", "arbitrary"),
        name="fourier_cols",
    )(y, cs_c)
    return z.reshape(batch * seq, width)


def _fourier_dense_body(x_ref, c_ref, s_ref, o_ref, *, width, norm):
    x = x_ref[...]
    o_ref[...] = (_dot(c_ref[...], x[:, :width]) + _dot(s_ref[...], x[:, width:])) * norm


def _fourier_dense(f, batch, length, width):
    c, s = _dft_tables(length)
    return pl.pallas_call(
        functools.partial(_fourier_dense_body, width=width, norm=float((length * HEAD_DIM) ** -0.5)),
        grid=(batch,),
        in_specs=[
            pl.BlockSpec((length, 2 * width), lambda b: (b, 0)),
            pl.BlockSpec((length, length), lambda b: (0, 0)),
            pl.BlockSpec((length, length), lambda b: (0, 0)),
        ],
        out_specs=pl.BlockSpec((length, width), lambda b: (b, 0)),
        out_shape=jax.ShapeDtypeStruct((batch * length, width), F32),
        compiler_params=_cparams("arbitrary"),
        name="fourier_dense",
    )(f, jnp.asarray(c, F32), jnp.asarray(s, F32))


def _natten_tables(rows, rpb):
    w = GRID_W
    qb, kw = NA_ROWS_PER_BLOCK, NA_WIN_ROWS
    nh, ndr, ndc = rpb.shape
    tq, tk = np.divmod(np.arange(w * w), w)
    dc_i = np.clip(tk - tq, -(NA_KC - 1), NA_KC - 1) + (NA_KC - 1)
    onehot = np.zeros((LANES, w * w), np.float32)
    onehot[dc_i, np.arange(w * w)] = 1.0
    c_start = np.clip(tq - NA_KC // 2, 0, w - NA_KC)
    col_ok = ((tk >= c_start) & (tk < c_start + NA_KC)).reshape(w, w)
    rp = jnp.zeros((LANES, LANES), F32).at[:nh * ndr, :ndc].set(rpb.reshape(nh * ndr, ndc).astype(F32))
    tiles = _matmul_f32(rp, jnp.asarray(onehot))[:nh * ndr].reshape(nh, ndr, w, w)
    tiles = jnp.where(jnp.asarray(col_ok), tiles, NEG)
    dr = np.arange(kw)[None, :] - np.arange(qb)[:, None] - (kw - qb) // 2
    dr_i = np.clip(dr + (NA_KR - 1), 0, ndr - 1)
    col_bias = jnp.transpose(tiles[:, dr_i], (0, 1, 3, 2, 4)).reshape(nh, qb * w, kw * w)
    qi = np.repeat(np.arange(qb), w)
    km = np.repeat(np.arange(kw), w)
    nblk = rows // qb
    masks = []
    for j in range(nblk):
        qr = j * qb + qi
        kr = j * qb - (kw - qb) // 2 + km
        r_start = np.clip(qr - NA_KR // 2, 0, rows - NA_KR)
        ok = (kr[None, :] >= r_start[:, None]) & (kr[None, :] < r_start[:, None] + NA_KR)
        masks.append(np.where(ok, 0.0, NEG).astype(np.float32))
    kinds = [(j > 0) + (j == nblk - 1) for j in range(nblk)]
    table = np.zeros((3,) + masks[0].shape, np.float32)
    for j, kd in enumerate(kinds):
        table[kd] = masks[j]
    for j, kd in enumerate(kinds):
        assert np.array_equal(table[kd], masks[j])
    half, quarter = qb * w // 2, kw * w // 4
    assert (table[:, :half, 3 * quarter:] == NEG).all() and (table[:, half:, :quarter] == NEG).all()
    return col_bias, jnp.asarray(table)


def _natten_body(q_ref, k0_ref, k1_ref, k2_ref, k3_ref, v0_ref, v1_ref, v2_ref, v3_ref,
                 kc_ref, vc_ref, cb_ref, rm_ref, o_ref):
    k_refs = (k0_ref, k1_ref, k2_ref, k3_ref)
    v_refs = (v0_ref, v1_ref, v2_ref, v3_ref)
    qt, kb = q_ref.shape[1], k0_ref.shape[1]
    hq = qt // 2
    outs = []
    for hh in range(q_ref.shape[0]):
        halves = []
        for part in range(2):
            rows = slice(part * hq, (part + 1) * hq)
            cols = slice(part * kb, (part + 3) * kb)
            q = q_ref[hh, rows, :]
            s = jnp.concatenate([_dot_nt(q, kr[hh]) for kr in k_refs[part:part + 3]], axis=1)
            s = s + cb_ref[hh, rows, cols] + rm_ref[0, rows, cols]
            sc = _dot_nt(q, kc_ref[hh])
            m = jnp.maximum(jnp.max(s, axis=-1, keepdims=True), jnp.max(sc, axis=-1, keepdims=True))
            p = jnp.exp(s - m)
            pc = jnp.exp(sc - m)
            l = jnp.sum(p, axis=-1, keepdims=True) + jnp.sum(pc, axis=-1, keepdims=True)
            o = _dot(pc.astype(BF16), vc_ref[hh])
            for c, vr in enumerate(v_refs[part:part + 3]):
                o = o + _dot(p[:, c * kb:(c + 1) * kb].astype(BF16), vr[hh])
            halves.append(o / l)
        outs.append(jnp.concatenate(halves, axis=0))
    o_ref[...] = jnp.concatenate(outs, axis=1).astype(o_ref.dtype)


def _natten(q, k, v, kc, vc, col_bias, row_mask, batch, rows, ctx_len):
    nh, n, dh = q.shape
    hp = 4
    qt = NA_ROWS_PER_BLOCK * GRID_W
    kt = NA_WIN_ROWS * GRID_W // 4
    nblk = rows // NA_ROWS_PER_BLOCK
    kblocks = rows * GRID_W // kt

    def kv_spec(c):
        return pl.BlockSpec(
            (hp, kt, dh),
            lambda b, j, h: (h, b * kblocks + jnp.clip(2 * j - 1 + c, 0, kblocks - 1), 0))

    in_specs = [pl.BlockSpec((hp, qt, dh), lambda b, j, h: (h, b * nblk + j, 0))]
    in_specs += [kv_spec(c) for c in range(4)] * 2
    in_specs += [
        pl.BlockSpec((hp, ctx_len, dh), lambda b, j, h: (h, b, 0)),
        pl.BlockSpec((hp, ctx_len, dh), lambda b, j, h: (h, b, 0)),
        pl.BlockSpec((hp, qt, 4 * kt), lambda b, j, h: (h, 0, 0)),
        pl.BlockSpec((1, qt, 4 * kt), lambda b, j, h: ((j > 0).astype(jnp.int32) + (j == nblk - 1).astype(jnp.int32), 0, 0)),
    ]
    return pl.pallas_call(
        _natten_body,
        grid=(batch, nblk, nh // hp),
        in_specs=in_specs,
        out_specs=pl.BlockSpec((qt, hp * dh), lambda b, j, h: (b * nblk + j, h)),
        out_shape=jax.ShapeDtypeStruct((n, nh * dh), BF16),
        compiler_params=_cparams("arbitrary", "arbitrary", "arbitrary"),
        name="neighbourhood_attention",
    )(q, k, k, k, k, v, v, v, v, kc, vc, col_bias, row_mask)


def _ctx_attn_body(q_ref, k_ref, v_ref, o_ref):
    outs = []
    for hh in range(q_ref.shape[0]):
        s = _dot_nt(q_ref[hh], k_ref[hh])
        m = jnp.max(s, axis=-1, keepdims=True)
        p = jnp.exp(s - m)
        l = jnp.sum(p, axis=-1, keepdims=True)
        outs.append(_dot(p.astype(BF16), v_ref[hh]) / l)
    o_ref[...] = jnp.concatenate(outs, axis=1).astype(o_ref.dtype)


def _ctx_attn(q, k, v, batch, ctx_len):
    nh, n, dh = q.shape
    hp = 2
    spec = pl.BlockSpec((hp, ctx_len, dh), lambda b, h: (h, b, 0))
    return pl.pallas_call(
        _ctx_attn_body,
        grid=(batch, nh // hp),
        in_specs=[spec, spec, spec],
        out_specs=pl.BlockSpec((ctx_len, hp * dh), lambda b, h: (b, h)),
        out_shape=jax.ShapeDtypeStruct((n, nh * dh), BF16),
        compiler_params=_cparams("arbitrary", "arbitrary"),
        name="context_attention",
    )(q, k, v)


def _gqa_body(sink_ref, q_ref, kp_ref, kc_ref, kn_ref, vp_ref, vc_ref, vn_ref, kx_ref, vx_ref, o_ref, *, nb):
    hkv, blk, dh = kc_ref.shape
    g = q_ref.shape[0] // hkv
    n = pl.program_id(1)
    shape = (g * blk, 3 * blk)
    qi = lax.broadcasted_iota(jnp.int32, shape, 0) % blk
    kp = lax.broadcasted_iota(jnp.int32, shape, 1) - blk
    ok = (jnp.abs(qi - kp) <= GQA_WINDOW) & ((kp >= 0) | (n > 0)) & ((kp < blk) | (n < nb - 1))
    grp = lax.broadcasted_iota(jnp.int32, (g * blk, 1), 0) // blk
    for kvh in range(hkv):
        q = q_ref[kvh * g:(kvh + 1) * g].reshape(g * blk, dh)
        s = jnp.concatenate([_dot_nt(q, r[kvh]) for r in (kp_ref, kc_ref, kn_ref)], axis=1)
        s = jnp.where(ok, s, NEG)
        sx = _dot_nt(q, kx_ref[kvh])
        sk = jnp.zeros((g * blk, 1), F32)
        for gi in range(g):
            sk = jnp.where(grp == gi, sink_ref[kvh * g + gi], sk)
        m = jnp.maximum(jnp.maximum(jnp.max(s, axis=-1, keepdims=True), jnp.max(sx, axis=-1, keepdims=True)), sk)
        p = jnp.exp(s - m)
        px = jnp.exp(sx - m)
        l = jnp.sum(p, axis=-1, keepdims=True) + jnp.sum(px, axis=-1, keepdims=True) + jnp.exp(sk - m)
        o = _dot(px.astype(BF16), vx_ref[kvh])
        for c, vr in enumerate((vp_ref, vc_ref, vn_ref)):
            o = o + _dot(p[:, c * blk:(c + 1) * blk].astype(BF16), vr[kvh])
        o = o / l
        o_ref[:, kvh * g * dh:(kvh + 1) * g * dh] = jnp.concatenate(
            [o[gi * blk:(gi + 1) * blk] for gi in range(g)], axis=1).astype(o_ref.dtype)


def _gqa(q, k, v, kx, vx, sink, batch, seq, ctx_len):
    hq, n, dh = q.shape
    hkv = k.shape[0]
    g = hq // hkv
    blk = GQA_BLOCK
    nb = seq // blk

    def kv_spec(c):
        return pl.BlockSpec((hkv, blk, dh), lambda b, i, s: (0, b * nb + jnp.clip(i - 1 + c, 0, nb - 1), 0))

    x_spec = pl.BlockSpec((hkv, ctx_len, dh), lambda b, i, s: (0, b, 0))
    grid_spec = pltpu.PrefetchScalarGridSpec(
        num_scalar_prefetch=1,
        grid=(batch, nb),
        in_specs=[pl.BlockSpec((hq, blk, dh), lambda b, i, s: (0, b * nb + i, 0))]
        + [kv_spec(c) for c in range(3)] * 2 + [x_spec, x_spec],
        out_specs=pl.BlockSpec((blk, hq * dh), lambda b, i, s: (b * nb + i, 0)),
    )
    return pl.pallas_call(
        functools.partial(_gqa_body, nb=nb),
        grid_spec=grid_spec,
        out_shape=jax.ShapeDtypeStruct((n, hq * dh), BF16),
        compiler_params=_cparams("arbitrary", "arbitrary"),
        name="window_gqa",
    )(sink, q, k, k, k, v, v, v, kx, vx)


def _oddeven_merge(lo, hi, r):
    step = r * 2
    if step < hi - lo:
        yield from _oddeven_merge(lo, hi, step)
        yield from _oddeven_merge(lo + r, hi, step)
        yield from [(i, i + r) for i in range(lo + r, hi - r, step)]
    else:
        yield (lo, lo + r)


def _oddeven_sort(lo, hi):
    if hi - lo >= 1:
        mid = lo + (hi - lo) // 2
        yield from _oddeven_sort(lo, mid)
        yield from _oddeven_sort(mid + 1, hi)
        yield from _oddeven_merge(lo, hi, 1)


def _exchange(t, i, j):
    t[i], t[j] = jnp.maximum(t[i], t[j]), jnp.minimum(t[i], t[j])


def _sort_tiles(tiles):
    t = list(tiles)
    for i, j in _oddeven_sort(0, PEER_TOPK - 1):
        if j < len(t):
            _exchange(t, i, j)
    return t


def _top_tiles(sorted_tiles):
    w = list(sorted_tiles)
    n = len(w)
    shift = SUBLANES // 2
    while shift:
        other = [pltpu.roll(x, shift, 0) for x in w]
        w = [jnp.maximum(w[k], other[n - 1 - k]) for k in range(n)]
        d = n // 2
        while d:
            for k in range(n):
                if not k & d:
                    _exchange(w, k, k + d)
            d //= 2
        shift //= 2
    return w


def _next_below(tiles, bound):
    m = None
    for t in tiles:
        v = jnp.where(t < bound, t, -jnp.inf)
        m = v if m is None else jnp.maximum(m, v)
    return jnp.max(m, axis=0, keepdims=True)


def _stack_sublanes(vals, sub):
    out = vals[0]
    for j in range(1, len(vals)):
        out = jnp.where(sub == j, vals[j], out)
    return out


def _bf16_rounded(v):
    return v.astype(BF16).astype(F32)


def _bf16_pair_bits(v):
    hi = pltpu.bitcast(_bf16_rounded(v), jnp.int32)
    return hi | lax.shift_right_logical(hi, 16)


def _bf16_rows_from_pair_bits(bits_row, nrows):
    packed = pltpu.bitcast(jnp.broadcast_to(bits_row, (SUBLANES, bits_row.shape[1])), BF16)
    return jnp.tile(packed, (nrows // packed.shape[0], 1))


def _peer_route_body(x_ref, sh_ref, sc_ref, wq_ref, keys_ref, xm_ref, kap_ref, e1_ref, p2_ref):
    xm_ref, kap_ref, e1_ref, p2_ref = (r.at[0] for r in (xm_ref, kap_ref, e1_ref, p2_ref))
    nheads = p2_ref.shape[0]
    nk = keys_ref.shape[1]
    h = _ln(x_ref[...]) * (1.0 + sc_ref[0]) + sh_ref[0]
    ht = h.T.astype(BF16)
    xm_ref[...] = ht
    qt = _dot(wq_ref[...], ht).astype(BF16)
    qd = keys_ref.shape[2]
    k16 = PEER_TOPK
    sub = lax.broadcasted_iota(jnp.int32, (SUBLANES, 1), 0)
    for hd in range(nheads):
        halves = []
        for half in range(2):
            hp = hd * 2 + half
            s = _dot(keys_ref[hp], qt[hp * qd:(hp + 1) * qd, :])
            tiles = [s[SUBLANES * k:SUBLANES * (k + 1)] for k in range(nk // SUBLANES)]
            top = _top_tiles(_sort_tiles(tiles))
            halves.append((s, top, _next_below(tiles, top[-1])))
        (s1, t1, t1_17), (s2, t2, t2_17) = halves
        m1, m2 = t1[0][:1], t2[0][:1]
        cmax = m1 + m2
        t2_lo, t2_hi = _stack_sublanes(t2[:SUBLANES], sub), _stack_sublanes(t2[SUBLANES:], sub)
        t1_hi = _stack_sublanes(t1[SUBLANES:], sub)
        pairs = [(t1[0], t2_lo), (t1[0], t2_hi)]
        pairs += [(t1[a], jnp.where(sub < (k16 + 1) // (a + 1), t2_lo, -jnp.inf)) for a in range(1, SUBLANES)]
        pairs += [(t1_hi, t2[0])]
        pairs += [(jnp.where(sub == 0, t1[0], t1_17), jnp.where(sub == 0, t2_17, jnp.where(sub == 1, t2[0], -jnp.inf)))]
        cands = [a + b for a, b in pairs]
        ctop = _top_tiles(_sort_tiles(cands) + [jnp.full_like(cands[0], -jnp.inf)] * (k16 - len(cands)))
        c16 = ctop[-1][:1]
        tau = 0.5 * (c16 + _next_below(cands, c16))
        z = None
        for (a, b), c in zip(pairs, cands):
            picked = _bf16_rounded(jnp.exp(b - m2)) >= _bf16_rounded(jnp.exp(tau - a - m2))
            zc = jnp.where(picked, jnp.exp(c - cmax), 0.0)
            z = zc if z is None else z + zc
        z = jnp.sum(z, axis=0, keepdims=True)
        groups = (nk // SUBLANES, SUBLANES, s1.shape[1])
        kap_ref[hd] = _bf16_pair_bits(jnp.exp(tau - s1 - m2)).reshape(groups)
        e1_ref[hd] = _bf16_pair_bits(jnp.exp(s1 - m1) * (0.5 / z)).reshape(groups)
        p2_ref[hd] = jnp.exp(s2 - m2).astype(BF16)


def _peer_route(x, shift, scale, wq_t, keys, tokens_per_batch):
    n, d = x.shape
    tt = PEER_COLS
    per = tokens_per_batch // tt
    nh = keys.shape[0] // 2
    nk = keys.shape[1]
    big = pl.BlockSpec((1, nh, nk, tt), lambda i: (i, 0, 0, 0))
    rows = pl.BlockSpec((1, nh, nk // SUBLANES, SUBLANES, tt), lambda i: (i, 0, 0, 0, 0))
    rows_shape = jax.ShapeDtypeStruct((n // tt, nh, nk // SUBLANES, SUBLANES, tt), jnp.int32)
    return pl.pallas_call(
        _peer_route_body,
        grid=(n // tt,),
        in_specs=[
            pl.BlockSpec((tt, d), lambda i: (i, 0)),
            pl.BlockSpec((1, 1, d), lambda i: (i // per, 0, 0)),
            pl.BlockSpec((1, 1, d), lambda i: (i // per, 0, 0)),
            pl.BlockSpec(wq_t.shape, lambda i: (0, 0)),
            pl.BlockSpec(keys.shape, lambda i: (0, 0, 0)),
        ],
        out_specs=[pl.BlockSpec((1, d, tt), lambda i: (i, 0, 0)), rows, rows, big],
        out_shape=[jax.ShapeDtypeStruct((n // tt, d, tt), BF16), rows_shape, rows_shape,
                   jax.ShapeDtypeStruct((n // tt, nh, nk, tt), BF16)],
        compiler_params=_cparams("arbitrary"),
        name="peer_route",
    )(x, shift, scale, wq_t, keys)


PEER_UNIT_ROWS = 4


def _peer_expert_body(xm_ref, u_ref, vt_ref, kap_ref, e1_ref, p2_ref,
                      x_ref, gate_ref, g_ref, b_ref, o_ref, acc_ref, a_ref, w_ref, *, alpha):
    e = pl.program_id(1)
    ncb, nheads, ngroups, _, cw = kap_ref.shape
    nk = p2_ref.shape[2]
    ur = PEER_UNIT_ROWS
    ue = ur * nk
    nrp = ngroups * SUBLANES // ur
    n_units = ncb * nrp
    assert 2 * ur == SUBLANES and nrp % 2 == 0

    @pl.when(e == 0)
    def _():
        acc_ref[...] = jnp.zeros_like(acc_ref)

    def first_matmul(i, slot):
        rows = pl.ds(pl.multiple_of((i % nrp) * ue, ue), ue)
        a_ref[slot] = _dot(u_ref[rows, :], xm_ref[i // nrp])

    def second_matmul(i, slot):
        acc_ref[i // nrp] += _dot(vt_ref[i % nrp], w_ref[slot])

    def gate_and_activate(i, slot):
        c, grp = i // nrp, (i % nrp) // 2
        for r in range(ur):
            row = slot * ur + r
            gsum = None
            for hd in range(nheads):
                p2 = p2_ref[c, hd]
                keep = p2 >= _bf16_rows_from_pair_bits(kap_ref[c, hd, grp, row:row + 1, :], nk)
                gate = p2 * _bf16_rows_from_pair_bits(e1_ref[c, hd, grp, row:row + 1, :], nk)
                term = jnp.where(keep, gate, jnp.zeros_like(p2))
                gsum = term if gsum is None else gsum + term
            a = a_ref[slot, r * nk:(r + 1) * nk, :]
            act = (a * (1.0 + lax.erf(a * (2.0 ** -0.5)))).astype(BF16)
            w_ref[slot, r * nk:(r + 1) * nk, :] = gsum * act

    def steady(j, carry):
        i = 2 * j + 1
        first_matmul(i + 1, 0)
        gate_and_activate(i, 1)
        second_matmul(i - 1, 0)
        first_matmul(i + 2, 1)
        gate_and_activate(i + 1, 0)
        second_matmul(i, 1)
        return carry

    assert n_units % 2 == 0
    first_matmul(0, 0)
    first_matmul(1, 1)
    gate_and_activate(0, 0)
    lax.fori_loop(0, n_units // 2 - 1, steady, 0)
    gate_and_activate(n_units - 1, 1)
    second_matmul(n_units - 2, 0)
    second_matmul(n_units - 1, 1)

    @pl.when(e == pl.num_programs(1) - 1)
    def _():
        for c in range(ncb):
            f = acc_ref[c].T
            z = alpha * x_ref[c * cw:(c + 1) * cw, :] + gate_ref[0] * f
            o_ref[c * cw:(c + 1) * cw, :] = _ln(z) * g_ref[...] + b_ref[...]


def _peer_experts(xm_t, u, v_t, kap, e1, p2, x, gate, g, b, tokens_per_batch, alpha, tt=512, et=2048):
    n, d = x.shape
    tt = min(tt, tokens_per_batch)
    per = tokens_per_batch // tt
    _, nh, nk, cw = p2.shape
    ncb = tt // cw
    n1 = et // nk
    ue = PEER_UNIT_ROWS * nk
    sel = pl.BlockSpec((ncb, nh, n1 // SUBLANES, SUBLANES, cw), lambda i, e: (i, 0, e, 0, 0))
    full = pl.BlockSpec((ncb, nh, nk, cw), lambda i, e: (i, 0, 0, 0))
    return pl.pallas_call(
        functools.partial(_peer_expert_body, alpha=alpha),
        grid=(n // tt, u.shape[0] // et),
        in_specs=[
            pl.BlockSpec((ncb, d, cw), lambda i, e: (i, 0, 0)),
            pl.BlockSpec((et, d), lambda i, e: (e, 0)),
            pl.BlockSpec((et // ue, d, ue), lambda i, e: (e, 0, 0)),
            sel, sel, full,
            pl.BlockSpec((tt, d), lambda i, e: (i, 0)),
            pl.BlockSpec((1, 1, d), lambda i, e: (i // per, 0, 0)),
            pl.BlockSpec((1, d), lambda i, e: (0, 0)),
            pl.BlockSpec((1, d), lambda i, e: (0, 0)),
        ],
        out_specs=pl.BlockSpec((tt, d), lambda i, e: (i, 0)),
        out_shape=jax.ShapeDtypeStruct((n, d), F32),
        scratch_shapes=[
            pltpu.VMEM((ncb, d, cw), F32),
            pltpu.VMEM((2, ue, cw), F32),
            pltpu.VMEM((2, ue, cw), BF16),
        ],
        compiler_params=_cparams("arbitrary", "arbitrary"),
        name="peer_experts",
    )(xm_t, u, v_t, kap, e1, p2, x, gate, g.reshape(1, d), b.reshape(1, d))


def _peer_layer(x, shift, scale, gate, g, b, tables, tokens_per_batch, alpha):
    wq_t, keys, u, v_t = tables
    xm_t, kap, e1, p2 = _peer_route(x, shift, scale, wq_t, keys, tokens_per_batch)
    return _peer_experts(xm_t, u, v_t, kap, e1, p2, x, gate, g, b, tokens_per_batch, alpha)


def _rope_tables(seq):
    t = jnp.arange(seq)
    row = (t // GRID_W).astype(F32)
    col = (t % GRID_W).astype(F32)
    n_freq = HEAD_DIM // 4
    inv_freq = ROPE_THETA ** (-jnp.arange(n_freq, dtype=F32) / n_freq)
    ang = jnp.concatenate([row[:, None] * inv_freq, col[:, None] * inv_freq], -1)
    cos, sin = jnp.cos(ang), jnp.sin(ang)
    reps = LANES // (HEAD_DIM // 2)
    return jnp.tile(cos, (1, reps)), jnp.tile(sin, (1, reps))


def _rotate_half_columns(w, n_heads):
    d = w.shape[0]
    wh = w.reshape(d, n_heads, 2, HEAD_DIM // 2)
    return jnp.concatenate([-wh[:, :, 1], wh[:, :, 0]], axis=-1).reshape(d, n_heads * HEAD_DIM)


def _expert_table_body(u_ref, v_ref, ub_ref, vt_ref):
    ub_ref[...] = u_ref[0].astype(BF16)
    vt_ref[0] = v_ref[0].T.astype(BF16)


def _peer_tables(w_q, sub_keys, u_all, v_all, layer):
    nh, _, nk, qd = sub_keys.shape
    _, ne, d = u_all.shape
    ue = PEER_UNIT_ROWS * nk
    u_b, v_chunks = pl.pallas_call(
        _expert_table_body,
        grid=(ne // ue,),
        in_specs=[pl.BlockSpec((1, ue, d), lambda i: (layer, i, 0)), pl.BlockSpec((1, ue, d), lambda i: (layer, i, 0))],
        out_specs=[pl.BlockSpec((ue, d), lambda i: (i, 0)), pl.BlockSpec((1, d, ue), lambda i: (i, 0, 0))],
        out_shape=[jax.ShapeDtypeStruct((ne, d), BF16), jax.ShapeDtypeStruct((ne // ue, d, ue), BF16)],
        compiler_params=_cparams("arbitrary"),
        name="expert_table_layout",
    )(u_all, v_all)
    return (w_q.T.astype(BF16), sub_keys.reshape(nh * 2, nk, qd).astype(BF16), u_b, v_chunks)


def kernel(x, c, ctx, c_ctx, ada_w, ada_b, post_ln_g, post_ln_b, even_w_in, even_w_out, na_rpb,
           odd_w_in, odd_w_out, gqa_sink, peer_w_q, peer_sub_keys, peer_u, peer_v):
    batch, seq, d = x.shape
    ctx_len = ctx.shape[1]
    depth = ada_w.shape[0]
    rows = seq // GRID_W
    alpha = float((2 * depth) ** 0.25)
    fw = FNET_GROUPS * HEAD_DIM
    nw = NA_HEADS * HEAD_DIM
    qw = GQA_Q_HEADS * HEAD_DIM
    kvw = GQA_KV_HEADS * HEAD_DIM
    qscale = HEAD_DIM ** -0.5

    cond = jnp.zeros((8, d), F32).at[:batch].set(c).at[batch].set(c_ctx)
    mods = _ada(cond, ada_w, ada_b)

    xl = x.reshape(batch * seq, d)
    hc = ctx.reshape(batch * ctx_len, d)
    cos_t, sin_t = _rope_tables(seq)

    cg, sg = _dft_tables(HEAD_DIM)
    eye = np.eye(FNET_GROUPS)
    chan = jnp.asarray(np.concatenate([np.kron(eye, cg), -np.kron(eye, sg)], axis=1), F32)

    for layer in range(depth):
        ctx_out = layer < depth - 1
        i = layer // 2
        m_l = [m.reshape(batch, 1, d) for m in jnp.split(mods[layer, :batch], 6, axis=-1)]
        m_c = [jnp.broadcast_to(m.reshape(1, 1, d), (batch, 1, d)) for m in jnp.split(mods[layer, batch], 6, axis=-1)]
        g0, b0 = post_ln_g[layer, 0], post_ln_b[layer, 0]
        g1, b1 = post_ln_g[layer, 1], post_ln_b[layer, 1]

        if layer % 2 == 0:
            w_in, w_out = even_w_in[i], even_w_out[i]
            w_f = _matmul_f32(w_in[:, :fw], chan)
            w_aug = jnp.concatenate([w_f, w_in[:, fw:]], axis=1).astype(BF16)
            plan = (("nat", 0, 2 * fw, 1.0, None),
                    ("heads", 2 * fw, nw, qscale, None),
                    ("heads", 2 * fw + nw, nw, 1.0, None),
                    ("heads", 2 * fw + 2 * nw, nw, 1.0, None))
            f_l, q_l, k_l, v_l = _proj(xl, m_l[0], m_l[1], w_aug, plan, seq)
            f_c, q_c, k_c, v_c = _proj(hc, m_c[0], m_c[1], w_aug, plan, ctx_len)
            col_bias, row_mask = _natten_tables(rows, na_rpb[i])
            na_l = _natten(q_l, k_l, v_l, k_c, v_c, col_bias, row_mask, batch, rows, ctx_len)
            fm_l = _fourier_latent(f_l, batch, rows, fw)
            w_out_b = w_out.astype(BF16)
            ws = [w_out_b[:fw], w_out_b[fw:]]
            xl_new = _outproj_ln([fm_l, na_l], ws, xl, m_l[2], g0, b0, seq, alpha)
            if ctx_out:
                na_c = _ctx_attn(q_c, k_c, v_c, batch, ctx_len)
                fm_c = _fourier_dense(f_c, batch, ctx_len, fw)
                hc_new = _outproj_ln([fm_c, na_c], ws, hc, m_c[2], g0, b0, ctx_len, alpha)
        else:
            w_in, w_out = odd_w_in[i], odd_w_out[i]
            wq, wk, wv = w_in[:, :qw], w_in[:, qw:qw + kvw], w_in[:, qw + kvw:]
            w_aug = jnp.concatenate([wq, wk, wv, _rotate_half_columns(wq, GQA_Q_HEADS),
                                     _rotate_half_columns(wk, GQA_KV_HEADS)], axis=1).astype(BF16)
            plan = (("heads", 0, qw, qscale, qw + 2 * kvw),
                    ("heads", qw, kvw, 1.0, 2 * qw + 2 * kvw),
                    ("heads", qw + kvw, kvw, 1.0, None))
            q_l, k_l, v_l = _proj(xl, m_l[0], m_l[1], w_aug, plan, seq, rope=(cos_t, sin_t))
            plan_c = (("heads", 0, kvw, 1.0, None), ("heads", kvw, kvw, 1.0, None))
            if ctx_out:
                raise NotImplementedError("an odd layer must be the last layer (no context output path)")
            k_c, v_c = _proj(hc, m_c[0], m_c[1], w_in[:, qw:].astype(BF16), plan_c, ctx_len)
            y_l = _gqa(q_l, k_l, v_l, k_c, v_c, gqa_sink[i], batch, seq, ctx_len)
            xl_new = _outproj_ln([y_l], [w_out.astype(BF16)], xl, m_l[2], g0, b0, seq, alpha)

        tables = _peer_tables(peer_w_q[layer], peer_sub_keys[layer], peer_u, peer_v, layer)
        xl = _peer_layer(xl_new, m_l[3], m_l[4], m_l[5], g1, b1, tables, seq, alpha)
        if ctx_out:
            hc = _peer_layer(hc_new, m_c[3], m_c[4], m_c[5], g1, b1, tables, ctx_len, alpha)

    return xl.reshape(batch, seq, d)
```

```python
import functools
import math

import numpy as np
import jax
import jax.numpy as jnp
from jax import lax
from jax.experimental import pallas as pl
from jax.experimental.pallas import tpu as pltpu

F32 = jnp.float32
BF16 = jnp.bfloat16

HEAD_DIM = 64
GRID_W = 64
FNET_GROUPS = 8
NA_HEADS = 8
NA_KR = 8
NA_KC = 16
NA_ROWS_PER_BLOCK = 8
NA_WIN_ROWS = 16
GQA_Q_HEADS = 16
GQA_KV_HEADS = 4
GQA_WINDOW = 128
GQA_BLOCK = 128
ROPE_THETA = 10000.0
PEER_HEADS = 8
PEER_NKEYS = 128
PEER_TOPK = 16
LN_EPS = 1e-6
NEG = -1e30

PEER_COLS = 256
LANES = 128
SUBLANES = 8
VMEM_LIMIT = 56 * 1024 * 1024


def _cparams(*sem):
    return pltpu.CompilerParams(dimension_semantics=sem, vmem_limit_bytes=VMEM_LIMIT)


def _ln(x):
    mu = jnp.mean(x, axis=-1, keepdims=True)
    xc = x - mu
    var = jnp.mean(xc * xc, axis=-1, keepdims=True)
    return xc * lax.rsqrt(var + LN_EPS)


def _dot(a, b):
    return jnp.dot(a, b, preferred_element_type=F32)


def _dot_nt(a, b):
    return lax.dot_general(a, b, (((1,), (1,)), ((), ())), preferred_element_type=F32)


def _ada_body(c_ref, w_ref, b_ref, o_ref):
    c = c_ref[...]
    o_ref[0] = _dot(c * jax.nn.sigmoid(c), w_ref[0]) + b_ref[0]


def _ada(cond, ada_w, ada_b):
    depth, d, n = ada_w.shape
    tn = 1536
    return pl.pallas_call(
        _ada_body,
        grid=(depth, n // tn),
        in_specs=[
            pl.BlockSpec((8, d), lambda l, j: (0, 0)),
            pl.BlockSpec((1, d, tn), lambda l, j: (l, 0, j)),
            pl.BlockSpec((1, 1, tn), lambda l, j: (l, 0, j)),
        ],
        out_specs=pl.BlockSpec((1, 8, tn), lambda l, j: (l, 0, j)),
        out_shape=jax.ShapeDtypeStruct((depth, 8, n), F32),
        compiler_params=_cparams("arbitrary", "arbitrary"),
        name="ada_modulation",
    )(cond, ada_w, ada_b.reshape(depth, 1, n))


def _matmul_f32_body(a_ref, b_ref, o_ref):
    o_ref[...] = jnp.dot(a_ref[...], b_ref[...], preferred_element_type=F32, precision=lax.Precision.HIGHEST)


def _matmul_f32(a, b):
    m, k = a.shape
    n = b.shape[1]
    tm = min(256, m)
    return pl.pallas_call(
        _matmul_f32_body,
        grid=(m // tm,),
        in_specs=[pl.BlockSpec((tm, k), lambda i: (i, 0)), pl.BlockSpec((k, n), lambda i: (0, 0))],
        out_specs=pl.BlockSpec((tm, n), lambda i: (i, 0)),
        out_shape=jax.ShapeDtypeStruct((m, n), F32),
        compiler_params=_cparams("arbitrary"),
        name="small_matmul_f32",
    )(a, b)


def _proj_body(*refs, plan, use_rope):
    x_ref, sh_ref, sc_ref, w_ref = refs[:4]
    rest = refs[4:]
    if use_rope:
        cos_ref, sin_ref = rest[:2]
        rest = rest[2:]
    h = _ln(x_ref[...]) * (1.0 + sc_ref[0]) + sh_ref[0]
    acc = _dot(h.astype(BF16), w_ref[...])
    for o_ref, (kind, start, width, scale, rot_start) in zip(rest, plan):
        y = acc[:, start:start + width]
        if rot_start is not None:
            reps = width // LANES
            cos = jnp.tile(cos_ref[...], (1, reps))
            sin = jnp.tile(sin_ref[...], (1, reps))
            y = y * cos + acc[:, rot_start:rot_start + width] * sin
        if scale != 1.0:
            y = y * scale
        if kind == "nat":
            o_ref[...] = y.astype(o_ref.dtype)
        else:
            for hh in range(width // HEAD_DIM):
                o_ref[hh] = y[:, hh * HEAD_DIM:(hh + 1) * HEAD_DIM].astype(o_ref.dtype)


def _proj(x, shift, scale, w, plan, tokens_per_batch, rope=None, tm=512):
    n, d = x.shape
    tm = min(tm, tokens_per_batch)
    per = tokens_per_batch // tm
    in_specs = [
        pl.BlockSpec((tm, d), lambda i: (i, 0)),
        pl.BlockSpec((1, 1, d), lambda i: (i // per, 0, 0)),
        pl.BlockSpec((1, 1, d), lambda i: (i // per, 0, 0)),
        pl.BlockSpec(w.shape, lambda i: (0, 0)),
    ]
    args = [x, shift, scale, w]
    if rope is not None:
        in_specs += [pl.BlockSpec((tm, LANES), lambda i: (i % per, 0))] * 2
        args += list(rope)
    out_specs, out_shape = [], []
    for kind, start, width, sc, rot in plan:
        if kind == "nat":
            out_specs.append(pl.BlockSpec((tm, width), lambda i: (i, 0)))
            out_shape.append(jax.ShapeDtypeStruct((n, width), F32))
        else:
            nh = width // HEAD_DIM
            out_specs.append(pl.BlockSpec((nh, tm, HEAD_DIM), lambda i: (0, i, 0)))
            out_shape.append(jax.ShapeDtypeStruct((nh, n, HEAD_DIM), BF16))
    return pl.pallas_call(
        functools.partial(_proj_body, plan=plan, use_rope=rope is not None),
        grid=(n // tm,),
        in_specs=in_specs,
        out_specs=out_specs,
        out_shape=out_shape,
        compiler_params=_cparams("arbitrary"),
        name="modln_proj",
    )(*args)


def _outproj_body(*refs, n_in, alpha):
    ys, ws = refs[:n_in], refs[n_in:2 * n_in]
    x_ref, gate_ref, g_ref, b_ref, o_ref = refs[2 * n_in:]
    acc = None
    for y_ref, w_ref in zip(ys, ws):
        t = _dot(y_ref[...].astype(BF16), w_ref[...])
        acc = t if acc is None else acc + t
    z = alpha * x_ref[...] + gate_ref[0] * acc
    o_ref[...] = _ln(z) * g_ref[...] + b_ref[...]


def _outproj_ln(ys, ws, x, gate, g, b, tokens_per_batch, alpha, tm=512):
    n, d = x.shape
    tm = min(tm, tokens_per_batch)
    per = tokens_per_batch // tm
    in_specs = [pl.BlockSpec((tm, y.shape[1]), lambda i: (i, 0)) for y in ys]
    in_specs += [pl.BlockSpec(w.shape, lambda i: (0, 0)) for w in ws]
    in_specs += [
        pl.BlockSpec((tm, d), lambda i: (i, 0)),
        pl.BlockSpec((1, 1, d), lambda i: (i // per, 0, 0)),
        pl.BlockSpec((1, d), lambda i: (0, 0)),
        pl.BlockSpec((1, d), lambda i: (0, 0)),
    ]
    return pl.pallas_call(
        functools.partial(_outproj_body, n_in=len(ys), alpha=alpha),
        grid=(n // tm,),
        in_specs=in_specs,
        out_specs=pl.BlockSpec((tm, d), lambda i: (i, 0)),
        out_shape=jax.ShapeDtypeStruct((n, d), F32),
        compiler_params=_cparams("arbitrary"),
        name="outproj_residual_ln",
    )(*ys, *ws, x, gate, g.reshape(1, d), b.reshape(1, d))


def _dft_tables(n):
    idx = np.arange(n)
    ang = 2.0 * np.pi * ((idx[:, None] * idx[None, :]) % n) / n
    return np.cos(ang), np.sin(ang)


def _fourier_rows_body(x_ref, cs_ref, tc_ref, ts_ref, o_ref, *, cb, width):
    nr = cs_ref.shape[1]
    for j in range(cb):
        pq = _dot(cs_ref[...], x_ref[0, :, j, :])
        yr = pq[:nr, :width] + pq[nr:, width:]
        yi = pq[:nr, width:] - pq[nr:, :width]
        tc, ts = tc_ref[j], ts_ref[j]
        o_ref[0, :, j, :width] = yr * tc + yi * ts
        o_ref[0, :, j, width:] = yi * tc - yr * ts


def _fourier_cols_body(y_ref, cs_ref, o_ref, *, kb, width, norm):
    nc = cs_ref.shape[1]
    for j in range(kb):
        pq = _dot(cs_ref[...], y_ref[0, j])
        o_ref[0, :, j, :] = (pq[:nc, :width] + pq[nc:, width:]) * norm


def _fourier_latent(f, batch, rows, width):
    cols = GRID_W
    seq = rows * cols
    c_r, s_r = _dft_tables(rows)
    c_c, s_c = _dft_tables(cols)
    k1 = np.arange(rows)[None, :]
    cc = np.arange(cols)[:, None]
    tw = 2.0 * np.pi * ((cc * k1) % seq) / seq
    cs_r = jnp.asarray(np.concatenate([c_r, s_r], 0), F32)
    cs_c = jnp.asarray(np.concatenate([c_c, s_c], 0), F32)
    tc = jnp.asarray(np.cos(tw)[:, :, None], F32)
    ts = jnp.asarray(np.sin(tw)[:, :, None], F32)
    cb = SUBLANES
    kb = SUBLANES
    lane_w = 2 * width
    y = pl.pallas_call(
        functools.partial(_fourier_rows_body, cb=cb, width=width),
        grid=(batch, cols // cb),
        in_specs=[
            pl.BlockSpec((1, rows, cb, lane_w), lambda b, j: (b, 0, j, 0)),
            pl.BlockSpec((2 * rows, rows), lambda b, j: (0, 0)),
            pl.BlockSpec((cb, rows, 1), lambda b, j: (j, 0, 0)),
            pl.BlockSpec((cb, rows, 1), lambda b, j: (j, 0, 0)),
        ],
        out_specs=pl.BlockSpec((1, rows, cb, lane_w), lambda b, j: (b, 0, j, 0)),
        out_shape=jax.ShapeDtypeStruct((batch, rows, cols, lane_w), F32),
        compiler_params=_cparams("arbitrary", "arbitrary"),
        name="fourier_rows",
    )(f.reshape(batch, rows, cols, lane_w), cs_r, tc, ts)
    z = pl.pallas_call(
        functools.partial(_fourier_cols_body, kb=kb, width=width, norm=float((seq * HEAD_DIM) ** -0.5)),
        grid=(batch, rows // kb),
        in_specs=[
            pl.BlockSpec((1, kb, cols, lane_w), lambda b, j: (b, j, 0, 0)),
            pl.BlockSpec((2 * cols, cols), lambda b, j: (0, 0)),
        ],
        out_specs=pl.BlockSpec((1, cols, kb, width), lambda b, j: (b, 0, j, 0)),
        out_shape=jax.ShapeDtypeStruct((batch, cols, rows, width), F32),
        compiler_params=_cparams("arbitrary", "arbitrary"),
        name="fourier_cols",
    )(y, cs_c)
    return z.reshape(batch * seq, width)


def _fourier_dense_body(x_ref, c_ref, s_ref, o_ref, *, width, norm):
    x = x_ref[...]
    o_ref[...] = (_dot(c_ref[...], x[:, :width]) + _dot(s_ref[...], x[:, width:])) * norm


def _fourier_dense(f, batch, length, width):
    c, s = _dft_tables(length)
    return pl.pallas_call(
        functools.partial(_fourier_dense_body, width=width, norm=float((length * HEAD_DIM) ** -0.5)),
        grid=(batch,),
        in_specs=[
            pl.BlockSpec((length, 2 * width), lambda b: (b, 0)),
            pl.BlockSpec((length, length), lambda b: (0, 0)),
            pl.BlockSpec((length, length), lambda b: (0, 0)),
        ],
        out_specs=pl.BlockSpec((length, width), lambda b: (b, 0)),
        out_shape=jax.ShapeDtypeStruct((batch * length, width), F32),
        compiler_params=_cparams("arbitrary"),
        name="fourier_dense",
    )(f, jnp.asarray(c, F32), jnp.asarray(s, F32))


def _natten_tables(rows, rpb):
    w = GRID_W
    qb, kw = NA_ROWS_PER_BLOCK, NA_WIN_ROWS
    nh, ndr, ndc = rpb.shape
    tq, tk = np.divmod(np.arange(w * w), w)
    dc_i = np.clip(tk - tq, -(NA_KC - 1), NA_KC - 1) + (NA_KC - 1)
    onehot = np.zeros((LANES, w * w), np.float32)
    onehot[dc_i, np.arange(w * w)] = 1.0
    c_start = np.clip(tq - NA_KC // 2, 0, w - NA_KC)
    col_ok = ((tk >= c_start) & (tk < c_start + NA_KC)).reshape(w, w)
    rp = jnp.zeros((LANES, LANES), F32).at[:nh * ndr, :ndc].set(rpb.reshape(nh * ndr, ndc).astype(F32))
    tiles = _matmul_f32(rp, jnp.asarray(onehot))[:nh * ndr].reshape(nh, ndr, w, w)
    tiles = jnp.where(jnp.asarray(col_ok), tiles, NEG)
    dr = np.arange(kw)[None, :] - np.arange(qb)[:, None] - (kw - qb) // 2
    dr_i = np.clip(dr + (NA_KR - 1), 0, ndr - 1)
    col_bias = jnp.transpose(tiles[:, dr_i], (0, 1, 3, 2, 4)).reshape(nh, qb * w, kw * w)
    qi = np.repeat(np.arange(qb), w)
    km = np.repeat(np.arange(kw), w)
    nblk = rows // qb
    masks = []
    for j in range(nblk):
        qr = j * qb + qi
        kr = j * qb - (kw - qb) // 2 + km
        r_start = np.clip(qr - NA_KR // 2, 0, rows - NA_KR)
        ok = (kr[None, :] >= r_start[:, None]) & (kr[None, :] < r_start[:, None] + NA_KR)
        masks.append(np.where(ok, 0.0, NEG).astype(np.float32))
    kinds = [(j > 0) + (j == nblk - 1) for j in range(nblk)]
    table = np.zeros((3,) + masks[0].shape, np.float32)
    for j, kd in enumerate(kinds):
        table[kd] = masks[j]
    for j, kd in enumerate(kinds):
        assert np.array_equal(table[kd], masks[j])
    half, quarter = qb * w // 2, kw * w // 4
    assert (table[:, :half, 3 * quarter:] == NEG).all() and (table[:, half:, :quarter] == NEG).all()
    return col_bias, jnp.asarray(table)


def _natten_body(q_ref, k0_ref, k1_ref, k2_ref, k3_ref, v0_ref, v1_ref, v2_ref, v3_ref,
                 kc_ref, vc_ref, cb_ref, rm_ref, o_ref):
    k_refs = (k0_ref, k1_ref, k2_ref, k3_ref)
    v_refs = (v0_ref, v1_ref, v2_ref, v3_ref)
    qt, kb = q_ref.shape[1], k0_ref.shape[1]
    hq = qt // 2
    outs = []
    for hh in range(q_ref.shape[0]):
        halves = []
        for part in range(2):
            rows = slice(part * hq, (part + 1) * hq)
            cols = slice(part * kb, (part + 3) * kb)
            q = q_ref[hh, rows, :]
            s = jnp.concatenate([_dot_nt(q, kr[hh]) for kr in k_refs[part:part + 3]], axis=1)
            s = s + cb_ref[hh, rows, cols] + rm_ref[0, rows, cols]
            sc = _dot_nt(q, kc_ref[hh])
            m = jnp.maximum(jnp.max(s, axis=-1, keepdims=True), jnp.max(sc, axis=-1, keepdims=True))
            p = jnp.exp(s - m)
            pc = jnp.exp(sc - m)
            l = jnp.sum(p, axis=-1, keepdims=True) + jnp.sum(pc, axis=-1, keepdims=True)
            o = _dot(pc.astype(BF16), vc_ref[hh])
            for c, vr in enumerate(v_refs[part:part + 3]):
                o = o + _dot(p[:, c * kb:(c + 1) * kb].astype(BF16), vr[hh])
            halves.append(o / l)
        outs.append(jnp.concatenate(halves, axis=0))
    o_ref[...] = jnp.concatenate(outs, axis=1).astype(o_ref.dtype)


def _natten(q, k, v, kc, vc, col_bias, row_mask, batch, rows, ctx_len):
    nh, n, dh = q.shape
    hp = 4
    qt = NA_ROWS_PER_BLOCK * GRID_W
    kt = NA_WIN_ROWS * GRID_W // 4
    nblk = rows // NA_ROWS_PER_BLOCK
    kblocks = rows * GRID_W // kt

    def kv_spec(c):
        return pl.BlockSpec(
            (hp, kt, dh),
            lambda b, j, h: (h, b * kblocks + jnp.clip(2 * j - 1 + c, 0, kblocks - 1), 0))

    in_specs = [pl.BlockSpec((hp, qt, dh), lambda b, j, h: (h, b * nblk + j, 0))]
    in_specs += [kv_spec(c) for c in range(4)] * 2
    in_specs += [
        pl.BlockSpec((hp, ctx_len, dh), lambda b, j, h: (h, b, 0)),
        pl.BlockSpec((hp, ctx_len, dh), lambda b, j, h: (h, b, 0)),
        pl.BlockSpec((hp, qt, 4 * kt), lambda b, j, h: (h, 0, 0)),
        pl.BlockSpec((1, qt, 4 * kt), lambda b, j, h: ((j > 0).astype(jnp.int32) + (j == nblk - 1).astype(jnp.int32), 0, 0)),
    ]
    return pl.pallas_call(
        _natten_body,
        grid=(batch, nblk, nh // hp),
        in_specs=in_specs,
        out_specs=pl.BlockSpec((qt, hp * dh), lambda b, j, h: (b * nblk + j, h)),
        out_shape=jax.ShapeDtypeStruct((n, nh * dh), BF16),
        compiler_params=_cparams("arbitrary", "arbitrary", "arbitrary"),
        name="neighbourhood_attention",
    )(q, k, k, k, k, v, v, v, v, kc, vc, col_bias, row_mask)


def _ctx_attn_body(q_ref, k_ref, v_ref, o_ref):
    outs = []
    for hh in range(q_ref.shape[0]):
        s = _dot_nt(q_ref[hh], k_ref[hh])
        m = jnp.max(s, axis=-1, keepdims=True)
        p = jnp.exp(s - m)
        l = jnp.sum(p, axis=-1, keepdims=True)
        outs.append(_dot(p.astype(BF16), v_ref[hh]) / l)
    o_ref[...] = jnp.concatenate(outs, axis=1).astype(o_ref.dtype)


def _ctx_attn(q, k, v, batch, ctx_len):
    nh, n, dh = q.shape
    hp = 2
    spec = pl.BlockSpec((hp, ctx_len, dh), lambda b, h: (h, b, 0))
    return pl.pallas_call(
        _ctx_attn_body,
        grid=(batch, nh // hp),
        in_specs=[spec, spec, spec],
        out_specs=pl.BlockSpec((ctx_len, hp * dh), lambda b, h: (b, h)),
        out_shape=jax.ShapeDtypeStruct((n, nh * dh), BF16),
        compiler_params=_cparams("arbitrary", "arbitrary"),
        name="context_attention",
    )(q, k, v)


def _gqa_body(sink_ref, q_ref, kp_ref, kc_ref, kn_ref, vp_ref, vc_ref, vn_ref, kx_ref, vx_ref, o_ref, *, nb):
    hkv, blk, dh = kc_ref.shape
    g = q_ref.shape[0] // hkv
    n = pl.program_id(1)
    shape = (g * blk, 3 * blk)
    qi = lax.broadcasted_iota(jnp.int32, shape, 0) % blk
    kp = lax.broadcasted_iota(jnp.int32, shape, 1) - blk
    ok = (jnp.abs(qi - kp) <= GQA_WINDOW) & ((kp >= 0) | (n > 0)) & ((kp < blk) | (n < nb - 1))
    grp = lax.broadcasted_iota(jnp.int32, (g * blk, 1), 0) // blk
    for kvh in range(hkv):
        q = q_ref[kvh * g:(kvh + 1) * g].reshape(g * blk, dh)
        s = jnp.concatenate([_dot_nt(q, r[kvh]) for r in (kp_ref, kc_ref, kn_ref)], axis=1)
        s = jnp.where(ok, s, NEG)
        sx = _dot_nt(q, kx_ref[kvh])
        sk = jnp.zeros((g * blk, 1), F32)
        for gi in range(g):
            sk = jnp.where(grp == gi, sink_ref[kvh * g + gi], sk)
        m = jnp.maximum(jnp.maximum(jnp.max(s, axis=-1, keepdims=True), jnp.max(sx, axis=-1, keepdims=True)), sk)
        p = jnp.exp(s - m)
        px = jnp.exp(sx - m)
        l = jnp.sum(p, axis=-1, keepdims=True) + jnp.sum(px, axis=-1, keepdims=True) + jnp.exp(sk - m)
        o = _dot(px.astype(BF16), vx_ref[kvh])
        for c, vr in enumerate((vp_ref, vc_ref, vn_ref)):
            o = o + _dot(p[:, c * blk:(c + 1) * blk].astype(BF16), vr[kvh])
        o = o / l
        o_ref[:, kvh * g * dh:(kvh + 1) * g * dh] = jnp.concatenate(
            [o[gi * blk:(gi + 1) * blk] for gi in range(g)], axis=1).astype(o_ref.dtype)


def _gqa(q, k, v, kx, vx, sink, batch, seq, ctx_len):
    hq, n, dh = q.shape
    hkv = k.shape[0]
    g = hq // hkv
    blk = GQA_BLOCK
    nb = seq // blk

    def kv_spec(c):
        return pl.BlockSpec((hkv, blk, dh), lambda b, i, s: (0, b * nb + jnp.clip(i - 1 + c, 0, nb - 1), 0))

    x_spec = pl.BlockSpec((hkv, ctx_len, dh), lambda b, i, s: (0, b, 0))
    grid_spec = pltpu.PrefetchScalarGridSpec(
        num_scalar_prefetch=1,
        grid=(batch, nb),
        in_specs=[pl.BlockSpec((hq, blk, dh), lambda b, i, s: (0, b * nb + i, 0))]
        + [kv_spec(c) for c in range(3)] * 2 + [x_spec, x_spec],
        out_specs=pl.BlockSpec((blk, hq * dh), lambda b, i, s: (b * nb + i, 0)),
    )
    return pl.pallas_call(
        functools.partial(_gqa_body, nb=nb),
        grid_spec=grid_spec,
        out_shape=jax.ShapeDtypeStruct((n, hq * dh), BF16),
        compiler_params=_cparams("arbitrary", "arbitrary"),
        name="window_gqa",
    )(sink, q, k, k, k, v, v, v, kx, vx)


def _oddeven_merge(lo, hi, r):
    step = r * 2
    if step < hi - lo:
        yield from _oddeven_merge(lo, hi, step)
        yield from _oddeven_merge(lo + r, hi, step)
        yield from [(i, i + r) for i in range(lo + r, hi - r, step)]
    else:
        yield (lo, lo + r)


def _oddeven_sort(lo, hi):
    if hi - lo >= 1:
        mid = lo + (hi - lo) // 2
        yield from _oddeven_sort(lo, mid)
        yield from _oddeven_sort(mid + 1, hi)
        yield from _oddeven_merge(lo, hi, 1)


def _exchange(t, i, j):
    t[i], t[j] = jnp.maximum(t[i], t[j]), jnp.minimum(t[i], t[j])


def _sort_tiles(tiles):
    t = list(tiles)
    for i, j in _oddeven_sort(0, PEER_TOPK - 1):
        if j < len(t):
            _exchange(t, i, j)
    return t


def _top_tiles(sorted_tiles):
    w = list(sorted_tiles)
    n = len(w)
    shift = SUBLANES // 2
    while shift:
        other = [pltpu.roll(x, shift, 0) for x in w]
        w = [jnp.maximum(w[k], other[n - 1 - k]) for k in range(n)]
        d = n // 2
        while d:
            for k in range(n):
                if not k & d:
                    _exchange(w, k, k + d)
            d //= 2
        shift //= 2
    return w


def _next_below(tiles, bound):
    m = None
    for t in tiles:
        v = jnp.where(t < bound, t, -jnp.inf)
        m = v if m is None else jnp.maximum(m, v)
    return jnp.max(m, axis=0, keepdims=True)


def _stack_sublanes(vals, sub):
    out = vals[0]
    for j in range(1, len(vals)):
        out = jnp.where(sub == j, vals[j], out)
    return out


def _bf16_rounded(v):
    return v.astype(BF16).astype(F32)


def _peer_route_body(x_ref, sh_ref, sc_ref, wq_ref, keys_ref, xm_ref, kap_ref, e1_ref, p2_ref):
    xm_ref, kap_ref, e1_ref, p2_ref = (r.at[0] for r in (xm_ref, kap_ref, e1_ref, p2_ref))
    nheads = p2_ref.shape[0]
    nk = keys_ref.shape[1]
    h = _ln(x_ref[...]) * (1.0 + sc_ref[0]) + sh_ref[0]
    ht = h.T.astype(BF16)
    xm_ref[...] = ht
    qt = _dot(wq_ref[...], ht).astype(BF16)
    qd = keys_ref.shape[2]
    k16 = PEER_TOPK
    sub = lax.broadcasted_iota(jnp.int32, (SUBLANES, 1), 0)
    for hd in range(nheads):
        halves = []
        for half in range(2):
            hp = hd * 2 + half
            s = _dot(keys_ref[hp], qt[hp * qd:(hp + 1) * qd, :])
            tiles = [s[SUBLANES * k:SUBLANES * (k + 1)] for k in range(nk // SUBLANES)]
            top = _top_tiles(_sort_tiles(tiles))
            halves.append((s, top, _next_below(tiles, top[-1])))
        (s1, t1, t1_17), (s2, t2, t2_17) = halves
        m1, m2 = t1[0][:1], t2[0][:1]
        cmax = m1 + m2
        t2_lo, t2_hi = _stack_sublanes(t2[:SUBLANES], sub), _stack_sublanes(t2[SUBLANES:], sub)
        t1_hi = _stack_sublanes(t1[SUBLANES:], sub)
        pairs = [(t1[0], t2_lo), (t1[0], t2_hi)]
        pairs += [(t1[a], jnp.where(sub < (k16 + 1) // (a + 1), t2_lo, -jnp.inf)) for a in range(1, SUBLANES)]
        pairs += [(t1_hi, t2[0])]
        pairs += [(jnp.where(sub == 0, t1[0], t1_17), jnp.where(sub == 0, t2_17, jnp.where(sub == 1, t2[0], -jnp.inf)))]
        cands = [a + b for a, b in pairs]
        ctop = _top_tiles(_sort_tiles(cands) + [jnp.full_like(cands[0], -jnp.inf)] * (k16 - len(cands)))
        c16 = ctop[-1][:1]
        tau = 0.5 * (c16 + _next_below(cands, c16))
        z = None
        for (a, b), c in zip(pairs, cands):
            picked = _bf16_rounded(jnp.exp(b - m2)) >= _bf16_rounded(jnp.exp(tau - a - m2))
            zc = jnp.where(picked, jnp.exp(c - cmax), 0.0)
            z = zc if z is None else z + zc
        z = jnp.sum(z, axis=0, keepdims=True)
        groups = (nk // SUBLANES, SUBLANES, s1.shape[1])
        kap_ref[hd] = jnp.exp(tau - s1 - m2).reshape(groups)
        e1_ref[hd] = (jnp.exp(s1 - m1) * (0.5 / z)).reshape(groups)
        p2_ref[hd] = jnp.exp(s2 - m2).astype(BF16)


def _peer_route(x, shift, scale, wq_t, keys, tokens_per_batch):
    n, d = x.shape
    tt = PEER_COLS
    per = tokens_per_batch // tt
    nh = keys.shape[0] // 2
    nk = keys.shape[1]
    big = pl.BlockSpec((1, nh, nk, tt), lambda i: (i, 0, 0, 0))
    rows = pl.BlockSpec((1, nh, nk // SUBLANES, SUBLANES, tt), lambda i: (i, 0, 0, 0, 0))
    rows_shape = jax.ShapeDtypeStruct((n // tt, nh, nk // SUBLANES, SUBLANES, tt), F32)
    return pl.pallas_call(
        _peer_route_body,
        grid=(n // tt,),
        in_specs=[
            pl.BlockSpec((tt, d), lambda i: (i, 0)),
            pl.BlockSpec((1, 1, d), lambda i: (i // per, 0, 0)),
            pl.BlockSpec((1, 1, d), lambda i: (i // per, 0, 0)),
            pl.BlockSpec(wq_t.shape, lambda i: (0, 0)),
            pl.BlockSpec(keys.shape, lambda i: (0, 0, 0)),
        ],
        out_specs=[pl.BlockSpec((1, d, tt), lambda i: (i, 0, 0)), rows, rows, big],
        out_shape=[jax.ShapeDtypeStruct((n // tt, d, tt), BF16), rows_shape, rows_shape,
                   jax.ShapeDtypeStruct((n // tt, nh, nk, tt), BF16)],
        compiler_params=_cparams("arbitrary"),
        name="peer_route",
    )(x, shift, scale, wq_t, keys)


PEER_UNIT_ROWS = 4


def _peer_expert_body(xm_ref, u_ref, vt_ref, kap_ref, e1_ref, p2_ref,
                      x_ref, gate_ref, g_ref, b_ref, o_ref, acc_ref, a_ref, w_ref, *, alpha):
    e = pl.program_id(1)
    ncb, nheads, ngroups, _, cw = kap_ref.shape
    nk = p2_ref.shape[2]
    ur = PEER_UNIT_ROWS
    ue = ur * nk
    nrp = ngroups * SUBLANES // ur
    n_units = ncb * nrp
    assert 2 * ur == SUBLANES and nrp % 2 == 0

    @pl.when(e == 0)
    def _():
        acc_ref[...] = jnp.zeros_like(acc_ref)

    def first_matmul(i, slot):
        rows = pl.ds(pl.multiple_of((i % nrp) * ue, ue), ue)
        a_ref[slot] = _dot(u_ref[rows, :], xm_ref[i // nrp])

    def second_matmul(i, slot):
        acc_ref[i // nrp] += _dot(vt_ref[i % nrp], w_ref[slot])

    def gate_and_activate(i, slot):
        c, grp = i // nrp, (i % nrp) // 2
        for r in range(ur):
            row = slot * ur + r
            gsum = None
            for hd in range(nheads):
                p2 = p2_ref[c, hd]
                keep = p2 >= kap_ref[c, hd, grp, row:row + 1, :].astype(BF16)
                gate = p2 * e1_ref[c, hd, grp, row:row + 1, :].astype(BF16)
                term = jnp.where(keep, gate, jnp.zeros_like(p2))
                gsum = term if gsum is None else gsum + term
            a = a_ref[slot, r * nk:(r + 1) * nk, :]
            act = (a * (1.0 + lax.erf(a * (2.0 ** -0.5)))).astype(BF16)
            w_ref[slot, r * nk:(r + 1) * nk, :] = gsum * act

    def steady(j, carry):
        i = 2 * j + 1
        first_matmul(i + 1, 0)
        gate_and_activate(i, 1)
        second_matmul(i - 1, 0)
        first_matmul(i + 2, 1)
        gate_and_activate(i + 1, 0)
        second_matmul(i, 1)
        return carry

    assert n_units % 2 == 0
    first_matmul(0, 0)
    first_matmul(1, 1)
    gate_and_activate(0, 0)
    lax.fori_loop(0, n_units // 2 - 1, steady, 0)
    gate_and_activate(n_units - 1, 1)
    second_matmul(n_units - 2, 0)
    second_matmul(n_units - 1, 1)

    @pl.when(e == pl.num_programs(1) - 1)
    def _():
        for c in range(ncb):
            f = acc_ref[c].T
            z = alpha * x_ref[c * cw:(c + 1) * cw, :] + gate_ref[0] * f
            o_ref[c * cw:(c + 1) * cw, :] = _ln(z) * g_ref[...] + b_ref[...]


def _peer_experts(xm_t, u, v_t, kap, e1, p2, x, gate, g, b, tokens_per_batch, alpha, tt=1024, et=2048):
    n, d = x.shape
    tt = min(tt, tokens_per_batch)
    per = tokens_per_batch // tt
    _, nh, nk, cw = p2.shape
    ncb = tt // cw
    n1 = et // nk
    ue = PEER_UNIT_ROWS * nk
    sel = pl.BlockSpec((ncb, nh, n1 // SUBLANES, SUBLANES, cw), lambda i, e: (i, 0, e, 0, 0))
    full = pl.BlockSpec((ncb, nh, nk, cw), lambda i, e: (i, 0, 0, 0))
    return pl.pallas_call(
        functools.partial(_peer_expert_body, alpha=alpha),
        grid=(n // tt, u.shape[0] // et),
        in_specs=[
            pl.BlockSpec((ncb, d, cw), lambda i, e: (i, 0, 0)),
            pl.BlockSpec((et, d), lambda i, e: (e, 0)),
            pl.BlockSpec((et // ue, d, ue), lambda i, e: (e, 0, 0)),
            sel, sel, full,
            pl.BlockSpec((tt, d), lambda i, e: (i, 0)),
            pl.BlockSpec((1, 1, d), lambda i, e: (i // per, 0, 0)),
            pl.BlockSpec((1, d), lambda i, e: (0, 0)),
            pl.BlockSpec((1, d), lambda i, e: (0, 0)),
        ],
        out_specs=pl.BlockSpec((tt, d), lambda i, e: (i, 0)),
        out_shape=jax.ShapeDtypeStruct((n, d), F32),
        scratch_shapes=[
            pltpu.VMEM((ncb, d, cw), F32),
            pltpu.VMEM((2, ue, cw), F32),
            pltpu.VMEM((2, ue, cw), BF16),
        ],
        compiler_params=_cparams("arbitrary", "arbitrary"),
        name="peer_experts",
    )(xm_t, u, v_t, kap, e1, p2, x, gate, g.reshape(1, d), b.reshape(1, d))


def _peer_layer(x, shift, scale, gate, g, b, tables, tokens_per_batch, alpha):
    wq_t, keys, u, v_t = tables
    xm_t, kap, e1, p2 = _peer_route(x, shift, scale, wq_t, keys, tokens_per_batch)
    return _peer_experts(xm_t, u, v_t, kap, e1, p2, x, gate, g, b, tokens_per_batch, alpha)


def _rope_tables(seq):
    t = jnp.arange(seq)
    row = (t // GRID_W).astype(F32)
    col = (t % GRID_W).astype(F32)
    n_freq = HEAD_DIM // 4
    inv_freq = ROPE_THETA ** (-jnp.arange(n_freq, dtype=F32) / n_freq)
    ang = jnp.concatenate([row[:, None] * inv_freq, col[:, None] * inv_freq], -1)
    cos, sin = jnp.cos(ang), jnp.sin(ang)
    reps = LANES // (HEAD_DIM // 2)
    return jnp.tile(cos, (1, reps)), jnp.tile(sin, (1, reps))


def _rotate_half_columns(w, n_heads):
    d = w.shape[0]
    wh = w.reshape(d, n_heads, 2, HEAD_DIM // 2)
    return jnp.concatenate([-wh[:, :, 1], wh[:, :, 0]], axis=-1).reshape(d, n_heads * HEAD_DIM)


def _expert_table_body(u_ref, v_ref, ub_ref, vt_ref):
    ub_ref[...] = u_ref[0].astype(BF16)
    vt_ref[0] = v_ref[0].T.astype(BF16)


def _peer_tables(w_q, sub_keys, u_all, v_all, layer):
    nh, _, nk, qd = sub_keys.shape
    _, ne, d = u_all.shape
    ue = PEER_UNIT_ROWS * nk
    u_b, v_chunks = pl.pallas_call(
        _expert_table_body,
        grid=(ne // ue,),
        in_specs=[pl.BlockSpec((1, ue, d), lambda i: (layer, i, 0)), pl.BlockSpec((1, ue, d), lambda i: (layer, i, 0))],
        out_specs=[pl.BlockSpec((ue, d), lambda i: (i, 0)), pl.BlockSpec((1, d, ue), lambda i: (i, 0, 0))],
        out_shape=[jax.ShapeDtypeStruct((ne, d), BF16), jax.ShapeDtypeStruct((ne // ue, d, ue), BF16)],
        compiler_params=_cparams("arbitrary"),
        name="expert_table_layout",
    )(u_all, v_all)
    return (w_q.T.astype(BF16), sub_keys.reshape(nh * 2, nk, qd).astype(BF16), u_b, v_chunks)


def kernel(x, c, ctx, c_ctx, ada_w, ada_b, post_ln_g, post_ln_b, even_w_in, even_w_out, na_rpb,
           odd_w_in, odd_w_out, gqa_sink, peer_w_q, peer_sub_keys, peer_u, peer_v):
    batch, seq, d = x.shape
    ctx_len = ctx.shape[1]
    depth = ada_w.shape[0]
    rows = seq // GRID_W
    alpha = float((2 * depth) ** 0.25)
    fw = FNET_GROUPS * HEAD_DIM
    nw = NA_HEADS * HEAD_DIM
    qw = GQA_Q_HEADS * HEAD_DIM
    kvw = GQA_KV_HEADS * HEAD_DIM
    qscale = HEAD_DIM ** -0.5

    cond = jnp.zeros((8, d), F32).at[:batch].set(c).at[batch].set(c_ctx)
    mods = _ada(cond, ada_w, ada_b)

    xl = x.reshape(batch * seq, d)
    hc = ctx.reshape(batch * ctx_len, d)
    cos_t, sin_t = _rope_tables(seq)

    cg, sg = _dft_tables(HEAD_DIM)
    eye = np.eye(FNET_GROUPS)
    chan = jnp.asarray(np.concatenate([np.kron(eye, cg), -np.kron(eye, sg)], axis=1), F32)

    for layer in range(depth):
        ctx_out = layer < depth - 1
        i = layer // 2
        m_l = [m.reshape(batch, 1, d) for m in jnp.split(mods[layer, :batch], 6, axis=-1)]
        m_c = [jnp.broadcast_to(m.reshape(1, 1, d), (batch, 1, d)) for m in jnp.split(mods[layer, batch], 6, axis=-1)]
        g0, b0 = post_ln_g[layer, 0], post_ln_b[layer, 0]
        g1, b1 = post_ln_g[layer, 1], post_ln_b[layer, 1]

        if layer % 2 == 0:
            w_in, w_out = even_w_in[i], even_w_out[i]
            w_f = _matmul_f32(w_in[:, :fw], chan)
            w_aug = jnp.concatenate([w_f, w_in[:, fw:]], axis=1).astype(BF16)
            plan = (("nat", 0, 2 * fw, 1.0, None),
                    ("heads", 2 * fw, nw, qscale, None),
                    ("heads", 2 * fw + nw, nw, 1.0, None),
                    ("heads", 2 * fw + 2 * nw, nw, 1.0, None))
            f_l, q_l, k_l, v_l = _proj(xl, m_l[0], m_l[1], w_aug, plan, seq)
            f_c, q_c, k_c, v_c = _proj(hc, m_c[0], m_c[1], w_aug, plan, ctx_len)
            col_bias, row_mask = _natten_tables(rows, na_rpb[i])
            na_l = _natten(q_l, k_l, v_l, k_c, v_c, col_bias, row_mask, batch, rows, ctx_len)
            fm_l = _fourier_latent(f_l, batch, rows, fw)
            w_out_b = w_out.astype(BF16)
            ws = [w_out_b[:fw], w_out_b[fw:]]
            xl_new = _outproj_ln([fm_l, na_l], ws, xl, m_l[2], g0, b0, seq, alpha)
            if ctx_out:
                na_c = _ctx_attn(q_c, k_c, v_c, batch, ctx_len)
                fm_c = _fourier_dense(f_c, batch, ctx_len, fw)
                hc_new = _outproj_ln([fm_c, na_c], ws, hc, m_c[2], g0, b0, ctx_len, alpha)
        else:
            w_in, w_out = odd_w_in[i], odd_w_out[i]
            wq, wk, wv = w_in[:, :qw], w_in[:, qw:qw + kvw], w_in[:, qw + kvw:]
            w_aug = jnp.concatenate([wq, wk, wv, _rotate_half_columns(wq, GQA_Q_HEADS),
                                     _rotate_half_columns(wk, GQA_KV_HEADS)], axis=1).astype(BF16)
            plan = (("heads", 0, qw, qscale, qw + 2 * kvw),
                    ("heads", qw, kvw, 1.0, 2 * qw + 2 * kvw),
                    ("heads", qw + kvw, kvw, 1.0, None))
            q_l, k_l, v_l = _proj(xl, m_l[0], m_l[1], w_aug, plan, seq, rope=(cos_t, sin_t))
            plan_c = (("heads", 0, kvw, 1.0, None), ("heads", kvw, kvw, 1.0, None))
            if ctx_out:
                raise NotImplementedError("an odd layer must be the last layer (no context output path)")
            k_c, v_c = _proj(hc, m_c[0], m_c[1], w_in[:, qw:].astype(BF16), plan_c, ctx_len)
            y_l = _gqa(q_l, k_l, v_l, k_c, v_c, gqa_sink[i], batch, seq, ctx_len)
            xl_new = _outproj_ln([y_l], [w_out.astype(BF16)], xl, m_l[2], g0, b0, seq, alpha)

        tables = _peer_tables(peer_w_q[layer], peer_sub_keys[layer], peer_u, peer_v, layer)
        xl = _peer_layer(xl_new, m_l[3], m_l[4], m_l[5], g1, b1, tables, seq, alpha)
        if ctx_out:
            hc = _peer_layer(hc_new, m_c[3], m_c[4], m_c[5], g1, b1, tables, ctx_len, alpha)

    return xl.reshape(batch, seq, d)
```

```python
import functools
import math

import numpy as np
import jax
import jax.numpy as jnp
from jax import lax
from jax.experimental import pallas as pl
from jax.experimental.pallas import tpu as pltpu

F32 = jnp.float32
BF16 = jnp.bfloat16

HEAD_DIM = 64
GRID_W = 64
FNET_GROUPS = 8
NA_HEADS = 8
NA_KR = 8
NA_KC = 16
NA_ROWS_PER_BLOCK = 8
NA_WIN_ROWS = 16
GQA_Q_HEADS = 16
GQA_KV_HEADS = 4
GQA_WINDOW = 128
GQA_BLOCK = 128
ROPE_THETA = 10000.0
PEER_HEADS = 8
PEER_NKEYS = 128
PEER_TOPK = 16
LN_EPS = 1e-6
NEG = -1e30

PEER_COLS = 256
LANES = 128
SUBLANES = 8
VMEM_LIMIT = 56 * 1024 * 1024


def _cparams(*sem):
    return pltpu.CompilerParams(dimension_semantics=sem, vmem_limit_bytes=VMEM_LIMIT)


def _ln(x):
    mu = jnp.mean(x, axis=-1, keepdims=True)
    xc = x - mu
    var = jnp.mean(xc * xc, axis=-1, keepdims=True)
    return xc * lax.rsqrt(var + LN_EPS)


def _dot(a, b):
    return jnp.dot(a, b, preferred_element_type=F32)


def _dot_nt(a, b):
    return lax.dot_general(a, b, (((1,), (1,)), ((), ())), preferred_element_type=F32)


def _ada_body(c_ref, w_ref, b_ref, o_ref):
    c = c_ref[...]
    o_ref[0] = _dot(c * jax.nn.sigmoid(c), w_ref[0]) + b_ref[0]


def _ada(cond, ada_w, ada_b):
    depth, d, n = ada_w.shape
    tn = 1536
    return pl.pallas_call(
        _ada_body,
        grid=(depth, n // tn),
        in_specs=[
            pl.BlockSpec((8, d), lambda l, j: (0, 0)),
            pl.BlockSpec((1, d, tn), lambda l, j: (l, 0, j)),
            pl.BlockSpec((1, 1, tn), lambda l, j: (l, 0, j)),
        ],
        out_specs=pl.BlockSpec((1, 8, tn), lambda l, j: (l, 0, j)),
        out_shape=jax.ShapeDtypeStruct((depth, 8, n), F32),
        compiler_params=_cparams("arbitrary", "arbitrary"),
        name="ada_modulation",
    )(cond, ada_w, ada_b.reshape(depth, 1, n))


def _matmul_f32_body(a_ref, b_ref, o_ref):
    o_ref[...] = jnp.dot(a_ref[...], b_ref[...], preferred_element_type=F32, precision=lax.Precision.HIGHEST)


def _matmul_f32(a, b):
    m, k = a.shape
    n = b.shape[1]
    tm = min(256, m)
    return pl.pallas_call(
        _matmul_f32_body,
        grid=(m // tm,),
        in_specs=[pl.BlockSpec((tm, k), lambda i: (i, 0)), pl.BlockSpec((k, n), lambda i: (0, 0))],
        out_specs=pl.BlockSpec((tm, n), lambda i: (i, 0)),
        out_shape=jax.ShapeDtypeStruct((m, n), F32),
        compiler_params=_cparams("arbitrary"),
        name="small_matmul_f32",
    )(a, b)


def _proj_body(*refs, plan, use_rope):
    x_ref, sh_ref, sc_ref, w_ref = refs[:4]
    rest = refs[4:]
    if use_rope:
        cos_ref, sin_ref = rest[:2]
        rest = rest[2:]
    h = _ln(x_ref[...]) * (1.0 + sc_ref[0]) + sh_ref[0]
    acc = _dot(h.astype(BF16), w_ref[...])
    for o_ref, (kind, start, width, scale, rot_start) in zip(rest, plan):
        y = acc[:, start:start + width]
        if rot_start is not None:
            reps = width // LANES
            cos = jnp.tile(cos_ref[...], (1, reps))
            sin = jnp.tile(sin_ref[...], (1, reps))
            y = y * cos + acc[:, rot_start:rot_start + width] * sin
        if scale != 1.0:
            y = y * scale
        if kind == "nat":
            o_ref[...] = y.astype(o_ref.dtype)
        else:
            for hh in range(width // HEAD_DIM):
                o_ref[hh] = y[:, hh * HEAD_DIM:(hh + 1) * HEAD_DIM].astype(o_ref.dtype)


def _proj(x, shift, scale, w, plan, tokens_per_batch, rope=None, tm=512):
    n, d = x.shape
    tm = min(tm, tokens_per_batch)
    per = tokens_per_batch // tm
    in_specs = [
        pl.BlockSpec((tm, d), lambda i: (i, 0)),
        pl.BlockSpec((1, 1, d), lambda i: (i // per, 0, 0)),
        pl.BlockSpec((1, 1, d), lambda i: (i // per, 0, 0)),
        pl.BlockSpec(w.shape, lambda i: (0, 0)),
    ]
    args = [x, shift, scale, w]
    if rope is not None:
        in_specs += [pl.BlockSpec((tm, LANES), lambda i: (i % per, 0))] * 2
        args += list(rope)
    out_specs, out_shape = [], []
    for kind, start, width, sc, rot in plan:
        if kind == "nat":
            out_specs.append(pl.BlockSpec((tm, width), lambda i: (i, 0)))
            out_shape.append(jax.ShapeDtypeStruct((n, width), F32))
        else:
            nh = width // HEAD_DIM
            out_specs.append(pl.BlockSpec((nh, tm, HEAD_DIM), lambda i: (0, i, 0)))
            out_shape.append(jax.ShapeDtypeStruct((nh, n, HEAD_DIM), BF16))
    return pl.pallas_call(
        functools.partial(_proj_body, plan=plan, use_rope=rope is not None),
        grid=(n // tm,),
        in_specs=in_specs,
        out_specs=out_specs,
        out_shape=out_shape,
        compiler_params=_cparams("arbitrary"),
        name="modln_proj",
    )(*args)


def _outproj_body(*refs, n_in, alpha):
    ys, ws = refs[:n_in], refs[n_in:2 * n_in]
    x_ref, gate_ref, g_ref, b_ref, o_ref = refs[2 * n_in:]
    acc = None
    for y_ref, w_ref in zip(ys, ws):
        t = _dot(y_ref[...].astype(BF16), w_ref[...])
        acc = t if acc is None else acc + t
    z = alpha * x_ref[...] + gate_ref[0] * acc
    o_ref[...] = _ln(z) * g_ref[...] + b_ref[...]


def _outproj_ln(ys, ws, x, gate, g, b, tokens_per_batch, alpha, tm=512):
    n, d = x.shape
    tm = min(tm, tokens_per_batch)
    per = tokens_per_batch // tm
    in_specs = [pl.BlockSpec((tm, y.shape[1]), lambda i: (i, 0)) for y in ys]
    in_specs += [pl.BlockSpec(w.shape, lambda i: (0, 0)) for w in ws]
    in_specs += [
        pl.BlockSpec((tm, d), lambda i: (i, 0)),
        pl.BlockSpec((1, 1, d), lambda i: (i // per, 0, 0)),
        pl.BlockSpec((1, d), lambda i: (0, 0)),
        pl.BlockSpec((1, d), lambda i: (0, 0)),
    ]
    return pl.pallas_call(
        functools.partial(_outproj_body, n_in=len(ys), alpha=alpha),
        grid=(n // tm,),
        in_specs=in_specs,
        out_specs=pl.BlockSpec((tm, d), lambda i: (i, 0)),
        out_shape=jax.ShapeDtypeStruct((n, d), F32),
        compiler_params=_cparams("arbitrary"),
        name="outproj_residual_ln",
    )(*ys, *ws, x, gate, g.reshape(1, d), b.reshape(1, d))


def _dft_tables(n):
    idx = np.arange(n)
    ang = 2.0 * np.pi * ((idx[:, None] * idx[None, :]) % n) / n
    return np.cos(ang), np.sin(ang)


def _fourier_rows_body(x_ref, cs_ref, tc_ref, ts_ref, o_ref, *, cb, width):
    nr = cs_ref.shape[1]
    for j in range(cb):
        pq = _dot(cs_ref[...], x_ref[0, :, j, :])
        yr = pq[:nr, :width] + pq[nr:, width:]
        yi = pq[:nr, width:] - pq[nr:, :width]
        tc, ts = tc_ref[j], ts_ref[j]
        o_ref[0, :, j, :width] = yr * tc + yi * ts
        o_ref[0, :, j, width:] = yi * tc - yr * ts


def _fourier_cols_body(y_ref, cs_ref, o_ref, *, kb, width, norm):
    nc = cs_ref.shape[1]
    for j in range(kb):
        pq = _dot(cs_ref[...], y_ref[0, j])
        o_ref[0, :, j, :] = (pq[:nc, :width] + pq[nc:, width:]) * norm


def _fourier_latent(f, batch, rows, width):
    cols = GRID_W
    seq = rows * cols
    c_r, s_r = _dft_tables(rows)
    c_c, s_c = _dft_tables(cols)
    k1 = np.arange(rows)[None, :]
    cc = np.arange(cols)[:, None]
    tw = 2.0 * np.pi * ((cc * k1) % seq) / seq
    cs_r = jnp.asarray(np.concatenate([c_r, s_r], 0), F32)
    cs_c = jnp.asarray(np.concatenate([c_c, s_c], 0), F32)
    tc = jnp.asarray(np.cos(tw)[:, :, None], F32)
    ts = jnp.asarray(np.sin(tw)[:, :, None], F32)
    cb = SUBLANES
    kb = SUBLANES
    lane_w = 2 * width
    y = pl.pallas_call(
        functools.partial(_fourier_rows_body, cb=cb, width=width),
        grid=(batch, cols // cb),
        in_specs=[
            pl.BlockSpec((1, rows, cb, lane_w), lambda b, j: (b, 0, j, 0)),
            pl.BlockSpec((2 * rows, rows), lambda b, j: (0, 0)),
            pl.BlockSpec((cb, rows, 1), lambda b, j: (j, 0, 0)),
            pl.BlockSpec((cb, rows, 1), lambda b, j: (j, 0, 0)),
        ],
        out_specs=pl.BlockSpec((1, rows, cb, lane_w), lambda b, j: (b, 0, j, 0)),
        out_shape=jax.ShapeDtypeStruct((batch, rows, cols, lane_w), F32),
        compiler_params=_cparams("arbitrary", "arbitrary"),
        name="fourier_rows",
    )(f.reshape(batch, rows, cols, lane_w), cs_r, tc, ts)
    z = pl.pallas_call(
        functools.partial(_fourier_cols_body, kb=kb, width=width, norm=float((seq * HEAD_DIM) ** -0.5)),
        grid=(batch, rows // kb),
        in_specs=[
            pl.BlockSpec((1, kb, cols, lane_w), lambda b, j: (b, j, 0, 0)),
            pl.BlockSpec((2 * cols, cols), lambda b, j: (0, 0)),
        ],
        out_specs=pl.BlockSpec((1, cols, kb, width), lambda b, j: (b, 0, j, 0)),
        out_shape=jax.ShapeDtypeStruct((batch, cols, rows, width), F32),
        compiler_params=_cparams("arbitrary", "arbitrary"),
        name="fourier_cols",
    )(y, cs_c)
    return z.reshape(batch * seq, width)


def _fourier_dense_body(x_ref, c_ref, s_ref, o_ref, *, width, norm):
    x = x_ref[...]
    o_ref[...] = (_dot(c_ref[...], x[:, :width]) + _dot(s_ref[...], x[:, width:])) * norm


def _fourier_dense(f, batch, length, width):
    c, s = _dft_tables(length)
    return pl.pallas_call(
        functools.partial(_fourier_dense_body, width=width, norm=float((length * HEAD_DIM) ** -0.5)),
        grid=(batch,),
        in_specs=[
            pl.BlockSpec((length, 2 * width), lambda b: (b, 0)),
            pl.BlockSpec((length, length), lambda b: (0, 0)),
            pl.BlockSpec((length, length), lambda b: (0, 0)),
        ],
        out_specs=pl.BlockSpec((length, width), lambda b: (b, 0)),
        out_shape=jax.ShapeDtypeStruct((batch * length, width), F32),
        compiler_params=_cparams("arbitrary"),
        name="fourier_dense",
    )(f, jnp.asarray(c, F32), jnp.asarray(s, F32))


def _natten_tables(rows, rpb):
    w = GRID_W
    qb, kw = NA_ROWS_PER_BLOCK, NA_WIN_ROWS
    nh, ndr, ndc = rpb.shape
    tq, tk = np.divmod(np.arange(w * w), w)
    dc_i = np.clip(tk - tq, -(NA_KC - 1), NA_KC - 1) + (NA_KC - 1)
    onehot = np.zeros((LANES, w * w), np.float32)
    onehot[dc_i, np.arange(w * w)] = 1.0
    c_start = np.clip(tq - NA_KC // 2, 0, w - NA_KC)
    col_ok = ((tk >= c_start) & (tk < c_start + NA_KC)).reshape(w, w)
    rp = jnp.zeros((LANES, LANES), F32).at[:nh * ndr, :ndc].set(rpb.reshape(nh * ndr, ndc).astype(F32))
    tiles = _matmul_f32(rp, jnp.asarray(onehot))[:nh * ndr].reshape(nh, ndr, w, w)
    tiles = jnp.where(jnp.asarray(col_ok), tiles, NEG)
    dr = np.arange(kw)[None, :] - np.arange(qb)[:, None] - (kw - qb) // 2
    dr_i = np.clip(dr + (NA_KR - 1), 0, ndr - 1)
    col_bias = jnp.transpose(tiles[:, dr_i], (0, 1, 3, 2, 4)).reshape(nh, qb * w, kw * w)
    qi = np.repeat(np.arange(qb), w)
    km = np.repeat(np.arange(kw), w)
    nblk = rows // qb
    masks = []
    for j in range(nblk):
        qr = j * qb + qi
        kr = j * qb - (kw - qb) // 2 + km
        r_start = np.clip(qr - NA_KR // 2, 0, rows - NA_KR)
        ok = (kr[None, :] >= r_start[:, None]) & (kr[None, :] < r_start[:, None] + NA_KR)
        masks.append(np.where(ok, 0.0, NEG).astype(np.float32))
    kinds = [(j > 0) + (j == nblk - 1) for j in range(nblk)]
    table = np.zeros((3,) + masks[0].shape, np.float32)
    for j, kd in enumerate(kinds):
        table[kd] = masks[j]
    for j, kd in enumerate(kinds):
        assert np.array_equal(table[kd], masks[j])
    half, quarter = qb * w // 2, kw * w // 4
    assert (table[:, :half, 3 * quarter:] == NEG).all() and (table[:, half:, :quarter] == NEG).all()
    return col_bias, jnp.asarray(table)


def _natten_body(q_ref, k0_ref, k1_ref, k2_ref, k3_ref, v0_ref, v1_ref, v2_ref, v3_ref,
                 kc_ref, vc_ref, cb_ref, rm_ref, o_ref):
    k_refs = (k0_ref, k1_ref, k2_ref, k3_ref)
    v_refs = (v0_ref, v1_ref, v2_ref, v3_ref)
    qt, kb = q_ref.shape[1], k0_ref.shape[1]
    hq = qt // 2
    outs = []
    for hh in range(q_ref.shape[0]):
        halves = []
        for part in range(2):
            rows = slice(part * hq, (part + 1) * hq)
            cols = slice(part * kb, (part + 3) * kb)
            q = q_ref[hh, rows, :]
            s = jnp.concatenate([_dot_nt(q, kr[hh]) for kr in k_refs[part:part + 3]], axis=1)
            s = s + cb_ref[hh, rows, cols] + rm_ref[0, rows, cols]
            sc = _dot_nt(q, kc_ref[hh])
            m = jnp.maximum(jnp.max(s, axis=-1, keepdims=True), jnp.max(sc, axis=-1, keepdims=True))
            p = jnp.exp(s - m)
            pc = jnp.exp(sc - m)
            l = jnp.sum(p, axis=-1, keepdims=True) + jnp.sum(pc, axis=-1, keepdims=True)
            o = _dot(pc.astype(BF16), vc_ref[hh])
            for c, vr in enumerate(v_refs[part:part + 3]):
                o = o + _dot(p[:, c * kb:(c + 1) * kb].astype(BF16), vr[hh])
            halves.append(o / l)
        outs.append(jnp.concatenate(halves, axis=0))
    o_ref[...] = jnp.concatenate(outs, axis=1).astype(o_ref.dtype)


def _natten(q, k, v, kc, vc, col_bias, row_mask, batch, rows, ctx_len):
    nh, n, dh = q.shape
    hp = 4
    qt = NA_ROWS_PER_BLOCK * GRID_W
    kt = NA_WIN_ROWS * GRID_W // 4
    nblk = rows // NA_ROWS_PER_BLOCK
    kblocks = rows * GRID_W // kt

    def kv_spec(c):
        return pl.BlockSpec(
            (hp, kt, dh),
            lambda b, j, h: (h, b * kblocks + jnp.clip(2 * j - 1 + c, 0, kblocks - 1), 0))

    in_specs = [pl.BlockSpec((hp, qt, dh), lambda b, j, h: (h, b * nblk + j, 0))]
    in_specs += [kv_spec(c) for c in range(4)] * 2
    in_specs += [
        pl.BlockSpec((hp, ctx_len, dh), lambda b, j, h: (h, b, 0)),
        pl.BlockSpec((hp, ctx_len, dh), lambda b, j, h: (h, b, 0)),
        pl.BlockSpec((hp, qt, 4 * kt), lambda b, j, h: (h, 0, 0)),
        pl.BlockSpec((1, qt, 4 * kt), lambda b, j, h: ((j > 0).astype(jnp.int32) + (j == nblk - 1).astype(jnp.int32), 0, 0)),
    ]
    return pl.pallas_call(
        _natten_body,
        grid=(batch, nblk, nh // hp),
        in_specs=in_specs,
        out_specs=pl.BlockSpec((qt, hp * dh), lambda b, j, h: (b * nblk + j, h)),
        out_shape=jax.ShapeDtypeStruct((n, nh * dh), BF16),
        compiler_params=_cparams("arbitrary", "arbitrary", "arbitrary"),
        name="neighbourhood_attention",
    )(q, k, k, k, k, v, v, v, v, kc, vc, col_bias, row_mask)


def _ctx_attn_body(q_ref, k_ref, v_ref, o_ref):
    outs = []
    for hh in range(q_ref.shape[0]):
        s = _dot_nt(q_ref[hh], k_ref[hh])
        m = jnp.max(s, axis=-1, keepdims=True)
        p = jnp.exp(s - m)
        l = jnp.sum(p, axis=-1, keepdims=True)
        outs.append(_dot(p.astype(BF16), v_ref[hh]) / l)
    o_ref[...] = jnp.concatenate(outs, axis=1).astype(o_ref.dtype)


def _ctx_attn(q, k, v, batch, ctx_len):
    nh, n, dh = q.shape
    hp = 2
    spec = pl.BlockSpec((hp, ctx_len, dh), lambda b, h: (h, b, 0))
    return pl.pallas_call(
        _ctx_attn_body,
        grid=(batch, nh // hp),
        in_specs=[spec, spec, spec],
        out_specs=pl.BlockSpec((ctx_len, hp * dh), lambda b, h: (b, h)),
        out_shape=jax.ShapeDtypeStruct((n, nh * dh), BF16),
        compiler_params=_cparams("arbitrary", "arbitrary"),
        name="context_attention",
    )(q, k, v)


def _gqa_body(sink_ref, q_ref, kp_ref, kc_ref, kn_ref, vp_ref, vc_ref, vn_ref, kx_ref, vx_ref, o_ref, *, nb):
    hkv, blk, dh = kc_ref.shape
    g = q_ref.shape[0] // hkv
    n = pl.program_id(1)
    shape = (g * blk, 3 * blk)
    qi = lax.broadcasted_iota(jnp.int32, shape, 0) % blk
    kp = lax.broadcasted_iota(jnp.int32, shape, 1) - blk
    ok = (jnp.abs(qi - kp) <= GQA_WINDOW) & ((kp >= 0) | (n > 0)) & ((kp < blk) | (n < nb - 1))
    grp = lax.broadcasted_iota(jnp.int32, (g * blk, 1), 0) // blk
    for kvh in range(hkv):
        q = q_ref[kvh * g:(kvh + 1) * g].reshape(g * blk, dh)
        s = jnp.concatenate([_dot_nt(q, r[kvh]) for r in (kp_ref, kc_ref, kn_ref)], axis=1)
        s = jnp.where(ok, s, NEG)
        sx = _dot_nt(q, kx_ref[kvh])
        sk = jnp.zeros((g * blk, 1), F32)
        for gi in range(g):
            sk = jnp.where(grp == gi, sink_ref[kvh * g + gi], sk)
        m = jnp.maximum(jnp.maximum(jnp.max(s, axis=-1, keepdims=True), jnp.max(sx, axis=-1, keepdims=True)), sk)
        p = jnp.exp(s - m)
        px = jnp.exp(sx - m)
        l = jnp.sum(p, axis=-1, keepdims=True) + jnp.sum(px, axis=-1, keepdims=True) + jnp.exp(sk - m)
        o = _dot(px.astype(BF16), vx_ref[kvh])
        for c, vr in enumerate((vp_ref, vc_ref, vn_ref)):
            o = o + _dot(p[:, c * blk:(c + 1) * blk].astype(BF16), vr[kvh])
        o = o / l
        o_ref[:, kvh * g * dh:(kvh + 1) * g * dh] = jnp.concatenate(
            [o[gi * blk:(gi + 1) * blk] for gi in range(g)], axis=1).astype(o_ref.dtype)


def _gqa(q, k, v, kx, vx, sink, batch, seq, ctx_len):
    hq, n, dh = q.shape
    hkv = k.shape[0]
    g = hq // hkv
    blk = GQA_BLOCK
    nb = seq // blk

    def kv_spec(c):
        return pl.BlockSpec((hkv, blk, dh), lambda b, i, s: (0, b * nb + jnp.clip(i - 1 + c, 0, nb - 1), 0))

    x_spec = pl.BlockSpec((hkv, ctx_len, dh), lambda b, i, s: (0, b, 0))
    grid_spec = pltpu.PrefetchScalarGridSpec(
        num_scalar_prefetch=1,
        grid=(batch, nb),
        in_specs=[pl.BlockSpec((hq, blk, dh), lambda b, i, s: (0, b * nb + i, 0))]
        + [kv_spec(c) for c in range(3)] * 2 + [x_spec, x_spec],
        out_specs=pl.BlockSpec((blk, hq * dh), lambda b, i, s: (b * nb + i, 0)),
    )
    return pl.pallas_call(
        functools.partial(_gqa_body, nb=nb),
        grid_spec=grid_spec,
        out_shape=jax.ShapeDtypeStruct((n, hq * dh), BF16),
        compiler_params=_cparams("arbitrary", "arbitrary"),
        name="window_gqa",
    )(sink, q, k, k, k, v, v, v, kx, vx)


def _oddeven_merge(lo, hi, r):
    step = r * 2
    if step < hi - lo:
        yield from _oddeven_merge(lo, hi, step)
        yield from _oddeven_merge(lo + r, hi, step)
        yield from [(i, i + r) for i in range(lo + r, hi - r, step)]
    else:
        yield (lo, lo + r)


def _oddeven_sort(lo, hi):
    if hi - lo >= 1:
        mid = lo + (hi - lo) // 2
        yield from _oddeven_sort(lo, mid)
        yield from _oddeven_sort(mid + 1, hi)
        yield from _oddeven_merge(lo, hi, 1)


def _exchange(t, i, j):
    t[i], t[j] = jnp.maximum(t[i], t[j]), jnp.minimum(t[i], t[j])


def _sort_tiles(tiles):
    t = list(tiles)
    for i, j in _oddeven_sort(0, PEER_TOPK - 1):
        if j < len(t):
            _exchange(t, i, j)
    return t


def _top_tiles(sorted_tiles):
    w = list(sorted_tiles)
    n = len(w)
    shift = SUBLANES // 2
    while shift:
        other = [pltpu.roll(x, shift, 0) for x in w]
        w = [jnp.maximum(w[k], other[n - 1 - k]) for k in range(n)]
        d = n // 2
        while d:
            for k in range(n):
                if not k & d:
                    _exchange(w, k, k + d)
            d //= 2
        shift //= 2
    return w


def _next_below(tiles, bound):
    m = None
    for t in tiles:
        v = jnp.where(t < bound, t, -jnp.inf)
        m = v if m is None else jnp.maximum(m, v)
    return jnp.max(m, axis=0, keepdims=True)


def _stack_sublanes(vals, sub):
    out = vals[0]
    for j in range(1, len(vals)):
        out = jnp.where(sub == j, vals[j], out)
    return out


def _bf16_rounded(v):
    return v.astype(BF16).astype(F32)


def _peer_route_body(x_ref, sh_ref, sc_ref, wq_ref, keys_ref, xm_ref, kap_ref, e1_ref, p2_ref):
    xm_ref, kap_ref, e1_ref, p2_ref = (r.at[0] for r in (xm_ref, kap_ref, e1_ref, p2_ref))
    nheads = p2_ref.shape[0]
    nk = keys_ref.shape[1]
    h = _ln(x_ref[...]) * (1.0 + sc_ref[0]) + sh_ref[0]
    ht = h.T.astype(BF16)
    xm_ref[...] = ht
    qt = _dot(wq_ref[...], ht).astype(BF16)
    qd = keys_ref.shape[2]
    k16 = PEER_TOPK
    sub = lax.broadcasted_iota(jnp.int32, (SUBLANES, 1), 0)
    for hd in range(nheads):
        halves = []
        for half in range(2):
            hp = hd * 2 + half
            s = _dot(keys_ref[hp], qt[hp * qd:(hp + 1) * qd, :])
            tiles = [s[SUBLANES * k:SUBLANES * (k + 1)] for k in range(nk // SUBLANES)]
            top = _top_tiles(_sort_tiles(tiles))
            halves.append((s, top, _next_below(tiles, top[-1])))
        (s1, t1, t1_17), (s2, t2, t2_17) = halves
        m1, m2 = t1[0][:1], t2[0][:1]
        cmax = m1 + m2
        t2_lo, t2_hi = _stack_sublanes(t2[:SUBLANES], sub), _stack_sublanes(t2[SUBLANES:], sub)
        t1_hi = _stack_sublanes(t1[SUBLANES:], sub)
        pairs = [(t1[0], t2_lo), (t1[0], t2_hi)]
        pairs += [(t1[a], jnp.where(sub < (k16 + 1) // (a + 1), t2_lo, -jnp.inf)) for a in range(1, SUBLANES)]
        pairs += [(t1_hi, t2[0])]
        pairs += [(jnp.where(sub == 0, t1[0], t1_17), jnp.where(sub == 0, t2_17, jnp.where(sub == 1, t2[0], -jnp.inf)))]
        cands = [a + b for a, b in pairs]
        ctop = _top_tiles(_sort_tiles(cands) + [jnp.full_like(cands[0], -jnp.inf)] * (k16 - len(cands)))
        c16 = ctop[-1][:1]
        tau = 0.5 * (c16 + _next_below(cands, c16))
        z = None
        for (a, b), c in zip(pairs, cands):
            picked = _bf16_rounded(jnp.exp(b - m2)) >= _bf16_rounded(jnp.exp(tau - a - m2))
            zc = jnp.where(picked, jnp.exp(c - cmax), 0.0)
            z = zc if z is None else z + zc
        z = jnp.sum(z, axis=0, keepdims=True)
        groups = (nk // SUBLANES, SUBLANES, s1.shape[1])
        kap_ref[hd] = jnp.exp(tau - s1 - m2).reshape(groups)
        e1_ref[hd] = (jnp.exp(s1 - m1) * (0.5 / z)).reshape(groups)
        p2_ref[hd] = jnp.exp(s2 - m2).astype(BF16)


def _peer_route(x, shift, scale, wq_t, keys, tokens_per_batch):
    n, d = x.shape
    tt = PEER_COLS
    per = tokens_per_batch // tt
    nh = keys.shape[0] // 2
    nk = keys.shape[1]
    big = pl.BlockSpec((1, nh, nk, tt), lambda i: (i, 0, 0, 0))
    rows = pl.BlockSpec((1, nh, nk // SUBLANES, SUBLANES, tt), lambda i: (i, 0, 0, 0, 0))
    rows_shape = jax.ShapeDtypeStruct((n // tt, nh, nk // SUBLANES, SUBLANES, tt), F32)
    return pl.pallas_call(
        _peer_route_body,
        grid=(n // tt,),
        in_specs=[
            pl.BlockSpec((tt, d), lambda i: (i, 0)),
            pl.BlockSpec((1, 1, d), lambda i: (i // per, 0, 0)),
            pl.BlockSpec((1, 1, d), lambda i: (i // per, 0, 0)),
            pl.BlockSpec(wq_t.shape, lambda i: (0, 0)),
            pl.BlockSpec(keys.shape, lambda i: (0, 0, 0)),
        ],
        out_specs=[pl.BlockSpec((1, d, tt), lambda i: (i, 0, 0)), rows, rows, big],
        out_shape=[jax.ShapeDtypeStruct((n // tt, d, tt), BF16), rows_shape, rows_shape,
                   jax.ShapeDtypeStruct((n // tt, nh, nk, tt), BF16)],
        compiler_params=_cparams("arbitrary"),
        name="peer_route",
    )(x, shift, scale, wq_t, keys)


PEER_UNIT_ROWS = 4


def _peer_expert_body(xm_ref, u_ref, vt_ref, kap_ref, e1_ref, p2_ref,
                      x_ref, gate_ref, g_ref, b_ref, o_ref, acc_ref, a_ref, w_ref, *, alpha):
    e = pl.program_id(1)
    ncb, nheads, ngroups, _, cw = kap_ref.shape
    nk = p2_ref.shape[2]
    ur = PEER_UNIT_ROWS
    ue = ur * nk
    nrp = ngroups * SUBLANES // ur
    n_units = ncb * nrp
    assert 2 * ur == SUBLANES and nrp % 2 == 0

    @pl.when(e == 0)
    def _():
        acc_ref[...] = jnp.zeros_like(acc_ref)

    def first_matmul(i, slot):
        rows = pl.ds(pl.multiple_of((i % nrp) * ue, ue), ue)
        a_ref[slot] = _dot(u_ref[rows, :], xm_ref[i // nrp])

    def second_matmul(i, slot):
        acc_ref[i // nrp] += _dot(vt_ref[i % nrp], w_ref[slot])

    def gate_and_activate(i, slot):
        c, grp = i // nrp, (i % nrp) // 2
        for r in range(ur):
            row = slot * ur + r
            gsum = None
            for hd in range(nheads):
                p2 = p2_ref[c, hd]
                keep = p2 >= kap_ref[c, hd, grp, row:row + 1, :].astype(BF16)
                gate = p2 * e1_ref[c, hd, grp, row:row + 1, :].astype(BF16)
                term = jnp.where(keep, gate, jnp.zeros_like(p2))
                gsum = term if gsum is None else gsum + term
            a = a_ref[slot, r * nk:(r + 1) * nk, :]
            act = (a * (1.0 + lax.erf(a * (2.0 ** -0.5)))).astype(BF16)
            w_ref[slot, r * nk:(r + 1) * nk, :] = gsum * act

    def steady(j, carry):
        i = 2 * j + 1
        first_matmul(i + 1, 0)
        gate_and_activate(i, 1)
        second_matmul(i - 1, 0)
        first_matmul(i + 2, 1)
        gate_and_activate(i + 1, 0)
        second_matmul(i, 1)
        return carry

    assert n_units % 2 == 0
    first_matmul(0, 0)
    first_matmul(1, 1)
    gate_and_activate(0, 0)
    lax.fori_loop(0, n_units // 2 - 1, steady, 0, unroll=3)
    gate_and_activate(n_units - 1, 1)
    second_matmul(n_units - 2, 0)
    second_matmul(n_units - 1, 1)

    @pl.when(e == pl.num_programs(1) - 1)
    def _():
        for c in range(ncb):
            f = acc_ref[c].T
            z = alpha * x_ref[c * cw:(c + 1) * cw, :] + gate_ref[0] * f
            o_ref[c * cw:(c + 1) * cw, :] = _ln(z) * g_ref[...] + b_ref[...]


def _peer_experts(xm_t, u, v_t, kap, e1, p2, x, gate, g, b, tokens_per_batch, alpha, tt=1024, et=2048):
    n, d = x.shape
    tt = min(tt, tokens_per_batch)
    per = tokens_per_batch // tt
    _, nh, nk, cw = p2.shape
    ncb = tt // cw
    n1 = et // nk
    ue = PEER_UNIT_ROWS * nk
    sel = pl.BlockSpec((ncb, nh, n1 // SUBLANES, SUBLANES, cw), lambda i, e: (i, 0, e, 0, 0))
    full = pl.BlockSpec((ncb, nh, nk, cw), lambda i, e: (i, 0, 0, 0))
    return pl.pallas_call(
        functools.partial(_peer_expert_body, alpha=alpha),
        grid=(n // tt, u.shape[0] // et),
        in_specs=[
            pl.BlockSpec((ncb, d, cw), lambda i, e: (i, 0, 0)),
            pl.BlockSpec((et, d), lambda i, e: (e, 0)),
            pl.BlockSpec((et // ue, d, ue), lambda i, e: (e, 0, 0)),
            sel, sel, full,
            pl.BlockSpec((tt, d), lambda i, e: (i, 0)),
            pl.BlockSpec((1, 1, d), lambda i, e: (i // per, 0, 0)),
            pl.BlockSpec((1, d), lambda i, e: (0, 0)),
            pl.BlockSpec((1, d), lambda i, e: (0, 0)),
        ],
        out_specs=pl.BlockSpec((tt, d), lambda i, e: (i, 0)),
        out_shape=jax.ShapeDtypeStruct((n, d), F32),
        scratch_shapes=[
            pltpu.VMEM((ncb, d, cw), F32),
            pltpu.VMEM((2, ue, cw), F32),
            pltpu.VMEM((2, ue, cw), BF16),
        ],
        compiler_params=_cparams("arbitrary", "arbitrary"),
        name="peer_experts",
    )(xm_t, u, v_t, kap, e1, p2, x, gate, g.reshape(1, d), b.reshape(1, d))


def _peer_layer(x, shift, scale, gate, g, b, tables, tokens_per_batch, alpha):
    wq_t, keys, u, v_t = tables
    xm_t, kap, e1, p2 = _peer_route(x, shift, scale, wq_t, keys, tokens_per_batch)
    return _peer_experts(xm_t, u, v_t, kap, e1, p2, x, gate, g, b, tokens_per_batch, alpha)


def _rope_tables(seq):
    t = jnp.arange(seq)
    row = (t // GRID_W).astype(F32)
    col = (t % GRID_W).astype(F32)
    n_freq = HEAD_DIM // 4
    inv_freq = ROPE_THETA ** (-jnp.arange(n_freq, dtype=F32) / n_freq)
    ang = jnp.concatenate([row[:, None] * inv_freq, col[:, None] * inv_freq], -1)
    cos, sin = jnp.cos(ang), jnp.sin(ang)
    reps = LANES // (HEAD_DIM // 2)
    return jnp.tile(cos, (1, reps)), jnp.tile(sin, (1, reps))


def _rotate_half_columns(w, n_heads):
    d = w.shape[0]
    wh = w.reshape(d, n_heads, 2, HEAD_DIM // 2)
    return jnp.concatenate([-wh[:, :, 1], wh[:, :, 0]], axis=-1).reshape(d, n_heads * HEAD_DIM)


def _expert_table_body(u_ref, v_ref, ub_ref, vt_ref):
    ub_ref[...] = u_ref[0].astype(BF16)
    vt_ref[0] = v_ref[0].T.astype(BF16)


def _peer_tables(w_q, sub_keys, u_all, v_all, layer):
    nh, _, nk, qd = sub_keys.shape
    _, ne, d = u_all.shape
    ue = PEER_UNIT_ROWS * nk
    u_b, v_chunks = pl.pallas_call(
        _expert_table_body,
        grid=(ne // ue,),
        in_specs=[pl.BlockSpec((1, ue, d), lambda i: (layer, i, 0)), pl.BlockSpec((1, ue, d), lambda i: (layer, i, 0))],
        out_specs=[pl.BlockSpec((ue, d), lambda i: (i, 0)), pl.BlockSpec((1, d, ue), lambda i: (i, 0, 0))],
        out_shape=[jax.ShapeDtypeStruct((ne, d), BF16), jax.ShapeDtypeStruct((ne // ue, d, ue), BF16)],
        compiler_params=_cparams("arbitrary"),
        name="expert_table_layout",
    )(u_all, v_all)
    return (w_q.T.astype(BF16), sub_keys.reshape(nh * 2, nk, qd).astype(BF16), u_b, v_chunks)


def kernel(x, c, ctx, c_ctx, ada_w, ada_b, post_ln_g, post_ln_b, even_w_in, even_w_out, na_rpb,
           odd_w_in, odd_w_out, gqa_sink, peer_w_q, peer_sub_keys, peer_u, peer_v):
    batch, seq, d = x.shape
    ctx_len = ctx.shape[1]
    depth = ada_w.shape[0]
    rows = seq // GRID_W
    alpha = float((2 * depth) ** 0.25)
    fw = FNET_GROUPS * HEAD_DIM
    nw = NA_HEADS * HEAD_DIM
    qw = GQA_Q_HEADS * HEAD_DIM
    kvw = GQA_KV_HEADS * HEAD_DIM
    qscale = HEAD_DIM ** -0.5

    cond = jnp.zeros((8, d), F32).at[:batch].set(c).at[batch].set(c_ctx)
    mods = _ada(cond, ada_w, ada_b)

    xl = x.reshape(batch * seq, d)
    hc = ctx.reshape(batch * ctx_len, d)
    cos_t, sin_t = _rope_tables(seq)

    cg, sg = _dft_tables(HEAD_DIM)
    eye = np.eye(FNET_GROUPS)
    chan = jnp.asarray(np.concatenate([np.kron(eye, cg), -np.kron(eye, sg)], axis=1), F32)

    for layer in range(depth):
        ctx_out = layer < depth - 1
        i = layer // 2
        m_l = [m.reshape(batch, 1, d) for m in jnp.split(mods[layer, :batch], 6, axis=-1)]
        m_c = [jnp.broadcast_to(m.reshape(1, 1, d), (batch, 1, d)) for m in jnp.split(mods[layer, batch], 6, axis=-1)]
        g0, b0 = post_ln_g[layer, 0], post_ln_b[layer, 0]
        g1, b1 = post_ln_g[layer, 1], post_ln_b[layer, 1]

        if layer % 2 == 0:
            w_in, w_out = even_w_in[i], even_w_out[i]
            w_f = _matmul_f32(w_in[:, :fw], chan)
            w_aug = jnp.concatenate([w_f, w_in[:, fw:]], axis=1).astype(BF16)
            plan = (("nat", 0, 2 * fw, 1.0, None),
                    ("heads", 2 * fw, nw, qscale, None),
                    ("heads", 2 * fw + nw, nw, 1.0, None),
                    ("heads", 2 * fw + 2 * nw, nw, 1.0, None))
            f_l, q_l, k_l, v_l = _proj(xl, m_l[0], m_l[1], w_aug, plan, seq)
            f_c, q_c, k_c, v_c = _proj(hc, m_c[0], m_c[1], w_aug, plan, ctx_len)
            col_bias, row_mask = _natten_tables(rows, na_rpb[i])
            na_l = _natten(q_l, k_l, v_l, k_c, v_c, col_bias, row_mask, batch, rows, ctx_len)
            fm_l = _fourier_latent(f_l, batch, rows, fw)
            w_out_b = w_out.astype(BF16)
            ws = [w_out_b[:fw], w_out_b[fw:]]
            xl_new = _outproj_ln([fm_l, na_l], ws, xl, m_l[2], g0, b0, seq, alpha)
            if ctx_out:
                na_c = _ctx_attn(q_c, k_c, v_c, batch, ctx_len)
                fm_c = _fourier_dense(f_c, batch, ctx_len, fw)
                hc_new = _outproj_ln([fm_c, na_c], ws, hc, m_c[2], g0, b0, ctx_len, alpha)
        else:
            w_in, w_out = odd_w_in[i], odd_w_out[i]
            wq, wk, wv = w_in[:, :qw], w_in[:, qw:qw + kvw], w_in[:, qw + kvw:]
            w_aug = jnp.concatenate([wq, wk, wv, _rotate_half_columns(wq, GQA_Q_HEADS),
                                     _rotate_half_columns(wk, GQA_KV_HEADS)], axis=1).astype(BF16)
            plan = (("heads", 0, qw, qscale, qw + 2 * kvw),
                    ("heads", qw, kvw, 1.0, 2 * qw + 2 * kvw),
                    ("heads", qw + kvw, kvw, 1.0, None))
            q_l, k_l, v_l = _proj(xl, m_l[0], m_l[1], w_aug, plan, seq, rope=(cos_t, sin_t))
            plan_c = (("heads", 0, kvw, 1.0, None), ("heads", kvw, kvw, 1.0, None))
            if ctx_out:
                raise NotImplementedError("an odd layer must be the last layer (no context output path)")
            k_c, v_c = _proj(hc, m_c[0], m_c[1], w_in[:, qw:].astype(BF16), plan_c, ctx_len)
            y_l = _gqa(q_l, k_l, v_l, k_c, v_c, gqa_sink[i], batch, seq, ctx_len)
            xl_new = _outproj_ln([y_l], [w_out.astype(BF16)], xl, m_l[2], g0, b0, seq, alpha)

        tables = _peer_tables(peer_w_q[layer], peer_sub_keys[layer], peer_u, peer_v, layer)
        xl = _peer_layer(xl_new, m_l[3], m_l[4], m_l[5], g1, b1, tables, seq, alpha)
        if ctx_out:
            hc = _peer_layer(hc_new, m_c[3], m_c[4], m_c[5], g1, b1, tables, ctx_len, alpha)

    return xl.reshape(batch, seq, d)
```

```python
import functools

import numpy as np
import jax
import jax.numpy as jnp
from jax import lax
from jax.experimental import pallas as pl
from jax.experimental.pallas import tpu as pltpu

F32 = jnp.float32
BF16 = jnp.bfloat16

HEAD_DIM = 64
GRID_W = 64
FNET_GROUPS = 8
NA_HEADS = 8
NA_KR = 8
NA_KC = 16
NA_ROWS_PER_BLOCK = 8
NA_WIN_ROWS = 16
GQA_Q_HEADS = 16
GQA_KV_HEADS = 4
GQA_WINDOW = 128
GQA_BLOCK = 128
ROPE_THETA = 10000.0
PEER_TOPK = 16
LN_EPS = 1e-6
NEG = -1e30

PEER_COLS = 256
LANES = 128
SUBLANES = 8
VMEM_LIMIT = 56 * 1024 * 1024


def _cparams(*sem):
    return pltpu.CompilerParams(dimension_semantics=sem, vmem_limit_bytes=VMEM_LIMIT)


def _ln(x):
    mu = jnp.mean(x, axis=-1, keepdims=True)
    xc = x - mu
    var = jnp.mean(xc * xc, axis=-1, keepdims=True)
    return xc * lax.rsqrt(var + LN_EPS)


def _dot(a, b):
    return jnp.dot(a, b, preferred_element_type=F32)


def _dot_nt(a, b):
    return lax.dot_general(a, b, (((1,), (1,)), ((), ())), preferred_element_type=F32)


def _ada_body(c_ref, w_ref, b_ref, o_ref):
    c = c_ref[...]
    o_ref[0] = _dot(c * jax.nn.sigmoid(c), w_ref[0]) + b_ref[0]


def _ada(cond, ada_w, ada_b):
    depth, d, n = ada_w.shape
    tn = 1536
    return pl.pallas_call(
        _ada_body,
        grid=(depth, n // tn),
        in_specs=[
            pl.BlockSpec((8, d), lambda l, j: (0, 0)),
            pl.BlockSpec((1, d, tn), lambda l, j: (l, 0, j)),
            pl.BlockSpec((1, 1, tn), lambda l, j: (l, 0, j)),
        ],
        out_specs=pl.BlockSpec((1, 8, tn), lambda l, j: (l, 0, j)),
        out_shape=jax.ShapeDtypeStruct((depth, 8, n), F32),
        compiler_params=_cparams("arbitrary", "arbitrary"),
        name="ada_modulation",
    )(cond, ada_w, ada_b.reshape(depth, 1, n))


def _matmul_f32_body(a_ref, b_ref, o_ref):
    o_ref[...] = jnp.dot(a_ref[...], b_ref[...], preferred_element_type=F32, precision=lax.Precision.HIGHEST)


def _matmul_f32(a, b):
    m, k = a.shape
    n = b.shape[1]
    tm = min(256, m)
    return pl.pallas_call(
        _matmul_f32_body,
        grid=(m // tm,),
        in_specs=[pl.BlockSpec((tm, k), lambda i: (i, 0)), pl.BlockSpec((k, n), lambda i: (0, 0))],
        out_specs=pl.BlockSpec((tm, n), lambda i: (i, 0)),
        out_shape=jax.ShapeDtypeStruct((m, n), F32),
        compiler_params=_cparams("arbitrary"),
        name="small_matmul_f32",
    )(a, b)


def _proj_body(*refs, plan, use_rope):
    x_ref, sh_ref, sc_ref, w_ref = refs[:4]
    rest = refs[4:]
    if use_rope:
        cos_ref, sin_ref = rest[:2]
        rest = rest[2:]
    h = _ln(x_ref[...]) * (1.0 + sc_ref[0]) + sh_ref[0]
    acc = _dot(h.astype(BF16), w_ref[...])
    for o_ref, (kind, start, width, scale, rot_start) in zip(rest, plan):
        y = acc[:, start:start + width]
        if rot_start is not None:
            reps = width // LANES
            cos = jnp.tile(cos_ref[...], (1, reps))
            sin = jnp.tile(sin_ref[...], (1, reps))
            y = y * cos + acc[:, rot_start:rot_start + width] * sin
        if scale != 1.0:
            y = y * scale
        if kind == "nat":
            o_ref[...] = y.astype(o_ref.dtype)
        else:
            for hh in range(width // HEAD_DIM):
                o_ref[hh] = y[:, hh * HEAD_DIM:(hh + 1) * HEAD_DIM].astype(o_ref.dtype)


def _proj(x, shift, scale, w, plan, tokens_per_batch, rope=None, tm=512):
    n, d = x.shape
    tm = min(tm, tokens_per_batch)
    per = tokens_per_batch // tm
    in_specs = [
        pl.BlockSpec((tm, d), lambda i: (i, 0)),
        pl.BlockSpec((1, 1, d), lambda i: (i // per, 0, 0)),
        pl.BlockSpec((1, 1, d), lambda i: (i // per, 0, 0)),
        pl.BlockSpec(w.shape, lambda i: (0, 0)),
    ]
    args = [x, shift, scale, w]
    if rope is not None:
        in_specs += [pl.BlockSpec((tm, LANES), lambda i: (i % per, 0))] * 2
        args += list(rope)
    out_specs, out_shape = [], []
    for kind, start, width, sc, rot in plan:
        if kind == "nat":
            out_specs.append(pl.BlockSpec((tm, width), lambda i: (i, 0)))
            out_shape.append(jax.ShapeDtypeStruct((n, width), F32))
        else:
            nh = width // HEAD_DIM
            out_specs.append(pl.BlockSpec((nh, tm, HEAD_DIM), lambda i: (0, i, 0)))
            out_shape.append(jax.ShapeDtypeStruct((nh, n, HEAD_DIM), BF16))
    return pl.pallas_call(
        functools.partial(_proj_body, plan=plan, use_rope=rope is not None),
        grid=(n // tm,),
        in_specs=in_specs,
        out_specs=out_specs,
        out_shape=out_shape,
        compiler_params=_cparams("arbitrary"),
        name="modln_proj",
    )(*args)


def _outproj_body(*refs, n_in, alpha):
    ys, ws = refs[:n_in], refs[n_in:2 * n_in]
    x_ref, gate_ref, g_ref, b_ref, o_ref = refs[2 * n_in:]
    acc = None
    for y_ref, w_ref in zip(ys, ws):
        t = _dot(y_ref[...].astype(BF16), w_ref[...])
        acc = t if acc is None else acc + t
    z = alpha * x_ref[...] + gate_ref[0] * acc
    o_ref[...] = _ln(z) * g_ref[...] + b_ref[...]


def _outproj_ln(ys, ws, x, gate, g, b, tokens_per_batch, alpha, tm=512):
    n, d = x.shape
    tm = min(tm, tokens_per_batch)
    per = tokens_per_batch // tm
    in_specs = [pl.BlockSpec((tm, y.shape[1]), lambda i: (i, 0)) for y in ys]
    in_specs += [pl.BlockSpec(w.shape, lambda i: (0, 0)) for w in ws]
    in_specs += [
        pl.BlockSpec((tm, d), lambda i: (i, 0)),
        pl.BlockSpec((1, 1, d), lambda i: (i // per, 0, 0)),
        pl.BlockSpec((1, d), lambda i: (0, 0)),
        pl.BlockSpec((1, d), lambda i: (0, 0)),
    ]
    return pl.pallas_call(
        functools.partial(_outproj_body, n_in=len(ys), alpha=alpha),
        grid=(n // tm,),
        in_specs=in_specs,
        out_specs=pl.BlockSpec((tm, d), lambda i: (i, 0)),
        out_shape=jax.ShapeDtypeStruct((n, d), F32),
        compiler_params=_cparams("arbitrary"),
        name="outproj_residual_ln",
    )(*ys, *ws, x, gate, g.reshape(1, d), b.reshape(1, d))


def _dft_tables(n):
    idx = np.arange(n)
    ang = 2.0 * np.pi * ((idx[:, None] * idx[None, :]) % n) / n
    return np.cos(ang), np.sin(ang)


def _fourier_rows_body(x_ref, cs_ref, tc_ref, ts_ref, o_ref, *, cb, width):
    nr = cs_ref.shape[1]
    for j in range(cb):
        pq = _dot(cs_ref[...], x_ref[0, :, j, :])
        yr = pq[:nr, :width] + pq[nr:, width:]
        yi = pq[:nr, width:] - pq[nr:, :width]
        tc, ts = tc_ref[j], ts_ref[j]
        o_ref[0, :, j, :width] = yr * tc + yi * ts
        o_ref[0, :, j, width:] = yi * tc - yr * ts


def _fourier_cols_body(y_ref, cs_ref, o_ref, *, kb, width, norm):
    nc = cs_ref.shape[1]
    for j in range(kb):
        pq = _dot(cs_ref[...], y_ref[0, j])
        o_ref[0, :, j, :] = (pq[:nc, :width] + pq[nc:, width:]) * norm


def _fourier_latent(f, batch, rows, width):
    cols = GRID_W
    seq = rows * cols
    c_r, s_r = _dft_tables(rows)
    c_c, s_c = _dft_tables(cols)
    k1 = np.arange(rows)[None, :]
    cc = np.arange(cols)[:, None]
    tw = 2.0 * np.pi * ((cc * k1) % seq) / seq
    cs_r = jnp.asarray(np.concatenate([c_r, s_r], 0), F32)
    cs_c = jnp.asarray(np.concatenate([c_c, s_c], 0), F32)
    tc = jnp.asarray(np.cos(tw)[:, :, None], F32)
    ts = jnp.asarray(np.sin(tw)[:, :, None], F32)
    cb = SUBLANES
    kb = SUBLANES
    lane_w = 2 * width
    y = pl.pallas_call(
        functools.partial(_fourier_rows_body, cb=cb, width=width),
        grid=(batch, cols // cb),
        in_specs=[
            pl.BlockSpec((1, rows, cb, lane_w), lambda b, j: (b, 0, j, 0)),
            pl.BlockSpec((2 * rows, rows), lambda b, j: (0, 0)),
            pl.BlockSpec((cb, rows, 1), lambda b, j: (j, 0, 0)),
            pl.BlockSpec((cb, rows, 1), lambda b, j: (j, 0, 0)),
        ],
        out_specs=pl.BlockSpec((1, rows, cb, lane_w), lambda b, j: (b, 0, j, 0)),
        out_shape=jax.ShapeDtypeStruct((batch, rows, cols, lane_w), F32),
        compiler_params=_cparams("arbitrary", "arbitrary"),
        name="fourier_rows",
    )(f.reshape(batch, rows, cols, lane_w), cs_r, tc, ts)
    z = pl.pallas_call(
        functools.partial(_fourier_cols_body, kb=kb, width=width, norm=float((seq * HEAD_DIM) ** -0.5)),
        grid=(batch, rows // kb),
        in_specs=[
            pl.BlockSpec((1, kb, cols, lane_w), lambda b, j: (b, j, 0, 0)),
            pl.BlockSpec((2 * cols, cols), lambda b, j: (0, 0)),
        ],
        out_specs=pl.BlockSpec((1, cols, kb, width), lambda b, j: (b, 0, j, 0)),
        out_shape=jax.ShapeDtypeStruct((batch, cols, rows, width), F32),
        compiler_params=_cparams("arbitrary", "arbitrary"),
        name="fourier_cols",
    )(y, cs_c)
    return z.reshape(batch * seq, width)


def _fourier_dense_body(x_ref, c_ref, s_ref, o_ref, *, width, norm):
    x = x_ref[...]
    o_ref[...] = (_dot(c_ref[...], x[:, :width]) + _dot(s_ref[...], x[:, width:])) * norm


def _fourier_dense(f, batch, length, width):
    c, s = _dft_tables(length)
    return pl.pallas_call(
        functools.partial(_fourier_dense_body, width=width, norm=float((length * HEAD_DIM) ** -0.5)),
        grid=(batch,),
        in_specs=[
            pl.BlockSpec((length, 2 * width), lambda b: (b, 0)),
            pl.BlockSpec((length, length), lambda b: (0, 0)),
            pl.BlockSpec((length, length), lambda b: (0, 0)),
        ],
        out_specs=pl.BlockSpec((length, width), lambda b: (b, 0)),
        out_shape=jax.ShapeDtypeStruct((batch * length, width), F32),
        compiler_params=_cparams("arbitrary"),
        name="fourier_dense",
    )(f, jnp.asarray(c, F32), jnp.asarray(s, F32))


def _natten_tables(rows, rpb):
    w = GRID_W
    qb, kw = NA_ROWS_PER_BLOCK, NA_WIN_ROWS
    nh, ndr, ndc = rpb.shape
    tq, tk = np.divmod(np.arange(w * w), w)
    dc_i = np.clip(tk - tq, -(NA_KC - 1), NA_KC - 1) + (NA_KC - 1)
    onehot = np.zeros((LANES, w * w), np.float32)
    onehot[dc_i, np.arange(w * w)] = 1.0
    c_start = np.clip(tq - NA_KC // 2, 0, w - NA_KC)
    col_ok = ((tk >= c_start) & (tk < c_start + NA_KC)).reshape(w, w)
    rp = jnp.zeros((LANES, LANES), F32).at[:nh * ndr, :ndc].set(rpb.reshape(nh * ndr, ndc).astype(F32))
    tiles = _matmul_f32(rp, jnp.asarray(onehot))[:nh * ndr].reshape(nh, ndr, w, w)
    tiles = jnp.where(jnp.asarray(col_ok), tiles, NEG)
    dr = np.arange(kw)[None, :] - np.arange(qb)[:, None] - (kw - qb) // 2
    dr_i = np.clip(dr + (NA_KR - 1), 0, ndr - 1)
    col_bias = jnp.transpose(tiles[:, dr_i], (0, 1, 3, 2, 4)).reshape(nh, qb * w, kw * w)
    qi = np.repeat(np.arange(qb), w)
    km = np.repeat(np.arange(kw), w)
    nblk = rows // qb
    masks = []
    for j in range(nblk):
        qr = j * qb + qi
        kr = j * qb - (kw - qb) // 2 + km
        r_start = np.clip(qr - NA_KR // 2, 0, rows - NA_KR)
        ok = (kr[None, :] >= r_start[:, None]) & (kr[None, :] < r_start[:, None] + NA_KR)
        masks.append(np.where(ok, 0.0, NEG).astype(np.float32))
    kinds = [(j > 0) + (j == nblk - 1) for j in range(nblk)]
    table = np.zeros((3,) + masks[0].shape, np.float32)
    for j, kd in enumerate(kinds):
        table[kd] = masks[j]
    for j, kd in enumerate(kinds):
        assert np.array_equal(table[kd], masks[j])
    half, quarter = qb * w // 2, kw * w // 4
    for mask in masks:
        assert (mask[:half, 3 * quarter:] == NEG).all() and (mask[half:, :quarter] == NEG).all()
    return col_bias, jnp.asarray(table)


def _natten_body(q_ref, k0_ref, k1_ref, k2_ref, k3_ref, v0_ref, v1_ref, v2_ref, v3_ref,
                 kc_ref, vc_ref, cb_ref, rm_ref, o_ref):
    k_refs = (k0_ref, k1_ref, k2_ref, k3_ref)
    v_refs = (v0_ref, v1_ref, v2_ref, v3_ref)
    qt, kb = q_ref.shape[1], k0_ref.shape[1]
    hq = qt // 2
    outs = []
    for hh in range(q_ref.shape[0]):
        halves = []
        for part in range(2):
            rows = slice(part * hq, (part + 1) * hq)
            cols = slice(part * kb, (part + 3) * kb)
            q = q_ref[hh, rows, :]
            s = jnp.concatenate([_dot_nt(q, kr[hh]) for kr in k_refs[part:part + 3]], axis=1)
            s = s + cb_ref[hh, rows, cols] + rm_ref[0, rows, cols]
            sc = _dot_nt(q, kc_ref[hh])
            m = jnp.maximum(jnp.max(s, axis=-1, keepdims=True), jnp.max(sc, axis=-1, keepdims=True))
            p = jnp.exp(s - m)
            pc = jnp.exp(sc - m)
            l = jnp.sum(p, axis=-1, keepdims=True) + jnp.sum(pc, axis=-1, keepdims=True)
            o = _dot(pc.astype(BF16), vc_ref[hh])
            for c, vr in enumerate(v_refs[part:part + 3]):
                o = o + _dot(p[:, c * kb:(c + 1) * kb].astype(BF16), vr[hh])
            halves.append(o / l)
        outs.append(jnp.concatenate(halves, axis=0))
    o_ref[...] = jnp.concatenate(outs, axis=1).astype(o_ref.dtype)


def _natten(q, k, v, kc, vc, col_bias, row_mask, batch, rows, ctx_len):
    nh, n, dh = q.shape
    hp = 4
    qt = NA_ROWS_PER_BLOCK * GRID_W
    kt = NA_WIN_ROWS * GRID_W // 4
    nblk = rows // NA_ROWS_PER_BLOCK
    kblocks = rows * GRID_W // kt

    def kv_spec(c):
        return pl.BlockSpec(
            (hp, kt, dh),
            lambda b, j, h: (h, b * kblocks + jnp.clip(2 * j - 1 + c, 0, kblocks - 1), 0))

    in_specs = [pl.BlockSpec((hp, qt, dh), lambda b, j, h: (h, b * nblk + j, 0))]
    in_specs += [kv_spec(c) for c in range(4)] * 2
    in_specs += [
        pl.BlockSpec((hp, ctx_len, dh), lambda b, j, h: (h, b, 0)),
        pl.BlockSpec((hp, ctx_len, dh), lambda b, j, h: (h, b, 0)),
        pl.BlockSpec((hp, qt, 4 * kt), lambda b, j, h: (h, 0, 0)),
        pl.BlockSpec((1, qt, 4 * kt), lambda b, j, h: ((j > 0).astype(jnp.int32) + (j == nblk - 1).astype(jnp.int32), 0, 0)),
    ]
    return pl.pallas_call(
        _natten_body,
        grid=(batch, nblk, nh // hp),
        in_specs=in_specs,
        out_specs=pl.BlockSpec((qt, hp * dh), lambda b, j, h: (b * nblk + j, h)),
        out_shape=jax.ShapeDtypeStruct((n, nh * dh), BF16),
        compiler_params=_cparams("arbitrary", "arbitrary", "arbitrary"),
        name="neighbourhood_attention",
    )(q, k, k, k, k, v, v, v, v, kc, vc, col_bias, row_mask)


def _ctx_attn_body(q_ref, k_ref, v_ref, o_ref):
    outs = []
    for hh in range(q_ref.shape[0]):
        s = _dot_nt(q_ref[hh], k_ref[hh])
        m = jnp.max(s, axis=-1, keepdims=True)
        p = jnp.exp(s - m)
        l = jnp.sum(p, axis=-1, keepdims=True)
        outs.append(_dot(p.astype(BF16), v_ref[hh]) / l)
    o_ref[...] = jnp.concatenate(outs, axis=1).astype(o_ref.dtype)


def _ctx_attn(q, k, v, batch, ctx_len):
    nh, n, dh = q.shape
    hp = 2
    spec = pl.BlockSpec((hp, ctx_len, dh), lambda b, h: (h, b, 0))
    return pl.pallas_call(
        _ctx_attn_body,
        grid=(batch, nh // hp),
        in_specs=[spec, spec, spec],
        out_specs=pl.BlockSpec((ctx_len, hp * dh), lambda b, h: (b, h)),
        out_shape=jax.ShapeDtypeStruct((n, nh * dh), BF16),
        compiler_params=_cparams("arbitrary", "arbitrary"),
        name="context_attention",
    )(q, k, v)


def _gqa_body(sink_ref, q_ref, kp_ref, kc_ref, kn_ref, vp_ref, vc_ref, vn_ref, kx_ref, vx_ref, o_ref, *, nb):
    hkv, blk, dh = kc_ref.shape
    g = q_ref.shape[0] // hkv
    n = pl.program_id(1)
    shape = (g * blk, 3 * blk)
    qi = lax.broadcasted_iota(jnp.int32, shape, 0) % blk
    kp = lax.broadcasted_iota(jnp.int32, shape, 1) - blk
    ok = (jnp.abs(qi - kp) <= GQA_WINDOW) & ((kp >= 0) | (n > 0)) & ((kp < blk) | (n < nb - 1))
    grp = lax.broadcasted_iota(jnp.int32, (g * blk, 1), 0) // blk
    for kvh in range(hkv):
        q = q_ref[kvh * g:(kvh + 1) * g].reshape(g * blk, dh)
        s = jnp.concatenate([_dot_nt(q, r[kvh]) for r in (kp_ref, kc_ref, kn_ref)], axis=1)
        s = jnp.where(ok, s, NEG)
        sx = _dot_nt(q, kx_ref[kvh])
        sk = jnp.zeros((g * blk, 1), F32)
        for gi in range(g):
            sk = jnp.where(grp == gi, sink_ref[kvh * g + gi], sk)
        m = jnp.maximum(jnp.maximum(jnp.max(s, axis=-1, keepdims=True), jnp.max(sx, axis=-1, keepdims=True)), sk)
        p = jnp.exp(s - m)
        px = jnp.exp(sx - m)
        l = jnp.sum(p, axis=-1, keepdims=True) + jnp.sum(px, axis=-1, keepdims=True) + jnp.exp(sk - m)
        o = _dot(px.astype(BF16), vx_ref[kvh])
        for c, vr in enumerate((vp_ref, vc_ref, vn_ref)):
            o = o + _dot(p[:, c * blk:(c + 1) * blk].astype(BF16), vr[kvh])
        o = o / l
        o_ref[:, kvh * g * dh:(kvh + 1) * g * dh] = jnp.concatenate(
            [o[gi * blk:(gi + 1) * blk] for gi in range(g)], axis=1).astype(o_ref.dtype)


def _gqa(q, k, v, kx, vx, sink, batch, seq, ctx_len):
    hq, n, dh = q.shape
    hkv = k.shape[0]
    g = hq // hkv
    blk = GQA_BLOCK
    nb = seq // blk

    def kv_spec(c):
        return pl.BlockSpec((hkv, blk, dh), lambda b, i, s: (0, b * nb + jnp.clip(i - 1 + c, 0, nb - 1), 0))

    x_spec = pl.BlockSpec((hkv, ctx_len, dh), lambda b, i, s: (0, b, 0))
    grid_spec = pltpu.PrefetchScalarGridSpec(
        num_scalar_prefetch=1,
        grid=(batch, nb),
        in_specs=[pl.BlockSpec((hq, blk, dh), lambda b, i, s: (0, b * nb + i, 0))]
        + [kv_spec(c) for c in range(3)] * 2 + [x_spec, x_spec],
        out_specs=pl.BlockSpec((blk, hq * dh), lambda b, i, s: (b * nb + i, 0)),
    )
    return pl.pallas_call(
        functools.partial(_gqa_body, nb=nb),
        grid_spec=grid_spec,
        out_shape=jax.ShapeDtypeStruct((n, hq * dh), BF16),
        compiler_params=_cparams("arbitrary", "arbitrary"),
        name="window_gqa",
    )(sink, q, k, k, k, v, v, v, kx, vx)


def _oddeven_merge(lo, hi, r):
    step = r * 2
    if step < hi - lo:
        yield from _oddeven_merge(lo, hi, step)
        yield from _oddeven_merge(lo + r, hi, step)
        yield from [(i, i + r) for i in range(lo + r, hi - r, step)]
    else:
        yield (lo, lo + r)


def _oddeven_sort(lo, hi):
    if hi - lo >= 1:
        mid = lo + (hi - lo) // 2
        yield from _oddeven_sort(lo, mid)
        yield from _oddeven_sort(mid + 1, hi)
        yield from _oddeven_merge(lo, hi, 1)


def _exchange(t, i, j):
    t[i], t[j] = jnp.maximum(t[i], t[j]), jnp.minimum(t[i], t[j])


def _sort_tiles(tiles):
    t = list(tiles)
    for i, j in _oddeven_sort(0, PEER_TOPK - 1):
        if j < len(t):
            _exchange(t, i, j)
    return t


def _top_tiles(sorted_tiles):
    w = list(sorted_tiles)
    n = len(w)
    shift = SUBLANES // 2
    while shift:
        other = [pltpu.roll(x, shift, 0) for x in w]
        w = [jnp.maximum(w[k], other[n - 1 - k]) for k in range(n)]
        d = n // 2
        while d:
            for k in range(n):
                if not k & d:
                    _exchange(w, k, k + d)
            d //= 2
        shift //= 2
    return w


def _next_below(tiles, bound):
    m = None
    for t in tiles:
        v = jnp.where(t < bound, t, -jnp.inf)
        m = v if m is None else jnp.maximum(m, v)
    return jnp.max(m, axis=0, keepdims=True)


def _stack_sublanes(vals, sub):
    out = vals[0]
    for j in range(1, len(vals)):
        out = jnp.where(sub == j, vals[j], out)
    return out


def _bf16_rounded(v):
    return v.astype(BF16).astype(F32)


def _peer_route_body(x_ref, sh_ref, sc_ref, wq_ref, keys_ref, xm_ref, kap_ref, e1_ref, p2_ref):
    xm_ref, kap_ref, e1_ref, p2_ref = (r.at[0] for r in (xm_ref, kap_ref, e1_ref, p2_ref))
    nheads = p2_ref.shape[0]
    nk = keys_ref.shape[1]
    h = _ln(x_ref[...]) * (1.0 + sc_ref[0]) + sh_ref[0]
    ht = h.T.astype(BF16)
    xm_ref[...] = ht
    qt = _dot(wq_ref[...], ht).astype(BF16)
    qd = keys_ref.shape[2]
    k16 = PEER_TOPK
    sub = lax.broadcasted_iota(jnp.int32, (SUBLANES, 1), 0)
    for hd in range(nheads):
        halves = []
        for half in range(2):
            hp = hd * 2 + half
            s = _dot(keys_ref[hp], qt[hp * qd:(hp + 1) * qd, :])
            tiles = [s[SUBLANES * k:SUBLANES * (k + 1)] for k in range(nk // SUBLANES)]
            top = _top_tiles(_sort_tiles(tiles))
            halves.append((s, top, _next_below(tiles, top[-1])))
        (s1, t1, t1_17), (s2, t2, t2_17) = halves
        m1, m2 = t1[0][:1], t2[0][:1]
        cmax = m1 + m2
        t2_lo, t2_hi = _stack_sublanes(t2[:SUBLANES], sub), _stack_sublanes(t2[SUBLANES:], sub)
        t1_hi = _stack_sublanes(t1[SUBLANES:], sub)
        pairs = [(t1[0], t2_lo), (t1[0], t2_hi)]
        pairs += [(t1[a], jnp.where(sub < (k16 + 1) // (a + 1), t2_lo, -jnp.inf)) for a in range(1, SUBLANES)]
        pairs += [(t1_hi, t2[0])]
        pairs += [(jnp.where(sub == 0, t1[0], t1_17), jnp.where(sub == 0, t2_17, jnp.where(sub == 1, t2[0], -jnp.inf)))]
        cands = [a + b for a, b in pairs]
        ctop = _top_tiles(_sort_tiles(cands) + [jnp.full_like(cands[0], -jnp.inf)] * (k16 - len(cands)))
        c16 = ctop[-1][:1]
        tau = 0.5 * (c16 + _next_below(cands, c16))
        z = None
        for (a, b), c in zip(pairs, cands):
            picked = _bf16_rounded(jnp.exp(b - m2)) >= _bf16_rounded(jnp.exp(tau - a - m2))
            zc = jnp.where(picked, jnp.exp(c - cmax), 0.0)
            z = zc if z is None else z + zc
        z = jnp.sum(z, axis=0, keepdims=True)
        groups = (nk // SUBLANES, SUBLANES, s1.shape[1])
        kap_ref[hd] = jnp.exp(tau - s1 - m2).reshape(groups)
        e1_ref[hd] = (jnp.exp(s1 - m1) * (0.5 / z)).reshape(groups)
        p2_ref[hd] = jnp.exp(s2 - m2).astype(BF16)


def _peer_route(x, shift, scale, wq_t, keys, tokens_per_batch):
    n, d = x.shape
    tt = PEER_COLS
    per = tokens_per_batch // tt
    nh = keys.shape[0] // 2
    nk = keys.shape[1]
    big = pl.BlockSpec((1, nh, nk, tt), lambda i: (i, 0, 0, 0))
    rows = pl.BlockSpec((1, nh, nk // SUBLANES, SUBLANES, tt), lambda i: (i, 0, 0, 0, 0))
    rows_shape = jax.ShapeDtypeStruct((n // tt, nh, nk // SUBLANES, SUBLANES, tt), F32)
    return pl.pallas_call(
        _peer_route_body,
        grid=(n // tt,),
        in_specs=[
            pl.BlockSpec((tt, d), lambda i: (i, 0)),
            pl.BlockSpec((1, 1, d), lambda i: (i // per, 0, 0)),
            pl.BlockSpec((1, 1, d), lambda i: (i // per, 0, 0)),
            pl.BlockSpec(wq_t.shape, lambda i: (0, 0)),
            pl.BlockSpec(keys.shape, lambda i: (0, 0, 0)),
        ],
        out_specs=[pl.BlockSpec((1, d, tt), lambda i: (i, 0, 0)), rows, rows, big],
        out_shape=[jax.ShapeDtypeStruct((n // tt, d, tt), BF16), rows_shape, rows_shape,
                   jax.ShapeDtypeStruct((n // tt, nh, nk, tt), BF16)],
        compiler_params=_cparams("arbitrary"),
        name="peer_route",
    )(x, shift, scale, wq_t, keys)


PEER_UNIT_ROWS = 4


def _peer_expert_body(xm_ref, u_ref, vt_ref, kap_ref, e1_ref, p2_ref,
                      x_ref, gate_ref, g_ref, b_ref, o_ref, acc_ref, a_ref, w_ref, *, alpha):
    e = pl.program_id(1)
    ncb, nheads, ngroups, _, cw = kap_ref.shape
    nk = p2_ref.shape[2]
    ur = PEER_UNIT_ROWS
    ue = ur * nk
    nrp = ngroups * SUBLANES // ur
    n_units = ncb * nrp
    assert 2 * ur == SUBLANES and nrp % 2 == 0

    @pl.when(e == 0)
    def _():
        acc_ref[...] = jnp.zeros_like(acc_ref)

    def first_matmul(i, slot):
        rows = pl.ds(pl.multiple_of((i % nrp) * ue, ue), ue)
        a_ref[slot] = _dot(u_ref[rows, :], xm_ref[i // nrp])

    def second_matmul(i, slot):
        acc_ref[i // nrp] += _dot(vt_ref[i % nrp], w_ref[slot])

    def gate_and_activate(i, slot):
        c, grp = i // nrp, (i % nrp) // 2
        for r in range(ur):
            row = slot * ur + r
            gsum = None
            for hd in range(nheads):
                p2 = p2_ref[c, hd]
                keep = p2 >= kap_ref[c, hd, grp, row:row + 1, :].astype(BF16)
                gate = p2 * e1_ref[c, hd, grp, row:row + 1, :].astype(BF16)
                term = jnp.where(keep, gate, jnp.zeros_like(p2))
                gsum = term if gsum is None else gsum + term
            a = a_ref[slot, r * nk:(r + 1) * nk, :]
            act = (a * (1.0 + lax.erf(a * (2.0 ** -0.5)))).astype(BF16)
            w_ref[slot, r * nk:(r + 1) * nk, :] = gsum * act

    def steady(j, carry):
        i = 2 * j + 1
        first_matmul(i + 1, 0)
        gate_and_activate(i, 1)
        second_matmul(i - 1, 0)
        first_matmul(i + 2, 1)
        gate_and_activate(i + 1, 0)
        second_matmul(i, 1)
        return carry

    assert n_units % 2 == 0
    first_matmul(0, 0)
    first_matmul(1, 1)
    gate_and_activate(0, 0)
    lax.fori_loop(0, n_units // 2 - 1, steady, 0, unroll=3)
    gate_and_activate(n_units - 1, 1)
    second_matmul(n_units - 2, 0)
    second_matmul(n_units - 1, 1)

    @pl.when(e == pl.num_programs(1) - 1)
    def _():
        for c in range(ncb):
            f = acc_ref[c].T
            z = alpha * x_ref[c * cw:(c + 1) * cw, :] + gate_ref[0] * f
            o_ref[c * cw:(c + 1) * cw, :] = _ln(z) * g_ref[...] + b_ref[...]


def _peer_experts(xm_t, u, v_t, kap, e1, p2, x, gate, g, b, tokens_per_batch, alpha, tt=1024, et=2048):
    n, d = x.shape
    tt = min(tt, tokens_per_batch)
    per = tokens_per_batch // tt
    _, nh, nk, cw = p2.shape
    ncb = tt // cw
    n1 = et // nk
    ue = PEER_UNIT_ROWS * nk
    sel = pl.BlockSpec((ncb, nh, n1 // SUBLANES, SUBLANES, cw), lambda i, e: (i, 0, e, 0, 0))
    full = pl.BlockSpec((ncb, nh, nk, cw), lambda i, e: (i, 0, 0, 0))
    return pl.pallas_call(
        functools.partial(_peer_expert_body, alpha=alpha),
        grid=(n // tt, u.shape[0] // et),
        in_specs=[
            pl.BlockSpec((ncb, d, cw), lambda i, e: (i, 0, 0)),
            pl.BlockSpec((et, d), lambda i, e: (e, 0)),
            pl.BlockSpec((et // ue, d, ue), lambda i, e: (e, 0, 0)),
            sel, sel, full,
            pl.BlockSpec((tt, d), lambda i, e: (i, 0)),
            pl.BlockSpec((1, 1, d), lambda i, e: (i // per, 0, 0)),
            pl.BlockSpec((1, d), lambda i, e: (0, 0)),
            pl.BlockSpec((1, d), lambda i, e: (0, 0)),
        ],
        out_specs=pl.BlockSpec((tt, d), lambda i, e: (i, 0)),
        out_shape=jax.ShapeDtypeStruct((n, d), F32),
        scratch_shapes=[
            pltpu.VMEM((ncb, d, cw), F32),
            pltpu.VMEM((2, ue, cw), F32),
            pltpu.VMEM((2, ue, cw), BF16),
        ],
        compiler_params=_cparams("arbitrary", "arbitrary"),
        name="peer_experts",
    )(xm_t, u, v_t, kap, e1, p2, x, gate, g.reshape(1, d), b.reshape(1, d))


def _peer_layer(x, shift, scale, gate, g, b, tables, tokens_per_batch, alpha):
    wq_t, keys, u, v_t = tables
    xm_t, kap, e1, p2 = _peer_route(x, shift, scale, wq_t, keys, tokens_per_batch)
    return _peer_experts(xm_t, u, v_t, kap, e1, p2, x, gate, g, b, tokens_per_batch, alpha)


def _rope_tables(seq):
    t = jnp.arange(seq)
    row = (t // GRID_W).astype(F32)
    col = (t % GRID_W).astype(F32)
    n_freq = HEAD_DIM // 4
    inv_freq = ROPE_THETA ** (-jnp.arange(n_freq, dtype=F32) / n_freq)
    ang = jnp.concatenate([row[:, None] * inv_freq, col[:, None] * inv_freq], -1)
    cos, sin = jnp.cos(ang), jnp.sin(ang)
    reps = LANES // (HEAD_DIM // 2)
    return jnp.tile(cos, (1, reps)), jnp.tile(sin, (1, reps))


def _rotate_half_columns(w, n_heads):
    d = w.shape[0]
    wh = w.reshape(d, n_heads, 2, HEAD_DIM // 2)
    return jnp.concatenate([-wh[:, :, 1], wh[:, :, 0]], axis=-1).reshape(d, n_heads * HEAD_DIM)


def _expert_table_body(u_ref, v_ref, ub_ref, vt_ref):
    ub_ref[...] = u_ref[0].astype(BF16)
    vt_ref[0] = v_ref[0].T.astype(BF16)


def _peer_tables(w_q, sub_keys, u_all, v_all, layer):
    nh, _, nk, qd = sub_keys.shape
    _, ne, d = u_all.shape
    ue = PEER_UNIT_ROWS * nk
    u_b, v_chunks = pl.pallas_call(
        _expert_table_body,
        grid=(ne // ue,),
        in_specs=[pl.BlockSpec((1, ue, d), lambda i: (layer, i, 0)), pl.BlockSpec((1, ue, d), lambda i: (layer, i, 0))],
        out_specs=[pl.BlockSpec((ue, d), lambda i: (i, 0)), pl.BlockSpec((1, d, ue), lambda i: (i, 0, 0))],
        out_shape=[jax.ShapeDtypeStruct((ne, d), BF16), jax.ShapeDtypeStruct((ne // ue, d, ue), BF16)],
        compiler_params=_cparams("arbitrary"),
        name="expert_table_layout",
    )(u_all, v_all)
    return (w_q.T.astype(BF16), sub_keys.reshape(nh * 2, nk, qd).astype(BF16), u_b, v_chunks)


def kernel(x, c, ctx, c_ctx, ada_w, ada_b, post_ln_g, post_ln_b, even_w_in, even_w_out, na_rpb,
           odd_w_in, odd_w_out, gqa_sink, peer_w_q, peer_sub_keys, peer_u, peer_v):
    batch, seq, d = x.shape
    ctx_len = ctx.shape[1]
    depth = ada_w.shape[0]
    rows = seq // GRID_W
    alpha = float((2 * depth) ** 0.25)
    fw = FNET_GROUPS * HEAD_DIM
    nw = NA_HEADS * HEAD_DIM
    qw = GQA_Q_HEADS * HEAD_DIM
    kvw = GQA_KV_HEADS * HEAD_DIM
    qscale = HEAD_DIM ** -0.5

    cond = jnp.zeros((8, d), F32).at[:batch].set(c).at[batch].set(c_ctx)
    mods = _ada(cond, ada_w, ada_b)

    xl = x.reshape(batch * seq, d)
    hc = ctx.reshape(batch * ctx_len, d)
    cos_t, sin_t = _rope_tables(seq)

    cg, sg = _dft_tables(HEAD_DIM)
    eye = np.eye(FNET_GROUPS)
    chan = jnp.asarray(np.concatenate([np.kron(eye, cg), -np.kron(eye, sg)], axis=1), F32)

    for layer in range(depth):
        ctx_out = layer < depth - 1
        i = layer // 2
        m_l = [m.reshape(batch, 1, d) for m in jnp.split(mods[layer, :batch], 6, axis=-1)]
        m_c = [jnp.broadcast_to(m.reshape(1, 1, d), (batch, 1, d)) for m in jnp.split(mods[layer, batch], 6, axis=-1)]
        g0, b0 = post_ln_g[layer, 0], post_ln_b[layer, 0]
        g1, b1 = post_ln_g[layer, 1], post_ln_b[layer, 1]

        if layer % 2 == 0:
            w_in, w_out = even_w_in[i], even_w_out[i]
            w_f = _matmul_f32(w_in[:, :fw], chan)
            w_aug = jnp.concatenate([w_f, w_in[:, fw:]], axis=1).astype(BF16)
            plan = (("nat", 0, 2 * fw, 1.0, None),
                    ("heads", 2 * fw, nw, qscale, None),
                    ("heads", 2 * fw + nw, nw, 1.0, None),
                    ("heads", 2 * fw + 2 * nw, nw, 1.0, None))
            f_l, q_l, k_l, v_l = _proj(xl, m_l[0], m_l[1], w_aug, plan, seq)
            f_c, q_c, k_c, v_c = _proj(hc, m_c[0], m_c[1], w_aug, plan, ctx_len)
            col_bias, row_mask = _natten_tables(rows, na_rpb[i])
            na_l = _natten(q_l, k_l, v_l, k_c, v_c, col_bias, row_mask, batch, rows, ctx_len)
            fm_l = _fourier_latent(f_l, batch, rows, fw)
            w_out_b = w_out.astype(BF16)
            ws = [w_out_b[:fw], w_out_b[fw:]]
            xl_new = _outproj_ln([fm_l, na_l], ws, xl, m_l[2], g0, b0, seq, alpha)
            if ctx_out:
                na_c = _ctx_attn(q_c, k_c, v_c, batch, ctx_len)
                fm_c = _fourier_dense(f_c, batch, ctx_len, fw)
                hc_new = _outproj_ln([fm_c, na_c], ws, hc, m_c[2], g0, b0, ctx_len, alpha)
        else:
            w_in, w_out = odd_w_in[i], odd_w_out[i]
            wq, wk, wv = w_in[:, :qw], w_in[:, qw:qw + kvw], w_in[:, qw + kvw:]
            w_aug = jnp.concatenate([wq, wk, wv, _rotate_half_columns(wq, GQA_Q_HEADS),
                                     _rotate_half_columns(wk, GQA_KV_HEADS)], axis=1).astype(BF16)
            plan = (("heads", 0, qw, qscale, qw + 2 * kvw),
                    ("heads", qw, kvw, 1.0, 2 * qw + 2 * kvw),
                    ("heads", qw + kvw, kvw, 1.0, None))
            q_l, k_l, v_l = _proj(xl, m_l[0], m_l[1], w_aug, plan, seq, rope=(cos_t, sin_t))
            plan_c = (("heads", 0, kvw, 1.0, None), ("heads", kvw, kvw, 1.0, None))
            if ctx_out:
                raise NotImplementedError("an odd layer must be the last layer (no context output path)")
            k_c, v_c = _proj(hc, m_c[0], m_c[1], w_in[:, qw:].astype(BF16), plan_c, ctx_len)
            y_l = _gqa(q_l, k_l, v_l, k_c, v_c, gqa_sink[i], batch, seq, ctx_len)
            xl_new = _outproj_ln([y_l], [w_out.astype(BF16)], xl, m_l[2], g0, b0, seq, alpha)

        tables = _peer_tables(peer_w_q[layer], peer_sub_keys[layer], peer_u, peer_v, layer)
        xl = _peer_layer(xl_new, m_l[3], m_l[4], m_l[5], g1, b1, tables, seq, alpha)
        if ctx_out:
            hc = _peer_layer(hc_new, m_c[3][:1], m_c[4][:1], m_c[5][:1], g1, b1, tables, batch * ctx_len, alpha)

    return xl.reshape(batch, seq, d)
```

```python
import functools

import numpy as np
import jax
import jax.numpy as jnp
from jax import lax
from jax.experimental import pallas as pl
from jax.experimental.pallas import tpu as pltpu

F32 = jnp.float32
BF16 = jnp.bfloat16

HEAD_DIM = 64
GRID_W = 64
FNET_GROUPS = 8
NA_HEADS = 8
NA_KR = 8
NA_KC = 16
NA_ROWS_PER_BLOCK = 8
NA_WIN_ROWS = 16
GQA_Q_HEADS = 16
GQA_KV_HEADS = 4
GQA_WINDOW = 128
GQA_BLOCK = 128
ROPE_THETA = 10000.0
PEER_TOPK = 16
LN_EPS = 1e-6
NEG = -1e30

PEER_COLS = 256
LANES = 128
SUBLANES = 8
VMEM_LIMIT = 56 * 1024 * 1024


def _cparams(*sem):
    return pltpu.CompilerParams(dimension_semantics=sem, vmem_limit_bytes=VMEM_LIMIT)


def _ln(x):
    mu = jnp.mean(x, axis=-1, keepdims=True)
    xc = x - mu
    var = jnp.mean(xc * xc, axis=-1, keepdims=True)
    return xc * lax.rsqrt(var + LN_EPS)


def _dot(a, b):
    return jnp.dot(a, b, preferred_element_type=F32)


def _dot_nt(a, b):
    return lax.dot_general(a, b, (((1,), (1,)), ((), ())), preferred_element_type=F32)


def _ada_body(c_ref, w_ref, b_ref, o_ref):
    c = c_ref[...]
    o_ref[0] = _dot(c * jax.nn.sigmoid(c), w_ref[0]) + b_ref[0]


def _ada(cond, ada_w, ada_b):
    depth, d, n = ada_w.shape
    tn = 1536
    return pl.pallas_call(
        _ada_body,
        grid=(depth, n // tn),
        in_specs=[
            pl.BlockSpec((8, d), lambda l, j: (0, 0)),
            pl.BlockSpec((1, d, tn), lambda l, j: (l, 0, j)),
            pl.BlockSpec((1, 1, tn), lambda l, j: (l, 0, j)),
        ],
        out_specs=pl.BlockSpec((1, 8, tn), lambda l, j: (l, 0, j)),
        out_shape=jax.ShapeDtypeStruct((depth, 8, n), F32),
        compiler_params=_cparams("arbitrary", "arbitrary"),
        name="ada_modulation",
    )(cond, ada_w, ada_b.reshape(depth, 1, n))


def _matmul_f32_body(a_ref, b_ref, o_ref):
    o_ref[...] = jnp.dot(a_ref[...], b_ref[...], preferred_element_type=F32, precision=lax.Precision.HIGHEST)


def _matmul_f32(a, b):
    m, k = a.shape
    n = b.shape[1]
    tm = min(256, m)
    return pl.pallas_call(
        _matmul_f32_body,
        grid=(m // tm,),
        in_specs=[pl.BlockSpec((tm, k), lambda i: (i, 0)), pl.BlockSpec((k, n), lambda i: (0, 0))],
        out_specs=pl.BlockSpec((tm, n), lambda i: (i, 0)),
        out_shape=jax.ShapeDtypeStruct((m, n), F32),
        compiler_params=_cparams("arbitrary"),
        name="small_matmul_f32",
    )(a, b)


def _proj_body(*refs, plan, use_rope):
    x_ref, sh_ref, sc_ref, w_ref = refs[:4]
    rest = refs[4:]
    if use_rope:
        cos_ref, sin_ref = rest[:2]
        rest = rest[2:]
    h = _ln(x_ref[...]) * (1.0 + sc_ref[0]) + sh_ref[0]
    acc = _dot(h.astype(BF16), w_ref[...])
    for o_ref, (kind, start, width, scale, rotary) in zip(rest, plan):
        y = acc[:, start:start + width]
        if rotary:
            half = HEAD_DIM // 2
            first = lax.broadcasted_iota(jnp.int32, (1, LANES), 1) % HEAD_DIM < half
            cos, sin = cos_ref[...], sin_ref[...]
            pieces = []
            for j in range(width // LANES):
                yj = y[:, j * LANES:(j + 1) * LANES]
                swapped = jnp.where(first, pltpu.roll(yj, LANES - half, 1), pltpu.roll(yj, half, 1))
                pieces.append(yj * cos + swapped * sin)
            y = jnp.concatenate(pieces, axis=1)
        if scale != 1.0:
            y = y * scale
        if kind == "nat":
            o_ref[...] = y.astype(o_ref.dtype)
        else:
            for hh in range(width // HEAD_DIM):
                o_ref[hh] = y[:, hh * HEAD_DIM:(hh + 1) * HEAD_DIM].astype(o_ref.dtype)


def _proj(x, shift, scale, w, plan, tokens_per_batch, rope=None, tm=512):
    n, d = x.shape
    tm = min(tm, tokens_per_batch)
    per = tokens_per_batch // tm
    in_specs = [
        pl.BlockSpec((tm, d), lambda i: (i, 0)),
        pl.BlockSpec((1, 1, d), lambda i: (i // per, 0, 0)),
        pl.BlockSpec((1, 1, d), lambda i: (i // per, 0, 0)),
        pl.BlockSpec(w.shape, lambda i: (0, 0)),
    ]
    args = [x, shift, scale, w]
    if rope is not None:
        in_specs += [pl.BlockSpec((tm, LANES), lambda i: (i % per, 0))] * 2
        args += list(rope)
    out_specs, out_shape = [], []
    for kind, start, width, sc, rot in plan:
        if kind == "nat":
            out_specs.append(pl.BlockSpec((tm, width), lambda i: (i, 0)))
            out_shape.append(jax.ShapeDtypeStruct((n, width), F32))
        else:
            nh = width // HEAD_DIM
            out_specs.append(pl.BlockSpec((nh, tm, HEAD_DIM), lambda i: (0, i, 0)))
            out_shape.append(jax.ShapeDtypeStruct((nh, n, HEAD_DIM), BF16))
    return pl.pallas_call(
        functools.partial(_proj_body, plan=plan, use_rope=rope is not None),
        grid=(n // tm,),
        in_specs=in_specs,
        out_specs=out_specs,
        out_shape=out_shape,
        compiler_params=_cparams("arbitrary"),
        name="modln_proj",
    )(*args)


def _outproj_body(*refs, n_in, alpha):
    ys, ws = refs[:n_in], refs[n_in:2 * n_in]
    x_ref, gate_ref, g_ref, b_ref, o_ref = refs[2 * n_in:]
    acc = None
    for y_ref, w_ref in zip(ys, ws):
        t = _dot(y_ref[...].astype(BF16), w_ref[...])
        acc = t if acc is None else acc + t
    z = alpha * x_ref[...] + gate_ref[0] * acc
    o_ref[...] = _ln(z) * g_ref[...] + b_ref[...]


def _outproj_ln(ys, ws, x, gate, g, b, tokens_per_batch, alpha, tm=512):
    n, d = x.shape
    tm = min(tm, tokens_per_batch)
    per = tokens_per_batch // tm
    in_specs = [pl.BlockSpec((tm, y.shape[1]), lambda i: (i, 0)) for y in ys]
    in_specs += [pl.BlockSpec(w.shape, lambda i: (0, 0)) for w in ws]
    in_specs += [
        pl.BlockSpec((tm, d), lambda i: (i, 0)),
        pl.BlockSpec((1, 1, d), lambda i: (i // per, 0, 0)),
        pl.BlockSpec((1, d), lambda i: (0, 0)),
        pl.BlockSpec((1, d), lambda i: (0, 0)),
    ]
    return pl.pallas_call(
        functools.partial(_outproj_body, n_in=len(ys), alpha=alpha),
        grid=(n // tm,),
        in_specs=in_specs,
        out_specs=pl.BlockSpec((tm, d), lambda i: (i, 0)),
        out_shape=jax.ShapeDtypeStruct((n, d), F32),
        compiler_params=_cparams("arbitrary"),
        name="outproj_residual_ln",
    )(*ys, *ws, x, gate, g.reshape(1, d), b.reshape(1, d))


def _dft_tables(n):
    idx = np.arange(n)
    ang = 2.0 * np.pi * ((idx[:, None] * idx[None, :]) % n) / n
    return np.cos(ang), np.sin(ang)


def _fourier_rows_body(x_ref, cs_ref, tc_ref, ts_ref, o_ref, *, cb, width):
    nr = cs_ref.shape[1]
    for j in range(cb):
        pq = _dot(cs_ref[...], x_ref[0, :, j, :])
        yr = pq[:nr, :width] + pq[nr:, width:]
        yi = pq[:nr, width:] - pq[nr:, :width]
        tc, ts = tc_ref[j], ts_ref[j]
        o_ref[0, :, j, :width] = yr * tc + yi * ts
        o_ref[0, :, j, width:] = yi * tc - yr * ts


def _fourier_cols_body(y_ref, cs_ref, o_ref, *, kb, width, norm):
    nc = cs_ref.shape[1]
    for j in range(kb):
        pq = _dot(cs_ref[...], y_ref[0, j])
        o_ref[0, :, j, :] = (pq[:nc, :width] + pq[nc:, width:]) * norm


def _fourier_latent(f, batch, rows, width):
    cols = GRID_W
    seq = rows * cols
    c_r, s_r = _dft_tables(rows)
    c_c, s_c = _dft_tables(cols)
    k1 = np.arange(rows)[None, :]
    cc = np.arange(cols)[:, None]
    tw = 2.0 * np.pi * ((cc * k1) % seq) / seq
    cs_r = jnp.asarray(np.concatenate([c_r, s_r], 0), F32)
    cs_c = jnp.asarray(np.concatenate([c_c, s_c], 0), F32)
    tc = jnp.asarray(np.cos(tw)[:, :, None], F32)
    ts = jnp.asarray(np.sin(tw)[:, :, None], F32)
    cb = SUBLANES
    kb = SUBLANES
    lane_w = 2 * width
    y = pl.pallas_call(
        functools.partial(_fourier_rows_body, cb=cb, width=width),
        grid=(batch, cols // cb),
        in_specs=[
            pl.BlockSpec((1, rows, cb, lane_w), lambda b, j: (b, 0, j, 0)),
            pl.BlockSpec((2 * rows, rows), lambda b, j: (0, 0)),
            pl.BlockSpec((cb, rows, 1), lambda b, j: (j, 0, 0)),
            pl.BlockSpec((cb, rows, 1), lambda b, j: (j, 0, 0)),
        ],
        out_specs=pl.BlockSpec((1, rows, cb, lane_w), lambda b, j: (b, 0, j, 0)),
        out_shape=jax.ShapeDtypeStruct((batch, rows, cols, lane_w), F32),
        compiler_params=_cparams("arbitrary", "arbitrary"),
        name="fourier_rows",
    )(f.reshape(batch, rows, cols, lane_w), cs_r, tc, ts)
    z = pl.pallas_call(
        functools.partial(_fourier_cols_body, kb=kb, width=width, norm=float((seq * HEAD_DIM) ** -0.5)),
        grid=(batch, rows // kb),
        in_specs=[
            pl.BlockSpec((1, kb, cols, lane_w), lambda b, j: (b, j, 0, 0)),
            pl.BlockSpec((2 * cols, cols), lambda b, j: (0, 0)),
        ],
        out_specs=pl.BlockSpec((1, cols, kb, width), lambda b, j: (b, 0, j, 0)),
        out_shape=jax.ShapeDtypeStruct((batch, cols, rows, width), F32),
        compiler_params=_cparams("arbitrary", "arbitrary"),
        name="fourier_cols",
    )(y, cs_c)
    return z.reshape(batch * seq, width)


def _fourier_dense_body(x_ref, c_ref, s_ref, o_ref, *, width, norm):
    x = x_ref[...]
    o_ref[...] = (_dot(c_ref[...], x[:, :width]) + _dot(s_ref[...], x[:, width:])) * norm


def _fourier_dense(f, batch, length, width):
    c, s = _dft_tables(length)
    return pl.pallas_call(
        functools.partial(_fourier_dense_body, width=width, norm=float((length * HEAD_DIM) ** -0.5)),
        grid=(batch,),
        in_specs=[
            pl.BlockSpec((length, 2 * width), lambda b: (b, 0)),
            pl.BlockSpec((length, length), lambda b: (0, 0)),
            pl.BlockSpec((length, length), lambda b: (0, 0)),
        ],
        out_specs=pl.BlockSpec((length, width), lambda b: (b, 0)),
        out_shape=jax.ShapeDtypeStruct((batch * length, width), F32),
        compiler_params=_cparams("arbitrary"),
        name="fourier_dense",
    )(f, jnp.asarray(c, F32), jnp.asarray(s, F32))


def _natten_tables(rows, rpb):
    w = GRID_W
    qb, kw = NA_ROWS_PER_BLOCK, NA_WIN_ROWS
    nh, ndr, ndc = rpb.shape
    tq, tk = np.divmod(np.arange(w * w), w)
    dc_i = np.clip(tk - tq, -(NA_KC - 1), NA_KC - 1) + (NA_KC - 1)
    onehot = np.zeros((LANES, w * w), np.float32)
    onehot[dc_i, np.arange(w * w)] = 1.0
    c_start = np.clip(tq - NA_KC // 2, 0, w - NA_KC)
    col_ok = ((tk >= c_start) & (tk < c_start + NA_KC)).reshape(w, w)
    rp = jnp.zeros((LANES, LANES), F32).at[:nh * ndr, :ndc].set(rpb.reshape(nh * ndr, ndc).astype(F32))
    tiles = _matmul_f32(rp, jnp.asarray(onehot))[:nh * ndr].reshape(nh, ndr, w, w)
    tiles = jnp.where(jnp.asarray(col_ok), tiles, NEG)
    dr = np.arange(kw)[None, :] - np.arange(qb)[:, None] - (kw - qb) // 2
    dr_i = np.clip(dr + (NA_KR - 1), 0, ndr - 1)
    col_bias = jnp.transpose(tiles[:, dr_i], (0, 1, 3, 2, 4)).reshape(nh, qb * w, kw * w)
    qi = np.repeat(np.arange(qb), w)
    km = np.repeat(np.arange(kw), w)
    nblk = rows // qb
    masks = []
    for j in range(nblk):
        qr = j * qb + qi
        kr = j * qb - (kw - qb) // 2 + km
        r_start = np.clip(qr - NA_KR // 2, 0, rows - NA_KR)
        ok = (kr[None, :] >= r_start[:, None]) & (kr[None, :] < r_start[:, None] + NA_KR)
        masks.append(np.where(ok, 0.0, NEG).astype(np.float32))
    kinds = [(j > 0) + (j == nblk - 1) for j in range(nblk)]
    table = np.zeros((3,) + masks[0].shape, np.float32)
    for j, kd in enumerate(kinds):
        table[kd] = masks[j]
    for j, kd in enumerate(kinds):
        assert np.array_equal(table[kd], masks[j])
    half, quarter = qb * w // 2, kw * w // 4
    for mask in masks:
        assert (mask[:half, 3 * quarter:] == NEG).all() and (mask[half:, :quarter] == NEG).all()
    return col_bias, jnp.asarray(table)


def _natten_body(q_ref, k0_ref, k1_ref, k2_ref, k3_ref, v0_ref, v1_ref, v2_ref, v3_ref,
                 kc_ref, vc_ref, cb_ref, rm_ref, o_ref):
    k_refs = (k0_ref, k1_ref, k2_ref, k3_ref)
    v_refs = (v0_ref, v1_ref, v2_ref, v3_ref)
    qt, kb = q_ref.shape[1], k0_ref.shape[1]
    hq = qt // 2
    outs = []
    for hh in range(q_ref.shape[0]):
        halves = []
        for part in range(2):
            rows = slice(part * hq, (part + 1) * hq)
            cols = slice(part * kb, (part + 3) * kb)
            q = q_ref[hh, rows, :]
            s = jnp.concatenate([_dot_nt(q, kr[hh]) for kr in k_refs[part:part + 3]], axis=1)
            s = s + cb_ref[hh, rows, cols] + rm_ref[0, rows, cols]
            sc = _dot_nt(q, kc_ref[hh])
            m = jnp.maximum(jnp.max(s, axis=-1, keepdims=True), jnp.max(sc, axis=-1, keepdims=True))
            p = jnp.exp(s - m)
            pc = jnp.exp(sc - m)
            l = jnp.sum(p, axis=-1, keepdims=True) + jnp.sum(pc, axis=-1, keepdims=True)
            o = _dot(pc.astype(BF16), vc_ref[hh])
            for c, vr in enumerate(v_refs[part:part + 3]):
                o = o + _dot(p[:, c * kb:(c + 1) * kb].astype(BF16), vr[hh])
            halves.append(o / l)
        outs.append(jnp.concatenate(halves, axis=0))
    o_ref[...] = jnp.concatenate(outs, axis=1).astype(o_ref.dtype)


def _natten(q, k, v, kc, vc, col_bias, row_mask, batch, rows, ctx_len):
    nh, n, dh = q.shape
    hp = 4
    qt = NA_ROWS_PER_BLOCK * GRID_W
    kt = NA_WIN_ROWS * GRID_W // 4
    nblk = rows // NA_ROWS_PER_BLOCK
    kblocks = rows * GRID_W // kt

    def kv_spec(c):
        return pl.BlockSpec(
            (hp, kt, dh),
            lambda b, j, h: (h, b * kblocks + jnp.clip(2 * j - 1 + c, 0, kblocks - 1), 0))

    in_specs = [pl.BlockSpec((hp, qt, dh), lambda b, j, h: (h, b * nblk + j, 0))]
    in_specs += [kv_spec(c) for c in range(4)] * 2
    in_specs += [
        pl.BlockSpec((hp, ctx_len, dh), lambda b, j, h: (h, b, 0)),
        pl.BlockSpec((hp, ctx_len, dh), lambda b, j, h: (h, b, 0)),
        pl.BlockSpec((hp, qt, 4 * kt), lambda b, j, h: (h, 0, 0)),
        pl.BlockSpec((1, qt, 4 * kt), lambda b, j, h: ((j > 0).astype(jnp.int32) + (j == nblk - 1).astype(jnp.int32), 0, 0)),
    ]
    return pl.pallas_call(
        _natten_body,
        grid=(batch, nblk, nh // hp),
        in_specs=in_specs,
        out_specs=pl.BlockSpec((qt, hp * dh), lambda b, j, h: (b * nblk + j, h)),
        out_shape=jax.ShapeDtypeStruct((n, nh * dh), BF16),
        compiler_params=_cparams("arbitrary", "arbitrary", "arbitrary"),
        name="neighbourhood_attention",
    )(q, k, k, k, k, v, v, v, v, kc, vc, col_bias, row_mask)


def _ctx_attn_body(q_ref, k_ref, v_ref, o_ref):
    outs = []
    for hh in range(q_ref.shape[0]):
        s = _dot_nt(q_ref[hh], k_ref[hh])
        m = jnp.max(s, axis=-1, keepdims=True)
        p = jnp.exp(s - m)
        l = jnp.sum(p, axis=-1, keepdims=True)
        outs.append(_dot(p.astype(BF16), v_ref[hh]) / l)
    o_ref[...] = jnp.concatenate(outs, axis=1).astype(o_ref.dtype)


def _ctx_attn(q, k, v, batch, ctx_len):
    nh, n, dh = q.shape
    hp = 2
    spec = pl.BlockSpec((hp, ctx_len, dh), lambda b, h: (h, b, 0))
    return pl.pallas_call(
        _ctx_attn_body,
        grid=(batch, nh // hp),
        in_specs=[spec, spec, spec],
        out_specs=pl.BlockSpec((ctx_len, hp * dh), lambda b, h: (b, h)),
        out_shape=jax.ShapeDtypeStruct((n, nh * dh), BF16),
        compiler_params=_cparams("arbitrary", "arbitrary"),
        name="context_attention",
    )(q, k, v)


def _gqa_body(sink_ref, q_ref, kp_ref, kc_ref, kn_ref, vp_ref, vc_ref, vn_ref, kx_ref, vx_ref, o_ref, *, nb):
    hkv, blk, dh = kc_ref.shape
    g = q_ref.shape[0] // hkv
    n = pl.program_id(1)
    shape = (g * blk, 3 * blk)
    qi = lax.broadcasted_iota(jnp.int32, shape, 0) % blk
    kp = lax.broadcasted_iota(jnp.int32, shape, 1) - blk
    ok = (jnp.abs(qi - kp) <= GQA_WINDOW) & ((kp >= 0) | (n > 0)) & ((kp < blk) | (n < nb - 1))
    grp = lax.broadcasted_iota(jnp.int32, (g * blk, 1), 0) // blk
    for kvh in range(hkv):
        q = q_ref[kvh * g:(kvh + 1) * g].reshape(g * blk, dh)
        s = jnp.concatenate([_dot_nt(q, r[kvh]) for r in (kp_ref, kc_ref, kn_ref)], axis=1)
        s = jnp.where(ok, s, NEG)
        sx = _dot_nt(q, kx_ref[kvh])
        sk = jnp.zeros((g * blk, 1), F32)
        for gi in range(g):
            sk = jnp.where(grp == gi, sink_ref[kvh * g + gi], sk)
        m = jnp.maximum(jnp.maximum(jnp.max(s, axis=-1, keepdims=True), jnp.max(sx, axis=-1, keepdims=True)), sk)
        p = jnp.exp(s - m)
        px = jnp.exp(sx - m)
        l = jnp.sum(p, axis=-1, keepdims=True) + jnp.sum(px, axis=-1, keepdims=True) + jnp.exp(sk - m)
        o = _dot(px.astype(BF16), vx_ref[kvh])
        for c, vr in enumerate((vp_ref, vc_ref, vn_ref)):
            o = o + _dot(p[:, c * blk:(c + 1) * blk].astype(BF16), vr[kvh])
        o = o / l
        o_ref[:, kvh * g * dh:(kvh + 1) * g * dh] = jnp.concatenate(
            [o[gi * blk:(gi + 1) * blk] for gi in range(g)], axis=1).astype(o_ref.dtype)


def _gqa(q, k, v, kx, vx, sink, batch, seq, ctx_len):
    hq, n, dh = q.shape
    hkv = k.shape[0]
    g = hq // hkv
    blk = GQA_BLOCK
    nb = seq // blk

    def kv_spec(c):
        return pl.BlockSpec((hkv, blk, dh), lambda b, i, s: (0, b * nb + jnp.clip(i - 1 + c, 0, nb - 1), 0))

    x_spec = pl.BlockSpec((hkv, ctx_len, dh), lambda b, i, s: (0, b, 0))
    grid_spec = pltpu.PrefetchScalarGridSpec(
        num_scalar_prefetch=1,
        grid=(batch, nb),
        in_specs=[pl.BlockSpec((hq, blk, dh), lambda b, i, s: (0, b * nb + i, 0))]
        + [kv_spec(c) for c in range(3)] * 2 + [x_spec, x_spec],
        out_specs=pl.BlockSpec((blk, hq * dh), lambda b, i, s: (b * nb + i, 0)),
    )
    return pl.pallas_call(
        functools.partial(_gqa_body, nb=nb),
        grid_spec=grid_spec,
        out_shape=jax.ShapeDtypeStruct((n, hq * dh), BF16),
        compiler_params=_cparams("arbitrary", "arbitrary"),
        name="window_gqa",
    )(sink, q, k, k, k, v, v, v, kx, vx)


def _oddeven_merge(lo, hi, r):
    step = r * 2
    if step < hi - lo:
        yield from _oddeven_merge(lo, hi, step)
        yield from _oddeven_merge(lo + r, hi, step)
        yield from [(i, i + r) for i in range(lo + r, hi - r, step)]
    else:
        yield (lo, lo + r)


def _oddeven_sort(lo, hi):
    if hi - lo >= 1:
        mid = lo + (hi - lo) // 2
        yield from _oddeven_sort(lo, mid)
        yield from _oddeven_sort(mid + 1, hi)
        yield from _oddeven_merge(lo, hi, 1)


def _exchange(t, i, j):
    t[i], t[j] = jnp.maximum(t[i], t[j]), jnp.minimum(t[i], t[j])


def _sort_tiles(tiles):
    t = list(tiles)
    for i, j in _oddeven_sort(0, PEER_TOPK - 1):
        if j < len(t):
            _exchange(t, i, j)
    return t


def _top_tiles(sorted_tiles):
    w = list(sorted_tiles)
    n = len(w)
    shift = SUBLANES // 2
    while shift:
        other = [pltpu.roll(x, shift, 0) for x in w]
        w = [jnp.maximum(w[k], other[n - 1 - k]) for k in range(n)]
        d = n // 2
        while d:
            for k in range(n):
                if not k & d:
                    _exchange(w, k, k + d)
            d //= 2
        shift //= 2
    return w


def _next_below(tiles, bound):
    m = None
    for t in tiles:
        v = jnp.where(t < bound, t, -jnp.inf)
        m = v if m is None else jnp.maximum(m, v)
    return jnp.max(m, axis=0, keepdims=True)


def _stack_sublanes(vals, sub):
    out = vals[0]
    for j in range(1, len(vals)):
        out = jnp.where(sub == j, vals[j], out)
    return out


def _bf16_rounded(v):
    return v.astype(BF16).astype(F32)


def _peer_route_body(x_ref, sh_ref, sc_ref, wq_ref, keys_ref, xm_ref, kap_ref, e1_ref, p2_ref):
    xm_ref, kap_ref, e1_ref, p2_ref = (r.at[0] for r in (xm_ref, kap_ref, e1_ref, p2_ref))
    nheads = p2_ref.shape[0]
    nk = keys_ref.shape[1]
    h = _ln(x_ref[...]) * (1.0 + sc_ref[0]) + sh_ref[0]
    ht = h.T.astype(BF16)
    xm_ref[...] = ht
    qt = _dot(wq_ref[...], ht).astype(BF16)
    qd = keys_ref.shape[2]
    k16 = PEER_TOPK
    sub = lax.broadcasted_iota(jnp.int32, (SUBLANES, 1), 0)
    for hd in range(nheads):
        halves = []
        for half in range(2):
            hp = hd * 2 + half
            s = _dot(keys_ref[hp], qt[hp * qd:(hp + 1) * qd, :])
            tiles = [s[SUBLANES * k:SUBLANES * (k + 1)] for k in range(nk // SUBLANES)]
            top = _top_tiles(_sort_tiles(tiles))
            halves.append((s, top, _next_below(tiles, top[-1])))
        (s1, t1, t1_17), (s2, t2, t2_17) = halves
        m1, m2 = t1[0][:1], t2[0][:1]
        cmax = m1 + m2
        t2_lo, t2_hi = _stack_sublanes(t2[:SUBLANES], sub), _stack_sublanes(t2[SUBLANES:], sub)
        t1_hi = _stack_sublanes(t1[SUBLANES:], sub)
        pairs = [(t1[0], t2_lo), (t1[0], t2_hi)]
        pairs += [(t1[a], jnp.where(sub < (k16 + 1) // (a + 1), t2_lo, -jnp.inf)) for a in range(1, SUBLANES)]
        pairs += [(t1_hi, t2[0])]
        pairs += [(jnp.where(sub == 0, t1[0], t1_17), jnp.where(sub == 0, t2_17, jnp.where(sub == 1, t2[0], -jnp.inf)))]
        cands = [a + b for a, b in pairs]
        ctop = _top_tiles(_sort_tiles(cands) + [jnp.full_like(cands[0], -jnp.inf)] * (k16 - len(cands)))
        c16 = ctop[-1][:1]
        tau = 0.5 * (c16 + _next_below(cands, c16))
        z = None
        for (a, b), c in zip(pairs, cands):
            picked = _bf16_rounded(jnp.exp(b - m2)) >= _bf16_rounded(jnp.exp(tau - a - m2))
            zc = jnp.where(picked, jnp.exp(c - cmax), 0.0)
            z = zc if z is None else z + zc
        z = jnp.sum(z, axis=0, keepdims=True)
        groups = (nk // SUBLANES, SUBLANES, s1.shape[1])
        kap_ref[hd] = jnp.exp(tau - s1 - m2).reshape(groups)
        e1_ref[hd] = (jnp.exp(s1 - m1) * (0.5 / z)).reshape(groups)
        p2_ref[hd] = jnp.exp(s2 - m2).astype(BF16)


def _peer_route(x, shift, scale, wq_t, keys, tokens_per_batch):
    n, d = x.shape
    tt = PEER_COLS
    per = tokens_per_batch // tt
    nh = keys.shape[0] // 2
    nk = keys.shape[1]
    big = pl.BlockSpec((1, nh, nk, tt), lambda i: (i, 0, 0, 0))
    rows = pl.BlockSpec((1, nh, nk // SUBLANES, SUBLANES, tt), lambda i: (i, 0, 0, 0, 0))
    rows_shape = jax.ShapeDtypeStruct((n // tt, nh, nk // SUBLANES, SUBLANES, tt), F32)
    return pl.pallas_call(
        _peer_route_body,
        grid=(n // tt,),
        in_specs=[
            pl.BlockSpec((tt, d), lambda i: (i, 0)),
            pl.BlockSpec((1, 1, d), lambda i: (i // per, 0, 0)),
            pl.BlockSpec((1, 1, d), lambda i: (i // per, 0, 0)),
            pl.BlockSpec(wq_t.shape, lambda i: (0, 0)),
            pl.BlockSpec(keys.shape, lambda i: (0, 0, 0)),
        ],
        out_specs=[pl.BlockSpec((1, d, tt), lambda i: (i, 0, 0)), rows, rows, big],
        out_shape=[jax.ShapeDtypeStruct((n // tt, d, tt), BF16), rows_shape, rows_shape,
                   jax.ShapeDtypeStruct((n // tt, nh, nk, tt), BF16)],
        compiler_params=_cparams("arbitrary"),
        name="peer_route",
    )(x, shift, scale, wq_t, keys)


PEER_UNIT_ROWS = 4


def _peer_expert_body(xm_ref, u_ref, vt_ref, kap_ref, e1_ref, p2_ref,
                      x_ref, gate_ref, g_ref, b_ref, o_ref, acc_ref, a_ref, w_ref, *, alpha):
    e = pl.program_id(1)
    ncb, nheads, ngroups, _, cw = kap_ref.shape
    nk = p2_ref.shape[2]
    ur = PEER_UNIT_ROWS
    ue = ur * nk
    nrp = ngroups * SUBLANES // ur
    n_units = ncb * nrp
    assert 2 * ur == SUBLANES and nrp % 2 == 0

    @pl.when(e == 0)
    def _():
        acc_ref[...] = jnp.zeros_like(acc_ref)

    def first_matmul(i, slot):
        rows = pl.ds(pl.multiple_of((i % nrp) * ue, ue), ue)
        a_ref[slot] = _dot(u_ref[rows, :], xm_ref[i // nrp])

    def second_matmul(i, slot):
        acc_ref[i // nrp] += _dot(vt_ref[i % nrp], w_ref[slot])

    def gate_and_activate(i, slot):
        c, grp = i // nrp, (i % nrp) // 2
        for r in range(ur):
            row = slot * ur + r
            gsum = None
            for hd in range(nheads):
                p2 = p2_ref[c, hd]
                keep = p2 >= kap_ref[c, hd, grp, row:row + 1, :].astype(BF16)
                gate = p2 * e1_ref[c, hd, grp, row:row + 1, :].astype(BF16)
                term = jnp.where(keep, gate, jnp.zeros_like(p2))
                gsum = term if gsum is None else gsum + term
            a = a_ref[slot, r * nk:(r + 1) * nk, :]
            act = (a * (1.0 + lax.erf(a * (2.0 ** -0.5)))).astype(BF16)
            w_ref[slot, r * nk:(r + 1) * nk, :] = gsum * act

    def steady(j, carry):
        i = 2 * j + 1
        first_matmul(i + 1, 0)
        gate_and_activate(i, 1)
        second_matmul(i - 1, 0)
        first_matmul(i + 2, 1)
        gate_and_activate(i + 1, 0)
        second_matmul(i, 1)
        return carry

    assert n_units % 2 == 0
    first_matmul(0, 0)
    first_matmul(1, 1)
    gate_and_activate(0, 0)
    lax.fori_loop(0, n_units // 2 - 1, steady, 0, unroll=3)
    gate_and_activate(n_units - 1, 1)
    second_matmul(n_units - 2, 0)
    second_matmul(n_units - 1, 1)

    @pl.when(e == pl.num_programs(1) - 1)
    def _():
        for c in range(ncb):
            f = acc_ref[c].T
            z = alpha * x_ref[c * cw:(c + 1) * cw, :] + gate_ref[0] * f
            o_ref[c * cw:(c + 1) * cw, :] = _ln(z) * g_ref[...] + b_ref[...]


def _peer_experts(xm_t, u, v_t, kap, e1, p2, x, gate, g, b, tokens_per_batch, alpha, tt=1024, et=2048):
    n, d = x.shape
    tt = min(tt, tokens_per_batch)
    per = tokens_per_batch // tt
    _, nh, nk, cw = p2.shape
    ncb = tt // cw
    n1 = et // nk
    ue = PEER_UNIT_ROWS * nk
    sel = pl.BlockSpec((ncb, nh, n1 // SUBLANES, SUBLANES, cw), lambda i, e: (i, 0, e, 0, 0))
    full = pl.BlockSpec((ncb, nh, nk, cw), lambda i, e: (i, 0, 0, 0))
    return pl.pallas_call(
        functools.partial(_peer_expert_body, alpha=alpha),
        grid=(n // tt, u.shape[0] // et),
        in_specs=[
            pl.BlockSpec((ncb, d, cw), lambda i, e: (i, 0, 0)),
            pl.BlockSpec((et, d), lambda i, e: (e, 0)),
            pl.BlockSpec((et // ue, d, ue), lambda i, e: (e, 0, 0)),
            sel, sel, full,
            pl.BlockSpec((tt, d), lambda i, e: (i, 0)),
            pl.BlockSpec((1, 1, d), lambda i, e: (i // per, 0, 0)),
            pl.BlockSpec((1, d), lambda i, e: (0, 0)),
            pl.BlockSpec((1, d), lambda i, e: (0, 0)),
        ],
        out_specs=pl.BlockSpec((tt, d), lambda i, e: (i, 0)),
        out_shape=jax.ShapeDtypeStruct((n, d), F32),
        scratch_shapes=[
            pltpu.VMEM((ncb, d, cw), F32),
            pltpu.VMEM((2, ue, cw), F32),
            pltpu.VMEM((2, ue, cw), BF16),
        ],
        compiler_params=_cparams("arbitrary", "arbitrary"),
        name="peer_experts",
    )(xm_t, u, v_t, kap, e1, p2, x, gate, g.reshape(1, d), b.reshape(1, d))


def _peer_layer(x, shift, scale, gate, g, b, tables, tokens_per_batch, alpha):
    wq_t, keys, u, v_t = tables
    xm_t, kap, e1, p2 = _peer_route(x, shift, scale, wq_t, keys, tokens_per_batch)
    return _peer_experts(xm_t, u, v_t, kap, e1, p2, x, gate, g, b, tokens_per_batch, alpha)


def _rope_tables(seq):
    t = jnp.arange(seq)
    row = (t // GRID_W).astype(F32)
    col = (t % GRID_W).astype(F32)
    n_freq = HEAD_DIM // 4
    inv_freq = ROPE_THETA ** (-jnp.arange(n_freq, dtype=F32) / n_freq)
    ang = jnp.concatenate([row[:, None] * inv_freq, col[:, None] * inv_freq], -1)
    cos, sin = jnp.cos(ang), jnp.sin(ang)
    reps = LANES // HEAD_DIM
    return (jnp.tile(jnp.concatenate([cos, cos], -1), (1, reps)),
            jnp.tile(jnp.concatenate([-sin, sin], -1), (1, reps)))


def _expert_table_body(u_ref, v_ref, ub_ref, vt_ref):
    ub_ref[...] = u_ref[0].astype(BF16)
    vt_ref[0] = v_ref[0].T.astype(BF16)


def _peer_tables(w_q, sub_keys, u_all, v_all, layer):
    nh, _, nk, qd = sub_keys.shape
    _, ne, d = u_all.shape
    ue = PEER_UNIT_ROWS * nk
    u_b, v_chunks = pl.pallas_call(
        _expert_table_body,
        grid=(ne // ue,),
        in_specs=[pl.BlockSpec((1, ue, d), lambda i: (layer, i, 0)), pl.BlockSpec((1, ue, d), lambda i: (layer, i, 0))],
        out_specs=[pl.BlockSpec((ue, d), lambda i: (i, 0)), pl.BlockSpec((1, d, ue), lambda i: (i, 0, 0))],
        out_shape=[jax.ShapeDtypeStruct((ne, d), BF16), jax.ShapeDtypeStruct((ne // ue, d, ue), BF16)],
        compiler_params=_cparams("arbitrary"),
        name="expert_table_layout",
    )(u_all, v_all)
    return (w_q.T.astype(BF16), sub_keys.reshape(nh * 2, nk, qd).astype(BF16), u_b, v_chunks)


def kernel(x, c, ctx, c_ctx, ada_w, ada_b, post_ln_g, post_ln_b, even_w_in, even_w_out, na_rpb,
           odd_w_in, odd_w_out, gqa_sink, peer_w_q, peer_sub_keys, peer_u, peer_v):
    batch, seq, d = x.shape
    ctx_len = ctx.shape[1]
    depth = ada_w.shape[0]
    rows = seq // GRID_W
    alpha = float((2 * depth) ** 0.25)
    fw = FNET_GROUPS * HEAD_DIM
    nw = NA_HEADS * HEAD_DIM
    qw = GQA_Q_HEADS * HEAD_DIM
    kvw = GQA_KV_HEADS * HEAD_DIM
    qscale = HEAD_DIM ** -0.5

    cond = jnp.zeros((8, d), F32).at[:batch].set(c).at[batch].set(c_ctx)
    mods = _ada(cond, ada_w, ada_b)

    xl = x.reshape(batch * seq, d)
    hc = ctx.reshape(batch * ctx_len, d)
    cos_t, sin_t = _rope_tables(seq)

    cg, sg = _dft_tables(HEAD_DIM)
    eye = np.eye(FNET_GROUPS)
    chan = jnp.asarray(np.concatenate([np.kron(eye, cg), -np.kron(eye, sg)], axis=1), F32)

    for layer in range(depth):
        ctx_out = layer < depth - 1
        i = layer // 2
        m_l = [m.reshape(batch, 1, d) for m in jnp.split(mods[layer, :batch], 6, axis=-1)]
        m_c = [jnp.broadcast_to(m.reshape(1, 1, d), (batch, 1, d)) for m in jnp.split(mods[layer, batch], 6, axis=-1)]
        g0, b0 = post_ln_g[layer, 0], post_ln_b[layer, 0]
        g1, b1 = post_ln_g[layer, 1], post_ln_b[layer, 1]

        if layer % 2 == 0:
            w_in, w_out = even_w_in[i], even_w_out[i]
            w_f = _matmul_f32(w_in[:, :fw], chan)
            w_aug = jnp.concatenate([w_f, w_in[:, fw:]], axis=1).astype(BF16)
            plan = (("nat", 0, 2 * fw, 1.0, False),
                    ("heads", 2 * fw, nw, qscale, False),
                    ("heads", 2 * fw + nw, nw, 1.0, False),
                    ("heads", 2 * fw + 2 * nw, nw, 1.0, False))
            f_l, q_l, k_l, v_l = _proj(xl, m_l[0], m_l[1], w_aug, plan, seq)
            f_c, q_c, k_c, v_c = _proj(hc, m_c[0], m_c[1], w_aug, plan, ctx_len)
            col_bias, row_mask = _natten_tables(rows, na_rpb[i])
            na_l = _natten(q_l, k_l, v_l, k_c, v_c, col_bias, row_mask, batch, rows, ctx_len)
            fm_l = _fourier_latent(f_l, batch, rows, fw)
            w_out_b = w_out.astype(BF16)
            ws = [w_out_b[:fw], w_out_b[fw:]]
            xl_new = _outproj_ln([fm_l, na_l], ws, xl, m_l[2], g0, b0, seq, alpha)
            if ctx_out:
                na_c = _ctx_attn(q_c, k_c, v_c, batch, ctx_len)
                fm_c = _fourier_dense(f_c, batch, ctx_len, fw)
                hc_new = _outproj_ln([fm_c, na_c], ws, hc, m_c[2], g0, b0, ctx_len, alpha)
        else:
            w_in, w_out = odd_w_in[i], odd_w_out[i]
            w_b = w_in.astype(BF16)
            plan = (("heads", 0, qw, qscale, True),
                    ("heads", qw, kvw, 1.0, True),
                    ("heads", qw + kvw, kvw, 1.0, False))
            q_l, k_l, v_l = _proj(xl, m_l[0], m_l[1], w_b, plan, seq, rope=(cos_t, sin_t))
            plan_c = (("heads", 0, kvw, 1.0, False), ("heads", kvw, kvw, 1.0, False))
            if ctx_out:
                raise NotImplementedError("an odd layer must be the last layer (no context output path)")
            k_c, v_c = _proj(hc, m_c[0], m_c[1], w_b[:, qw:], plan_c, ctx_len)
            y_l = _gqa(q_l, k_l, v_l, k_c, v_c, gqa_sink[i], batch, seq, ctx_len)
            xl_new = _outproj_ln([y_l], [w_out.astype(BF16)], xl, m_l[2], g0, b0, seq, alpha)

        tables = _peer_tables(peer_w_q[layer], peer_sub_keys[layer], peer_u, peer_v, layer)
        xl = _peer_layer(xl_new, m_l[3], m_l[4], m_l[5], g1, b1, tables, seq, alpha)
        if ctx_out:
            hc = _peer_layer(hc_new, m_c[3][:1], m_c[4][:1], m_c[5][:1], g1, b1, tables, batch * ctx_len, alpha)

    return xl.reshape(batch, seq, d)
```
